```python
import math
import jax
import jax.numpy as jnp
from jax import lax
import numpy as np

D_MODEL = 2048
BATCH = 2
SEQ = 4096
DEPTH = 4

GRID_W = 64
CTX_LEN = 256
N_MIXERS = 3
N_LAYERS_A = (DEPTH + 2) // N_MIXERS
N_LAYERS_B = (DEPTH + 1) // N_MIXERS
N_LAYERS_C = DEPTH // N_MIXERS
N_MOD = 9
D_FF = 5632
RMS_EPS = 1e-6
NEG_INF = -1e30
CHUNK = 128
A_WIDTH = D_MODEL
A_GROUPS = 16
A_GROUP_DIM = A_WIDTH // A_GROUPS
N_HEADS = 16
HEAD_DIM = D_MODEL // N_HEADS
WIN_H = 8
WIN_W = 16
C_WIDTH = D_MODEL
C_GROUP = 16
C_GROUPS = C_WIDTH // C_GROUP
C_STATE = 64
DT_MIN = 1e-3
DT_MAX = 1e-1

kernel_name = "hybrid_dit_gmlp_nat_s5"


def _rms_norm(x, gain):
    xf = x.astype(jnp.float32)
    xf = xf * lax.rsqrt(jnp.mean(xf * xf, axis=-1, keepdims=True) + RMS_EPS)
    return (xf * gain.astype(jnp.float32)).astype(x.dtype)


def _adaln(cond, w, b):
    m = jax.nn.silu(cond) @ w + b
    return m.reshape(cond.shape[0], N_MOD, D_MODEL)


def _sublayer_in(h, gain, mod, s):
    shift = mod[:, 3 * s, None, :]
    scale = mod[:, 3 * s + 1, None, :]
    return _rms_norm(h, gain) * (1.0 + scale) + shift


def _sublayer_gate(mod, s):
    return mod[:, 3 * s + 2, None, :]


def _swiglu(x, w_gu, w_down):
    g, u = jnp.split(x @ w_gu, 2, axis=-1)
    return (jax.nn.silu(g) * u) @ w_down


def _chunk_gmlp(x, w_in, v_gain, w_s, b_s, w_out):
    bsz, length, _ = x.shape
    u, v = jnp.split(jax.nn.gelu(x @ w_in), 2, axis=-1)
    v = _rms_norm(v, v_gain).reshape(bsz, length // CHUNK, CHUNK, A_GROUPS, A_GROUP_DIM)
    s = jnp.einsum("gpq,bnqgc->bnpgc", w_s, v) + b_s.T[None, None, :, :, None]
    return (u * s.reshape(bsz, length, A_WIDTH)) @ w_out


def _nat_project(h, w_qkv, q_gain, k_gain):
    bsz, length, _ = h.shape
    qkv = (h @ w_qkv).reshape(bsz, length, 3, N_HEADS, HEAD_DIM)
    return _rms_norm(qkv[:, :, 0], q_gain), _rms_norm(qkv[:, :, 1], k_gain), qkv[:, :, 2]


def _neighbourhood_attention(h, hc, w_qkv, q_gain, k_gain, rpb, w_out, ctx_out):
    bsz, length, _ = h.shape
    rows = length // GRID_W
    kh = min(WIN_H, rows)
    scale = HEAD_DIM ** -0.5
    q, k, v = _nat_project(h, w_qkv, q_gain, k_gain)
    qc, kc, vc = _nat_project(hc, w_qkv, q_gain, k_gain)
    grid = (bsz, rows, GRID_W, N_HEADS, HEAD_DIM)
    q, k, v = q.reshape(grid), k.reshape(grid), v.reshape(grid)
    qcol = jnp.arange(GRID_W)[:, None]
    kcol = jnp.arange(GRID_W)[None, :]
    cstart = jnp.clip(qcol - WIN_W // 2, 0, GRID_W - WIN_W)
    col_valid = (kcol >= cstart) & (kcol < cstart + WIN_W)
    dc_idx = jnp.clip(kcol - qcol, 1 - WIN_W, WIN_W - 1) + (WIN_W - 1)
    n_win = kh * GRID_W

    def row_block(r):
        rstart = jnp.clip(r - kh // 2, 0, rows - kh)
        q_r = lax.dynamic_index_in_dim(q, r, axis=1, keepdims=False)
        k_r = lax.dynamic_slice_in_dim(k, rstart, kh, axis=1)
        v_r = lax.dynamic_slice_in_dim(v, rstart, kh, axis=1)
        dr_idx = rstart + jnp.arange(kh) - r + (WIN_H - 1)
        bias = rpb[:, dr_idx[None, :, None], dc_idx[:, None, :]]
        s_win = jnp.einsum("bqhd,bjkhd->bhqjk", q_r, k_r, preferred_element_type=jnp.float32) * scale
        s_win = jnp.where(col_valid[:, None, :], s_win + bias, NEG_INF)
        s_ctx = jnp.einsum("bqhd,bchd->bhqc", q_r, kc, preferred_element_type=jnp.float32) * scale
        p = jax.nn.softmax(jnp.concatenate([s_win.reshape(bsz, N_HEADS, GRID_W, n_win), s_ctx], axis=-1), axis=-1)
        p_win = p[..., :n_win].reshape(bsz, N_HEADS, GRID_W, kh, GRID_W).astype(v.dtype)
        p_ctx = p[..., n_win:].astype(vc.dtype)
        return jnp.einsum("bhqjk,bjkhd->bqhd", p_win, v_r) + jnp.einsum("bhqc,bchd->bqhd", p_ctx, vc)

    o = lax.map(row_block, jnp.arange(rows))
    y = jnp.moveaxis(o, 0, 1).reshape(bsz, length, D_MODEL) @ w_out
    y_ctx = None
    if ctx_out:
        s = jnp.einsum("bqhd,bkhd->bhqk", qc, kc, preferred_element_type=jnp.float32) * scale
        p = jax.nn.softmax(s, axis=-1).astype(vc.dtype)
        y_ctx = jnp.einsum("bhqk,bkhd->bqhd", p, vc).reshape(bsz, hc.shape[1], D_MODEL) @ w_out
    return y, y_ctx


def _ssm_combine(e1, e2):
    a1, b1 = e1
    a2, b2 = e2
    return a1 * a2, a2 * b1 + b2


def _s5_direction(u, uc, a_re, a_im, log_dt, b_re, b_im, c_re, c_im, reverse, ctx_out):
    f32 = jnp.float32
    lam = lax.complex(a_re.astype(f32), a_im.astype(f32))
    dt = jnp.exp(log_dt.astype(f32))[:, None]
    a_bar = jnp.exp(lam * dt)
    b_bar = ((a_bar - 1.0) / lam)[..., None] * lax.complex(b_re.astype(f32), b_im.astype(f32))
    b_bar_re, b_bar_im = jnp.real(b_bar), jnp.imag(b_bar)
    c_re32, c_im32 = c_re.astype(f32), c_im.astype(f32)

    def drive(w):
        wg = w.astype(f32).reshape(w.shape[0], w.shape[1], C_GROUPS, C_GROUP)
        return lax.complex(jnp.einsum("gpc,blgc->blgp", b_bar_re, wg),
                           jnp.einsum("gpc,blgc->blgp", b_bar_im, wg))

    def scan(bu):
        a = jnp.broadcast_to(a_bar, (1,) + bu.shape[1:])
        return lax.associative_scan(_ssm_combine, (a, bu), reverse=reverse, axis=1)[1]

    def readout(hs):
        y = (jnp.einsum("gcp,blgp->blgc", c_re32, jnp.real(hs))
             - jnp.einsum("gcp,blgp->blgc", c_im32, jnp.imag(hs)))
        return y.reshape(hs.shape[0], hs.shape[1], C_WIDTH)

    hs_ctx = scan(drive(uc))
    h0 = hs_ctx[:, 0] if reverse else hs_ctx[:, -1]
    start = -1 if reverse else 0
    bu = drive(u).at[:, start].add(a_bar * h0)
    y = readout(scan(bu))
    y_ctx = readout(hs_ctx) if ctx_out else None
    return y, y_ctx


def _glu(y, w_glu):
    a, g = jnp.split(jax.nn.gelu(y) @ w_glu, 2, axis=-1)
    return a * jax.nn.sigmoid(g)


def _s5_mixer(h, hc, w_in, a_re, a_im, log_dt, b_re, b_im, c_re, c_im, d_skip, w_glu, ctx_out):
    u = h @ w_in
    uc = hc @ w_in
    y = d_skip * u.astype(jnp.float32)
    y_ctx = d_skip * uc.astype(jnp.float32) if ctx_out else None
    for di, rev in enumerate((False, True)):
        yd, yd_ctx = _s5_direction(u, uc, a_re[di], a_im[di], log_dt[di], b_re[di], b_im[di],
                                   c_re[di], c_im[di], rev, ctx_out)
        y = y + yd
        if ctx_out:
            y_ctx = y_ctx + yd_ctx
    out_ctx = _glu(y_ctx, w_glu) if ctx_out else None
    return _glu(y, w_glu), out_ctx


def setup_inputs(seed: int = 0) -> dict:
    key = jax.random.key(seed)
    ks = iter(jax.random.split(key, 40))
    f32 = jnp.float32

    def nrm(shape, s):
        return jax.random.normal(next(ks), shape, f32) * s

    D = D_MODEL
    n_idx = jnp.arange(C_STATE, dtype=f32)
    return {
        "x": nrm((BATCH, SEQ, D), 1.0),
        "c": nrm((BATCH, D), 1.0),
        "ctx": nrm((BATCH, CTX_LEN, D), 1.0),
        "c_ctx": nrm((D,), 1.0),
        "w_ada": nrm((DEPTH, D, N_MOD * D), 0.5 * D ** -0.5),
        "b_ada": nrm((DEPTH, N_MOD * D), 0.02),
        "norm_g": 1.0 + nrm((DEPTH, 3, D), 0.02),
        "ffn_w_gu": nrm((DEPTH, 2, D, 2 * D_FF), D ** -0.5),
        "ffn_w_down": nrm((DEPTH, 2, D_FF, D), D_FF ** -0.5),
        "a_w_in": nrm((N_LAYERS_A, D, 2 * A_WIDTH), D ** -0.5),
        "a_v_gain": 1.0 + nrm((N_LAYERS_A, A_WIDTH), 0.02),
        "a_w_s": nrm((N_LAYERS_A, A_GROUPS, CHUNK, CHUNK), CHUNK ** -0.5),
        "a_b_s": nrm((N_LAYERS_A, A_GROUPS, CHUNK), 0.02),
        "a_w_out": nrm((N_LAYERS_A, A_WIDTH, D), A_WIDTH ** -0.5),
        "b_w_qkv": nrm((N_LAYERS_B, D, 3 * D), D ** -0.5),
        "b_q_gain": 1.0 + nrm((N_LAYERS_B, HEAD_DIM), 0.02),
        "b_k_gain": 1.0 + nrm((N_LAYERS_B, HEAD_DIM), 0.02),
        "b_rpb": nrm((N_LAYERS_B, N_HEADS, 2 * WIN_H - 1, 2 * WIN_W - 1), 0.02),
        "b_w_out": nrm((N_LAYERS_B, D, D), D ** -0.5),
        "c_w_in": nrm((N_LAYERS_C, D, C_WIDTH), D ** -0.5),
        "c_a_re": -0.5 + nrm((N_LAYERS_C, 2, C_GROUPS, C_STATE), 0.01),
        "c_a_im": math.pi * n_idx + nrm((N_LAYERS_C, 2, C_GROUPS, C_STATE), 0.01),
        "c_log_dt": jax.random.uniform(next(ks), (N_LAYERS_C, 2, C_GROUPS), f32,
                                       minval=math.log(DT_MIN), maxval=math.log(DT_MAX)),
        "c_b_re": nrm((N_LAYERS_C, 2, C_GROUPS, C_STATE, C_GROUP), (2 * C_GROUP) ** -0.5),
        "c_b_im": nrm((N_LAYERS_C, 2, C_GROUPS, C_STATE, C_GROUP), (2 * C_GROUP) ** -0.5),
        "c_c_re": nrm((N_LAYERS_C, 2, C_GROUPS, C_GROUP, C_STATE), (2 * C_STATE) ** -0.5),
        "c_c_im": nrm((N_LAYERS_C, 2, C_GROUPS, C_GROUP, C_STATE), (2 * C_STATE) ** -0.5),
        "c_d": nrm((N_LAYERS_C, C_WIDTH), 1.0),
        "c_w_glu": nrm((N_LAYERS_C, C_WIDTH, 2 * D), C_WIDTH ** -0.5),
    }


def reference(x, c, ctx, c_ctx, w_ada, b_ada, norm_g, ffn_w_gu, ffn_w_down,
              a_w_in, a_v_gain, a_w_s, a_b_s, a_w_out,
              b_w_qkv, b_q_gain, b_k_gain, b_rpb, b_w_out,
              c_w_in, c_a_re, c_a_im, c_log_dt, c_b_re, c_b_im, c_c_re, c_c_im, c_d, c_w_glu):
    h, hc = x, ctx
    for i in range(DEPTH):
        kind = i % N_MIXERS
        j = i // N_MIXERS
        last = i == DEPTH - 1
        ctx_in = (not last) or kind != 0
        ctx_out = not last
        mod = _adaln(c, w_ada[i], b_ada[i])
        mod_c = _adaln(c_ctx[None], w_ada[i], b_ada[i])

        h = h + 0.5 * _sublayer_gate(mod, 0) * _swiglu(_sublayer_in(h, norm_g[i, 0], mod, 0),
                                                       ffn_w_gu[i, 0], ffn_w_down[i, 0])
        if ctx_in:
            hc = hc + 0.5 * _sublayer_gate(mod_c, 0) * _swiglu(_sublayer_in(hc, norm_g[i, 0], mod_c, 0),
                                                               ffn_w_gu[i, 0], ffn_w_down[i, 0])
        xin = _sublayer_in(h, norm_g[i, 1], mod, 1)
        if kind == 0:
            y = _chunk_gmlp(xin, a_w_in[j], a_v_gain[j], a_w_s[j], a_b_s[j], a_w_out[j])
            y_c = None
            if ctx_out:
                xin_c = _sublayer_in(hc, norm_g[i, 1], mod_c, 1)
                y_c = _chunk_gmlp(xin_c, a_w_in[j], a_v_gain[j], a_w_s[j], a_b_s[j], a_w_out[j])
        elif kind == 1:
            xin_c = _sublayer_in(hc, norm_g[i, 1], mod_c, 1)
            y, y_c = _neighbourhood_attention(xin, xin_c, b_w_qkv[j], b_q_gain[j], b_k_gain[j],
                                              b_rpb[j], b_w_out[j], ctx_out)
        else:
            xin_c = _sublayer_in(hc, norm_g[i, 1], mod_c, 1)
            y, y_c = _s5_mixer(xin, xin_c, c_w_in[j], c_a_re[j], c_a_im[j], c_log_dt[j],
                               c_b_re[j], c_b_im[j], c_c_re[j], c_c_im[j], c_d[j], c_w_glu[j], ctx_out)
        h = h + _sublayer_gate(mod, 1) * y
        h = h + 0.5 * _sublayer_gate(mod, 2) * _swiglu(_sublayer_in(h, norm_g[i, 2], mod, 2),
                                                       ffn_w_gu[i, 1], ffn_w_down[i, 1])
        if ctx_out:
            hc = hc + _sublayer_gate(mod_c, 1) * y_c
            hc = hc + 0.5 * _sublayer_gate(mod_c, 2) * _swiglu(_sublayer_in(hc, norm_g[i, 2], mod_c, 2),
                                                               ffn_w_gu[i, 1], ffn_w_down[i, 1])
    return h
```

```python
import functools

import numpy as np
import jax
import jax.numpy as jnp
from jax import lax
from jax.experimental import pallas as pl
from jax.experimental.pallas import tpu as pltpu

F32 = jnp.float32
BF16 = jnp.bfloat16

D = 2048
BATCH = 2
SEQ = 4096
CTX = 256
DEPTH = 4
N_LAT = BATCH * SEQ
N_CTX = BATCH * CTX
N_ALL = N_LAT + N_CTX
N_MOD = 9
D_FF = 5632
RMS_EPS = 1e-6
NEG_INF = -1e30
GRID_W = 64
ROWS = SEQ // GRID_W
CHUNK = 128
A_GROUPS = 16
N_HEADS = 16
HEAD_DIM = 128
WIN_H = 8
WIN_W = 16
C_GROUP = 16
C_GROUPS = D // C_GROUP
C_STATE = 64

TM = 512
TF = 512
TN = 512
ADA_TN = 1024
S5_Q = 16
S5_GB = 8
S5_XW = S5_Q * C_GROUP
VMEM_LIMIT = 56 * 1024 * 1024


def _params(*sem):
    return pltpu.CompilerParams(dimension_semantics=sem, vmem_limit_bytes=VMEM_LIMIT)


def _seg(i):
    return jnp.minimum((i * TM) // SEQ, 2)


def _norm_mod(x, gain, scale, shift):
    ms = jnp.mean(x * x, axis=-1, keepdims=True)
    return (x * lax.rsqrt(ms + RMS_EPS) * gain) * (1.0 + scale) + shift


def _mod_row(mod_ref, r):
    return mod_ref[0, r:r + 1, :]


def _dot(a, b):
    return jnp.dot(a, b, preferred_element_type=F32)


def _dot_nt(a, b):
    return lax.dot_general(a, b, (((1,), (1,)), ((), ())), preferred_element_type=F32)


def _ada_kernel(c_ref, w_ref, b_ref, o_ref):
    c = c_ref[...]
    a = (c * jax.nn.sigmoid(c)).astype(BF16)
    o_ref[0] = _dot(a, w_ref[0].astype(BF16)) + b_ref[0]


def _adaln(cond8, w_ada, b_ada):
    n = N_MOD * D
    out = pl.pallas_call(
        _ada_kernel,
        grid=(DEPTH, n // ADA_TN),
        in_specs=[
            pl.BlockSpec((8, D), lambda l, j: (0, 0)),
            pl.BlockSpec((1, D, ADA_TN), lambda l, j: (l, 0, j)),
            pl.BlockSpec((1, 1, ADA_TN), lambda l, j: (l, 0, j)),
        ],
        out_specs=pl.BlockSpec((1, 8, ADA_TN), lambda l, j: (l, 0, j)),
        out_shape=jax.ShapeDtypeStruct((DEPTH, 8, n), F32),
        compiler_params=_params("parallel", "parallel"),
        name="adaln",
    )(cond8, w_ada, b_ada.reshape(DEPTH, 1, n))
    return out[:, :3].reshape(DEPTH, 3, N_MOD, D)


def _ffn_kernel(h_ref, mod_ref, g_ref, wg_ref, wu_ref, wd_ref, o_ref, xn_ref, *, s, nk):
    k = pl.program_id(1)

    @pl.when(k == 0)
    def _():
        xn = _norm_mod(h_ref[...], g_ref[s:s + 1, :], _mod_row(mod_ref, 3 * s + 1),
                       _mod_row(mod_ref, 3 * s))
        xn_ref[...] = xn.astype(BF16)
        o_ref[...] = jnp.zeros_like(o_ref)

    xn = xn_ref[...]
    g = _dot(xn, wg_ref[...])
    u = _dot(xn, wu_ref[...])
    a = (g * jax.nn.sigmoid(g)) * u
    o_ref[...] += _dot(a.astype(BF16), wd_ref[...])

    @pl.when(k == nk - 1)
    def _():
        o_ref[...] = h_ref[...] + (0.5 * _mod_row(mod_ref, 3 * s + 2)) * o_ref[...]


def _ffn(h, mod, gains, w_gu, w_down, s, rows):
    nk = D_FF // TF
    return pl.pallas_call(
        functools.partial(_ffn_kernel, s=s, nk=nk),
        grid=(rows // TM, nk),
        in_specs=[
            pl.BlockSpec((TM, D), lambda i, k: (i, 0)),
            pl.BlockSpec((1, N_MOD, D), lambda i, k: (_seg(i), 0, 0)),
            pl.BlockSpec((3, D), lambda i, k: (0, 0)),
            pl.BlockSpec((D, TF), lambda i, k: (0, k)),
            pl.BlockSpec((D, TF), lambda i, k: (0, nk + k)),
            pl.BlockSpec((TF, D), lambda i, k: (k, 0)),
        ],
        out_specs=pl.BlockSpec((TM, D), lambda i, k: (i, 0)),
        out_shape=jax.ShapeDtypeStruct((rows, D), F32),
        scratch_shapes=[pltpu.VMEM((TM, D), BF16)],
        compiler_params=_params("parallel", "arbitrary"),
        name="ffn",
    )(h, mod, gains, w_gu, w_gu, w_down)


def _mm_res_kernel(z_ref, w_ref, h_ref, mod_ref, o_ref):
    o_ref[...] = h_ref[...] + _mod_row(mod_ref, 5) * _dot(z_ref[...], w_ref[...])


def _mm_res(z, w, h, mod, rows):
    return pl.pallas_call(
        _mm_res_kernel,
        grid=(rows // TM,),
        in_specs=[
            pl.BlockSpec((TM, D), lambda i: (i, 0)),
            pl.BlockSpec((D, D), lambda i: (0, 0)),
            pl.BlockSpec((TM, D), lambda i: (i, 0)),
            pl.BlockSpec((1, N_MOD, D), lambda i: (_seg(i), 0, 0)),
        ],
        out_specs=pl.BlockSpec((TM, D), lambda i: (i, 0)),
        out_shape=jax.ShapeDtypeStruct((rows, D), F32),
        compiler_params=_params("parallel"),
        name="mm_res",
    )(z, w, h, mod)


def _gmlp_kernel(h_ref, mod_ref, g_ref, win_ref, vg_ref, ws_ref, bst_ref, z_ref, xn_ref, y_ref, *, nj):
    j = pl.program_id(1)

    @pl.when(j == 0)
    def _():
        xn = _norm_mod(h_ref[...], g_ref[1:2, :], _mod_row(mod_ref, 4), _mod_row(mod_ref, 3))
        xn_ref[...] = xn.astype(BF16)

    y_ref[j] = jax.nn.gelu(_dot(xn_ref[...], win_ref[...]))

    @pl.when(j == nj - 1)
    def _():
        half = nj // 2
        ssq = jnp.zeros((TM, 1), F32)
        for jj in range(half, nj):
            yv = y_ref[jj]
            ssq = ssq + jnp.sum(yv * yv, axis=-1, keepdims=True)
        inv = lax.rsqrt(ssq / D + RMS_EPS)
        gpb = TN // CHUNK
        for cb in range(half):
            v = (y_ref[half + cb] * inv * vg_ref[:, cb * TN:(cb + 1) * TN]).astype(BF16)
            u = y_ref[cb]
            for gg in range(gpb):
                grp = cb * gpb + gg
                cs = slice(gg * CHUNK, (gg + 1) * CHUNK)
                for c in range(TM // CHUNK):
                    rs = slice(c * CHUNK, (c + 1) * CHUNK)
                    sg = _dot(ws_ref[grp], v[rs, cs]) + bst_ref[:, grp:grp + 1]
                    z_ref[rs, grp * CHUNK:(grp + 1) * CHUNK] = (u[rs, cs] * sg).astype(BF16)


def _gmlp(h, mod, gains, w_in, v_gain, w_s, b_s_t, rows):
    nj = (2 * D) // TN
    return pl.pallas_call(
        functools.partial(_gmlp_kernel, nj=nj),
        grid=(rows // TM, nj),
        in_specs=[
            pl.BlockSpec((TM, D), lambda i, j: (i, 0)),
            pl.BlockSpec((1, N_MOD, D), lambda i, j: (_seg(i), 0, 0)),
            pl.BlockSpec((3, D), lambda i, j: (0, 0)),
            pl.BlockSpec((D, TN), lambda i, j: (0, j)),
            pl.BlockSpec((1, D), lambda i, j: (0, 0)),
            pl.BlockSpec((A_GROUPS, CHUNK, CHUNK), lambda i, j: (0, 0, 0)),
            pl.BlockSpec((CHUNK, A_GROUPS), lambda i, j: (0, 0)),
        ],
        out_specs=pl.BlockSpec((TM, D), lambda i, j: (i, 0)),
        out_shape=jax.ShapeDtypeStruct((rows, D), BF16),
        scratch_shapes=[pltpu.VMEM((TM, D), BF16), pltpu.VMEM((nj, TM, TN), F32)],
        compiler_params=_params("parallel", "arbitrary"),
        name="gmlp",
    )(h, mod, gains, w_in, v_gain, w_s, b_s_t)


def _qkv_kernel(h_ref, mod_ref, g_ref, w_ref, hg_ref, o_ref, xn_ref):
    j = pl.program_id(1)

    @pl.when(j == 0)
    def _():
        xn = _norm_mod(h_ref[...], g_ref[1:2, :], _mod_row(mod_ref, 4), _mod_row(mod_ref, 3))
        xn_ref[...] = xn.astype(BF16)

    y = _dot(xn_ref[...], w_ref[...])
    part = j // (D // TN)

    @pl.when(part < 2)
    def _():
        gain = hg_ref[pl.ds(part, 1), :]
        for hh in range(TN // HEAD_DIM):
            cs = slice(hh * HEAD_DIM, (hh + 1) * HEAD_DIM)
            yh = y[:, cs]
            ms = jnp.mean(yh * yh, axis=-1, keepdims=True)
            o_ref[:, cs] = (yh * lax.rsqrt(ms + RMS_EPS) * gain).astype(BF16)

    @pl.when(part == 2)
    def _():
        o_ref[...] = y.astype(BF16)


def _qkv(h, mod, gains, w_qkv, head_gains, rows):
    return pl.pallas_call(
        _qkv_kernel,
        grid=(rows // TM, (3 * D) // TN),
        in_specs=[
            pl.BlockSpec((TM, D), lambda i, j: (i, 0)),
            pl.BlockSpec((1, N_MOD, D), lambda i, j: (_seg(i), 0, 0)),
            pl.BlockSpec((3, D), lambda i, j: (0, 0)),
            pl.BlockSpec((D, TN), lambda i, j: (0, j)),
            pl.BlockSpec((2, HEAD_DIM), lambda i, j: (0, 0)),
        ],
        out_specs=pl.BlockSpec((TM, TN), lambda i, j: (i, j)),
        out_shape=jax.ShapeDtypeStruct((rows, 3 * D), BF16),
        scratch_shapes=[pltpu.VMEM((TM, D), BF16)],
        compiler_params=_params("parallel", "arbitrary"),
        name="qkv",
    )(h, mod, gains, w_qkv, head_gains)


ATT_QR = 8
ATT_KR = 16
ATT_QB = ATT_QR * GRID_W
ATT_KB = ATT_KR * GRID_W
ATT_PATTERNS = ((0, 0), (ATT_QR, ATT_QR - WIN_H // 2), (ROWS - ATT_QR, ROWS - ATT_KR))


def _attn_kernel(q_ref, k_ref, v_ref, qc_ref, kc_ref, vc_ref, t_ref, o_ref, oc_ref, bias_ref):
    scale = HEAD_DIM ** -0.5
    left = lax.broadcasted_iota(jnp.int32, (GRID_W, 2 * GRID_W), 1) < GRID_W
    neg = jnp.full((GRID_W, 2 * GRID_W), NEG_INF, F32)

    for p, (r0, kr_base) in enumerate(ATT_PATTERNS):
        for qr in range(ATT_QR):
            r = r0 + qr
            rstart = min(max(r - WIN_H // 2, 0), ROWS - WIN_H)
            for kp in range(ATT_KR // 2):
                halves = []
                for kr in (kr_base + 2 * kp, kr_base + 2 * kp + 1):
                    inside = rstart <= kr < rstart + WIN_H
                    halves.append(t_ref[0, kr - r + WIN_H - 1] if inside else None)
                a, b = halves
                if a is None and b is None:
                    blk = neg
                else:
                    blk = jnp.where(left, neg if a is None else a, neg if b is None else b)
                bias_ref[p, qr * GRID_W:(qr + 1) * GRID_W, kp * 2 * GRID_W:(kp + 1) * 2 * GRID_W] = blk

    kc = kc_ref[...]
    vc = vc_ref[...]

    def block(q0, k0, p):
        q = q_ref[pl.ds(q0, ATT_QB), :]
        k = k_ref[pl.ds(k0, ATT_KB), :]
        v = v_ref[pl.ds(k0, ATT_KB), :]
        s_win = _dot_nt(q, k) * scale + bias_ref[p]
        s_ctx = _dot_nt(q, kc) * scale
        m = jnp.maximum(jnp.max(s_win, axis=-1, keepdims=True), jnp.max(s_ctx, axis=-1, keepdims=True))
        p_win = jnp.exp(s_win - m)
        p_ctx = jnp.exp(s_ctx - m)
        denom = jnp.sum(p_win, axis=-1, keepdims=True) + jnp.sum(p_ctx, axis=-1, keepdims=True)
        o = _dot(p_win.astype(BF16), v) + _dot(p_ctx.astype(BF16), vc)
        o_ref[pl.ds(q0, ATT_QB), :] = (o / denom).astype(BF16)

    block(0, 0, 0)

    def interior(rb, carry):
        q0 = pl.multiple_of(rb * ATT_QB, ATT_QB)
        k0 = pl.multiple_of(rb * ATT_QB - (WIN_H // 2) * GRID_W, (WIN_H // 2) * GRID_W)
        block(q0, k0, 1)
        return carry

    lax.fori_loop(1, ROWS // ATT_QR - 1, interior, 0)
    block(SEQ - ATT_QB, SEQ - ATT_KB, 2)

    s = _dot_nt(qc_ref[...], kc) * scale
    pc = jnp.exp(s - jnp.max(s, axis=-1, keepdims=True))
    oc = _dot(pc.astype(BF16), vc) / jnp.sum(pc, axis=-1, keepdims=True)
    oc_ref[...] = oc.astype(BF16)


def _attention(qkv, bias_tab):
    lat_blk = (SEQ, HEAD_DIM)
    ctx_blk = (CTX, HEAD_DIM)
    ctx0 = N_LAT // CTX
    return pl.pallas_call(
        _attn_kernel,
        grid=(N_HEADS, BATCH),
        in_specs=[
            pl.BlockSpec(lat_blk, lambda h, b: (b, h)),
            pl.BlockSpec(lat_blk, lambda h, b: (b, N_HEADS + h)),
            pl.BlockSpec(lat_blk, lambda h, b: (b, 2 * N_HEADS + h)),
            pl.BlockSpec(ctx_blk, lambda h, b: (ctx0 + b, h)),
            pl.BlockSpec(ctx_blk, lambda h, b: (ctx0 + b, N_HEADS + h)),
            pl.BlockSpec(ctx_blk, lambda h, b: (ctx0 + b, 2 * N_HEADS + h)),
            pl.BlockSpec((1, 2 * WIN_H, GRID_W, 2 * GRID_W), lambda h, b: (h, 0, 0, 0)),
        ],
        out_specs=[
            pl.BlockSpec(lat_blk, lambda h, b: (b, h)),
            pl.BlockSpec(ctx_blk, lambda h, b: (b, h)),
        ],
        out_shape=[
            jax.ShapeDtypeStruct((N_LAT, D), BF16),
            jax.ShapeDtypeStruct((N_CTX, D), BF16),
        ],
        scratch_shapes=[pltpu.VMEM((len(ATT_PATTERNS), ATT_QB, ATT_KB), F32)],
        compiler_params=_params("parallel", "parallel"),
        name="nat_attention",
    )(qkv, qkv, qkv, qkv, qkv, qkv, bias_tab)


def _attn_bias_table(rpb):
    qcol = np.arange(GRID_W)[:, None]
    kcol = np.arange(GRID_W)[None, :]
    cstart = np.clip(qcol - WIN_W // 2, 0, GRID_W - WIN_W)
    col_valid = (kcol >= cstart) & (kcol < cstart + WIN_W)
    dc_idx = np.clip(kcol - qcol, 1 - WIN_W, WIN_W - 1) + (WIN_W - 1)
    t = jnp.where(col_valid[None, None], rpb[:, :, dc_idx], NEG_INF)
    t = jnp.concatenate([t, t], axis=-1)
    return jnp.pad(t, ((0, 0), (0, 1), (0, 0), (0, 0)))


def _proj_kernel(h_ref, mod_ref, g_ref, w_ref, o_ref, ob_ref, xn_ref):
    @pl.when(pl.program_id(1) == 0)
    def _():
        xn = _norm_mod(h_ref[...], g_ref[1:2, :], _mod_row(mod_ref, 4), _mod_row(mod_ref, 3))
        xn_ref[...] = xn.astype(BF16)

    y = _dot(xn_ref[...], w_ref[...])
    o_ref[...] = y
    ob_ref[...] = y.astype(BF16)


def _proj(h, mod, gains, w, rows):
    return pl.pallas_call(
        _proj_kernel,
        grid=(rows // TM, D // TN),
        in_specs=[
            pl.BlockSpec((TM, D), lambda i, j: (i, 0)),
            pl.BlockSpec((1, N_MOD, D), lambda i, j: (_seg(i), 0, 0)),
            pl.BlockSpec((3, D), lambda i, j: (0, 0)),
            pl.BlockSpec((D, TN), lambda i, j: (0, j)),
        ],
        out_specs=[pl.BlockSpec((TM, TN), lambda i, j: (i, j))] * 2,
        out_shape=[jax.ShapeDtypeStruct((rows, D), F32), jax.ShapeDtypeStruct((rows, D), BF16)],
        scratch_shapes=[pltpu.VMEM((TM, D), BF16)],
        compiler_params=_params("parallel", "arbitrary"),
        name="s5_in_proj",
    )(h, mod, gains, w)


S5_CL = SEQ // S5_Q
S5_CC = CTX // S5_Q


def _s5_kernel(xl_ref, xc_ref, w1_ref, mo_ref, aq_ref, yl_ref, yc_ref, s_scr, ssw_scr, hp_scr):
    d = pl.program_id(2)
    ns = 2 * C_STATE

    @pl.when(d == 0)
    def _():
        yl_ref[...] = jnp.zeros_like(yl_ref)
        yc_ref[...] = jnp.zeros_like(yc_ref)

    lat0 = S5_CC * S5_GB
    for j in range(S5_GB):
        w1 = w1_ref[0, j]
        rl = _dot(xl_ref[j], w1)
        rc = _dot(xc_ref[j], w1)
        yl_ref[j] += rl[:, :S5_XW]
        yc_ref[j] += rc[:, :S5_XW]
        s_scr[pl.ds(j, S5_CC, stride=S5_GB), :] = rc[:, S5_XW:S5_XW + ns]
        s_scr[pl.ds(lat0 + j, S5_CL, stride=S5_GB), :] = rl[:, S5_XW:S5_XW + ns]
        ssw_scr[pl.ds(j, S5_CC, stride=S5_GB), :] = rc[:, S5_XW + ns:]
        ssw_scr[pl.ds(lat0 + j, S5_CL, stride=S5_GB), :] = rl[:, S5_XW + ns:]

    a1 = aq_ref[0, 0]
    a2 = aq_ref[0, 1]
    a3 = aq_ref[0, 2]

    def step(pos, carry):
        hs, hsw = carry
        row = pl.multiple_of(pos * S5_GB, S5_GB)
        hp_scr[pl.ds(row, S5_GB), :] = hs
        s = s_scr[pl.ds(row, S5_GB), :]
        ssw = ssw_scr[pl.ds(row, S5_GB), :]
        return hs * a1 + hsw * a2 + s, hsw * a1 + hs * a3 + ssw

    def ctx_step(i, carry):
        return step(jnp.where(d == 0, i, S5_CC - 1 - i), carry)

    def lat_step(i, carry):
        return step(S5_CC + jnp.where(d == 0, i, S5_CL - 1 - i), carry)

    zero = jnp.zeros((S5_GB, ns), F32)
    carry = lax.fori_loop(0, S5_CC, ctx_step, (zero, zero))
    lax.fori_loop(0, S5_CL, lat_step, carry, unroll=4)

    for j in range(S5_GB):
        mo = mo_ref[0, j]
        hc = hp_scr[pl.ds(j, S5_CC, stride=S5_GB), :]
        hl = hp_scr[pl.ds(lat0 + j, S5_CL, stride=S5_GB), :]
        yc_ref[j] += _dot(hc.astype(BF16), mo)
        yl_ref[j] += _dot(hl.astype(BF16), mo)


def _s5_scan(x_lat, x_ctx, w1, mo, aq):
    ns = 2 * C_STATE
    nrow = (S5_CC + S5_CL) * S5_GB
    return pl.pallas_call(
        _s5_kernel,
        grid=(BATCH, C_GROUPS // S5_GB, 2),
        in_specs=[
            pl.BlockSpec((S5_GB, S5_CL, S5_XW), lambda b, g, d: (g, b, 0)),
            pl.BlockSpec((S5_GB, S5_CC, S5_XW), lambda b, g, d: (g, b, 0)),
            pl.BlockSpec((1, S5_GB, S5_XW, S5_XW + 2 * ns), lambda b, g, d: (d, g, 0, 0)),
            pl.BlockSpec((1, S5_GB, ns, S5_XW), lambda b, g, d: (d, g, 0, 0)),
            pl.BlockSpec((1, 3, S5_GB, ns), lambda b, g, d: (d, 0, g, 0)),
        ],
        out_specs=[
            pl.BlockSpec((S5_GB, S5_CL, S5_XW), lambda b, g, d: (g, b, 0)),
            pl.BlockSpec((S5_GB, S5_CC, S5_XW), lambda b, g, d: (g, b, 0)),
        ],
        out_shape=[
            jax.ShapeDtypeStruct((C_GROUPS, BATCH * S5_CL, S5_XW), F32),
            jax.ShapeDtypeStruct((C_GROUPS, BATCH * S5_CC, S5_XW), F32),
        ],
        scratch_shapes=[pltpu.VMEM((nrow, ns), F32)] * 3,
        compiler_params=_params("parallel", "parallel", "arbitrary"),
        name="s5_scan",
    )(x_lat, x_ctx, w1, mo, aq)


def _s5_weights(a_re, a_im, log_dt, b_re, b_im, c_re, c_im):
    hp = lax.Precision.HIGHEST
    q = S5_Q
    lam = lax.complex(a_re.astype(F32), a_im.astype(F32))
    dt = jnp.exp(log_dt.astype(F32))[..., None]
    a_bar = jnp.exp(lam * dt)
    b_bar = ((a_bar - 1.0) / lam)[..., None] * lax.complex(b_re.astype(F32), b_im.astype(F32))
    c = lax.complex(c_re.astype(F32), c_im.astype(F32))
    taus = jnp.arange(q + 1, dtype=F32)
    apow = jnp.exp((lam * dt)[:, None] * taus[None, :, None, None])

    ca = c[:, None] * apow[:, :q, :, None, :]
    kmat = (jnp.einsum("dtgop,dgpi->dtgoi", jnp.real(ca), jnp.real(b_bar), precision=hp)
            - jnp.einsum("dtgop,dgpi->dtgoi", jnp.imag(ca), jnp.imag(b_bar), precision=hp))
    kz = jnp.concatenate([kmat, jnp.zeros_like(kmat[:, :1])], axis=1)
    s_idx = np.arange(q)[:, None]
    t_idx = np.arange(q)[None, :]
    lag_f = np.where(t_idx >= s_idx, t_idx - s_idx, q)
    lag_r = np.where(s_idx >= t_idx, s_idx - t_idx, q)
    m_intra = jnp.stack([kz[0][lag_f], kz[1][lag_r]])
    m_intra = m_intra.transpose(0, 3, 1, 5, 2, 4).reshape(2, C_GROUPS, S5_XW, S5_XW)

    pw_state = jnp.stack([apow[0, q - 1 - np.arange(q)], apow[1, np.arange(q)]])
    st = pw_state[..., None] * b_bar[:, None]
    st = st.transpose(0, 2, 1, 4, 3).reshape(2, C_GROUPS, S5_XW, C_STATE)
    st_re, st_im = jnp.real(st), jnp.imag(st)
    m_state = jnp.concatenate([st_re, st_im, st_im, st_re], axis=-1)

    pw_out = jnp.stack([apow[0, 1 + np.arange(q)], apow[1, q - np.arange(q)]])
    wo = c[:, None] * pw_out[:, :, :, None, :]
    wo = wo.transpose(0, 2, 4, 1, 3).reshape(2, C_GROUPS, C_STATE, S5_XW)
    m_out = jnp.concatenate([jnp.real(wo), -jnp.imag(wo)], axis=2)

    aq = apow[:, q]
    ar, ai = jnp.real(aq), jnp.imag(aq)
    aq3 = jnp.stack([jnp.concatenate([ar, ar], -1), jnp.concatenate([-ai, ai], -1),
                     jnp.concatenate([ai, -ai], -1)], axis=1)
    w1 = jnp.concatenate([m_intra, m_state], axis=-1).astype(BF16)
    return w1, m_out.astype(BF16), aq3


def _to_chunks(u):
    n = u.shape[0]
    return u.reshape(n // S5_Q, S5_Q, C_GROUPS, C_GROUP).transpose(2, 0, 1, 3).reshape(C_GROUPS, n // S5_Q, S5_XW)


def _from_chunks(y):
    n = y.shape[1] * S5_Q
    return y.reshape(C_GROUPS, n // S5_Q, S5_Q, C_GROUP).transpose(1, 2, 0, 3).reshape(n, D)


def _glu_kernel(u_ref, y_ref, dsk_ref, wa_ref, wg_ref, h_ref, mod_ref, o_ref, z_ref):
    @pl.when(pl.program_id(1) == 0)
    def _():
        z_ref[...] = jax.nn.gelu(dsk_ref[...] * u_ref[...] + y_ref[...]).astype(BF16)

    z = z_ref[...]
    a = _dot(z, wa_ref[...])
    g = _dot(z, wg_ref[...])
    o_ref[...] = h_ref[...] + _mod_row(mod_ref, 5) * (a * jax.nn.sigmoid(g))


def _glu(u, y, d_skip, w_glu, h, mod, rows):
    nj = D // TN
    return pl.pallas_call(
        _glu_kernel,
        grid=(rows // TM, nj),
        in_specs=[
            pl.BlockSpec((TM, D), lambda i, j: (i, 0)),
            pl.BlockSpec((TM, D), lambda i, j: (i, 0)),
            pl.BlockSpec((1, D), lambda i, j: (0, 0)),
            pl.BlockSpec((D, TN), lambda i, j: (0, j)),
            pl.BlockSpec((D, TN), lambda i, j: (0, nj + j)),
            pl.BlockSpec((TM, TN), lambda i, j: (i, j)),
            pl.BlockSpec((1, N_MOD, TN), lambda i, j: (_seg(i), 0, j)),
        ],
        out_specs=pl.BlockSpec((TM, TN), lambda i, j: (i, j)),
        out_shape=jax.ShapeDtypeStruct((rows, D), F32),
        scratch_shapes=[pltpu.VMEM((TM, D), BF16)],
        compiler_params=_params("parallel", "arbitrary"),
        name="s5_glu",
    )(u, y, d_skip, w_glu, w_glu, h, mod)


def kernel(x, c, ctx, c_ctx, w_ada, b_ada, norm_g, ffn_w_gu, ffn_w_down, a_w_in, a_v_gain, a_w_s, a_b_s, a_w_out, b_w_qkv, b_q_gain, b_k_gain, b_rpb, b_w_out, c_w_in, c_a_re, c_a_im, c_log_dt, c_b_re, c_b_im, c_c_re, c_c_im, c_d, c_w_glu):
    h = jnp.concatenate([x.reshape(N_LAT, D), ctx.reshape(N_CTX, D)], axis=0).astype(F32)
    cond8 = jnp.concatenate([c, c_ctx[None], jnp.zeros((8 - BATCH - 1, D), c.dtype)], axis=0).astype(F32)
    mods = _adaln(cond8, w_ada.astype(F32), b_ada.astype(F32))
    norm_g = norm_g.astype(F32)

    for i in range(DEPTH):
        kind, j = i % 3, i // 3
        last = i == DEPTH - 1
        rows = N_LAT if last else N_ALL
        mod, gains = mods[i], norm_g[i]

        h = _ffn(h, mod, gains, ffn_w_gu[i, 0].astype(BF16), ffn_w_down[i, 0].astype(BF16), 0, rows)

        if kind == 0:
            z = _gmlp(h, mod, gains, a_w_in[j].astype(BF16), a_v_gain[j].astype(F32)[None],
                      a_w_s[j].astype(BF16), a_b_s[j].astype(F32).T, rows)
            h = _mm_res(z, a_w_out[j].astype(BF16), h, mod, rows)
        elif kind == 1:
            head_gains = jnp.stack([b_q_gain[j], b_k_gain[j]]).astype(F32)
            qkv = _qkv(h, mod, gains, b_w_qkv[j].astype(BF16), head_gains, rows)
            o_lat, o_ctx = _attention(qkv, _attn_bias_table(b_rpb[j].astype(F32)))
            z = jnp.concatenate([o_lat, o_ctx], axis=0)
            h = _mm_res(z, b_w_out[j].astype(BF16), h, mod, rows)
        else:
            u, ub = _proj(h, mod, gains, c_w_in[j].astype(BF16), rows)
            w1, mo, aq = _s5_weights(c_a_re[j], c_a_im[j], c_log_dt[j], c_b_re[j], c_b_im[j],
                                     c_c_re[j], c_c_im[j])
            y_lat, y_ctx = _s5_scan(_to_chunks(ub[:N_LAT]), _to_chunks(ub[N_LAT:]), w1, mo, aq)
            y = jnp.concatenate([_from_chunks(y_lat), _from_chunks(y_ctx)], axis=0)
            h = _glu(u, y, c_d[j].astype(F32)[None], c_w_glu[j].astype(BF16), h, mod, rows)

        h = _ffn(h, mod, gains, ffn_w_gu[i, 1].astype(BF16), ffn_w_down[i, 1].astype(BF16), 2, rows)

    return h[:N_LAT].reshape(BATCH, SEQ, D).astype(x.dtype)
```

```python
import functools

import numpy as np
import jax
import jax.numpy as jnp
from jax import lax
from jax.experimental import pallas as pl
from jax.experimental.pallas import tpu as pltpu

F32 = jnp.float32
BF16 = jnp.bfloat16
HIGHEST = lax.Precision.HIGHEST

D = 2048
BATCH = 2
SEQ = 4096
CTX = 256
DEPTH = 4
N_LAT = BATCH * SEQ
N_CTX = BATCH * CTX
N_ALL = N_LAT + N_CTX
N_MOD = 9
D_FF = 5632
RMS_EPS = 1e-6
NEG_INF = -1e30
GRID_W = 64
ROWS = SEQ // GRID_W
CHUNK = 128
A_GROUPS = 16
N_HEADS = 16
HEAD_DIM = 128
WIN_H = 8
WIN_W = 16
C_GROUP = 16
C_GROUPS = D // C_GROUP
C_STATE = 64
LANES = 128

TM = 512
TM_F = 1024
TF = 256
TN = 512
ADA_TN = 1024
S5_Q = 16
S5_GB = LANES // C_GROUP
S5_XW = S5_Q * C_GROUP
VMEM_LIMIT = 56 * 1024 * 1024


def _params(*sem):
    return pltpu.CompilerParams(dimension_semantics=sem, vmem_limit_bytes=VMEM_LIMIT)


def _seg(i, tm=TM):
    return jnp.minimum((i * tm) // SEQ, 2)


def _norm_mod(x, gain, scale, shift):
    ms = jnp.mean(x * x, axis=-1, keepdims=True)
    return (x * lax.rsqrt(ms + RMS_EPS) * gain) * (1.0 + scale) + shift


def _mod_row(mod_ref, r):
    return mod_ref[0, r:r + 1, :]


def _dot(a, b):
    return jnp.dot(a, b, preferred_element_type=F32)


def _dot_nt(a, b):
    return lax.dot_general(a, b, (((1,), (1,)), ((), ())), preferred_element_type=F32)


def _ada_kernel(c_ref, w_ref, b_ref, o_ref):
    c = c_ref[...]
    a = (c * jax.nn.sigmoid(c)).astype(BF16)
    o_ref[0] = _dot(a, w_ref[0].astype(BF16)) + b_ref[0]


def _adaln(cond8, w_ada, b_ada):
    n = N_MOD * D
    out = pl.pallas_call(
        _ada_kernel,
        grid=(DEPTH, n // ADA_TN),
        in_specs=[
            pl.BlockSpec((8, D), lambda l, j: (0, 0)),
            pl.BlockSpec((1, D, ADA_TN), lambda l, j: (l, 0, j)),
            pl.BlockSpec((1, 1, ADA_TN), lambda l, j: (l, 0, j)),
        ],
        out_specs=pl.BlockSpec((1, 8, ADA_TN), lambda l, j: (l, 0, j)),
        out_shape=jax.ShapeDtypeStruct((DEPTH, 8, n), F32),
        compiler_params=_params("parallel", "parallel"),
        name="adaln",
    )(cond8, w_ada, b_ada.reshape(DEPTH, 1, n))
    return out[:, :3].reshape(DEPTH, 3, N_MOD, D)


def _ffn_kernel(h_ref, mod_ref, g_ref, wg_ref, wu_ref, wd_ref, o_ref, xn_ref, *, s, nk, n_full, tail):
    i = pl.program_id(0)
    k = pl.program_id(1)

    def body(m):
        @pl.when(k == 0)
        def _():
            xn = _norm_mod(h_ref[:m, :], g_ref[s:s + 1, :], _mod_row(mod_ref, 3 * s + 1),
                           _mod_row(mod_ref, 3 * s))
            xn_ref[:m, :] = xn.astype(BF16)
            o_ref[:m, :] = jnp.zeros((m, D), F32)

        xn = xn_ref[:m, :]
        g = _dot(xn, wg_ref[...].astype(BF16))
        u = _dot(xn, wu_ref[...].astype(BF16))
        a = (g * jax.nn.sigmoid(g)) * u
        o_ref[:m, :] += _dot(a.astype(BF16), wd_ref[...].astype(BF16))

        @pl.when(k == nk - 1)
        def _():
            o_ref[:m, :] = h_ref[:m, :] + (0.5 * _mod_row(mod_ref, 3 * s + 2)) * o_ref[:m, :]

    if tail:
        pl.when(i < n_full)(lambda: body(TM_F))
        pl.when(i >= n_full)(lambda: body(tail))
    else:
        body(TM_F)


def _ffn(h, mod, gains, w_gu, w_down, s, rows):
    nk = D_FF // TF
    n_full, tail = divmod(rows, TM_F)
    once = pl.Buffered(1)
    return pl.pallas_call(
        functools.partial(_ffn_kernel, s=s, nk=nk, n_full=n_full, tail=tail),
        grid=(n_full + (1 if tail else 0), nk),
        in_specs=[
            pl.BlockSpec((TM_F, D), lambda i, k: (i, 0), pipeline_mode=once),
            pl.BlockSpec((1, N_MOD, D), lambda i, k: (_seg(i, TM_F), 0, 0)),
            pl.BlockSpec((3, D), lambda i, k: (0, 0)),
            pl.BlockSpec((D, TF), lambda i, k: (0, k)),
            pl.BlockSpec((D, TF), lambda i, k: (0, nk + k)),
            pl.BlockSpec((TF, D), lambda i, k: (k, 0)),
        ],
        out_specs=pl.BlockSpec((TM_F, D), lambda i, k: (i, 0), pipeline_mode=once),
        out_shape=jax.ShapeDtypeStruct((rows, D), F32),
        scratch_shapes=[pltpu.VMEM((TM_F, D), BF16)],
        compiler_params=_params("arbitrary", "arbitrary"),
        name="ffn",
    )(h, mod, gains, w_gu, w_gu, w_down)


def _mm_res_kernel(*refs, n_lat):
    if n_lat is None:
        z_ref, w_ref, h_ref, mod_ref, o_ref = refs
        o_ref[...] = h_ref[...] + _mod_row(mod_ref, 5) * _dot(z_ref[...], w_ref[...])
        return
    zl_ref, zc_ref, w_ref, h_ref, mod_ref, o_ref = refs
    i = pl.program_id(0)

    @pl.when(i < n_lat)
    def _():
        o_ref[...] = h_ref[...] + _mod_row(mod_ref, 5) * _dot(zl_ref[...], w_ref[...])

    @pl.when(i >= n_lat)
    def _():
        o_ref[...] = h_ref[...] + _mod_row(mod_ref, 5) * _dot(zc_ref[...], w_ref[...])


def _split_specs(n_lat):
    return [pl.BlockSpec((TM, D), lambda i, *_: (jnp.minimum(i, n_lat - 1), 0)),
            pl.BlockSpec((TM, D), lambda i, *_: (jnp.maximum(i - n_lat, 0), 0))]


def _mm_res(z, z_ctx, w, h, mod, rows):
    n_lat = None if z_ctx is None else N_LAT // TM
    z_specs = [pl.BlockSpec((TM, D), lambda i: (i, 0))] if z_ctx is None else _split_specs(n_lat)
    zs = (z,) if z_ctx is None else (z, z_ctx)
    return pl.pallas_call(
        functools.partial(_mm_res_kernel, n_lat=n_lat),
        grid=(rows // TM,),
        in_specs=z_specs + [
            pl.BlockSpec((D, D), lambda i: (0, 0)),
            pl.BlockSpec((TM, D), lambda i: (i, 0)),
            pl.BlockSpec((1, N_MOD, D), lambda i: (_seg(i), 0, 0)),
        ],
        out_specs=pl.BlockSpec((TM, D), lambda i: (i, 0)),
        out_shape=jax.ShapeDtypeStruct((rows, D), F32),
        compiler_params=_params("parallel"),
        name="mm_res",
    )(*zs, w, h, mod)


def _gmlp_kernel(h_ref, mod_ref, g_ref, win_ref, vg_ref, ws_ref, bst_ref, z_ref, xn_ref, y_ref, *, nj):
    j = pl.program_id(1)

    @pl.when(j == 0)
    def _():
        xn = _norm_mod(h_ref[...], g_ref[1:2, :], _mod_row(mod_ref, 4), _mod_row(mod_ref, 3))
        xn_ref[...] = xn.astype(BF16)

    y_ref[j] = jax.nn.gelu(_dot(xn_ref[...], win_ref[...]))

    @pl.when(j == nj - 1)
    def _():
        half = nj // 2
        ssq = jnp.zeros((TM, 1), F32)
        for jj in range(half, nj):
            yv = y_ref[jj]
            ssq = ssq + jnp.sum(yv * yv, axis=-1, keepdims=True)
        inv = lax.rsqrt(ssq / D + RMS_EPS)
        gpb = TN // CHUNK
        for cb in range(half):
            v = (y_ref[half + cb] * inv * vg_ref[:, cb * TN:(cb + 1) * TN]).astype(BF16)
            u = y_ref[cb]
            for gg in range(gpb):
                grp = cb * gpb + gg
                cs = slice(gg * CHUNK, (gg + 1) * CHUNK)
                for c in range(TM // CHUNK):
                    rs = slice(c * CHUNK, (c + 1) * CHUNK)
                    sg = _dot(ws_ref[grp], v[rs, cs]) + bst_ref[:, grp:grp + 1]
                    z_ref[rs, grp * CHUNK:(grp + 1) * CHUNK] = (u[rs, cs] * sg).astype(BF16)


def _gmlp(h, mod, gains, w_in, v_gain, w_s, b_s_t, rows):
    nj = (2 * D) // TN
    return pl.pallas_call(
        functools.partial(_gmlp_kernel, nj=nj),
        grid=(rows // TM, nj),
        in_specs=[
            pl.BlockSpec((TM, D), lambda i, j: (i, 0)),
            pl.BlockSpec((1, N_MOD, D), lambda i, j: (_seg(i), 0, 0)),
            pl.BlockSpec((3, D), lambda i, j: (0, 0)),
            pl.BlockSpec((D, TN), lambda i, j: (0, j)),
            pl.BlockSpec((1, D), lambda i, j: (0, 0)),
            pl.BlockSpec((A_GROUPS, CHUNK, CHUNK), lambda i, j: (0, 0, 0)),
            pl.BlockSpec((CHUNK, A_GROUPS), lambda i, j: (0, 0)),
        ],
        out_specs=pl.BlockSpec((TM, D), lambda i, j: (i, 0)),
        out_shape=jax.ShapeDtypeStruct((rows, D), BF16),
        scratch_shapes=[pltpu.VMEM((TM, D), BF16), pltpu.VMEM((nj, TM, TN), F32)],
        compiler_params=_params("parallel", "arbitrary"),
        name="gmlp",
    )(h, mod, gains, w_in, v_gain, w_s, b_s_t)


def _qkv_kernel(h_ref, mod_ref, g_ref, w_ref, hg_ref, o_ref, xn_ref):
    j = pl.program_id(1)

    @pl.when(j == 0)
    def _():
        xn = _norm_mod(h_ref[...], g_ref[1:2, :], _mod_row(mod_ref, 4), _mod_row(mod_ref, 3))
        xn_ref[...] = xn.astype(BF16)

    y = _dot(xn_ref[...], w_ref[...])
    part = j // (D // TN)

    @pl.when(part < 2)
    def _():
        gain = hg_ref[pl.ds(part, 1), :]
        for hh in range(TN // HEAD_DIM):
            cs = slice(hh * HEAD_DIM, (hh + 1) * HEAD_DIM)
            yh = y[:, cs]
            ms = jnp.mean(yh * yh, axis=-1, keepdims=True)
            o_ref[:, cs] = (yh * lax.rsqrt(ms + RMS_EPS) * gain).astype(BF16)

    @pl.when(part == 2)
    def _():
        o_ref[...] = y.astype(BF16)


def _qkv(h, mod, gains, w_qkv, head_gains, rows):
    return pl.pallas_call(
        _qkv_kernel,
        grid=(rows // TM, (3 * D) // TN),
        in_specs=[
            pl.BlockSpec((TM, D), lambda i, j: (i, 0)),
            pl.BlockSpec((1, N_MOD, D), lambda i, j: (_seg(i), 0, 0)),
            pl.BlockSpec((3, D), lambda i, j: (0, 0)),
            pl.BlockSpec((D, TN), lambda i, j: (0, j)),
            pl.BlockSpec((2, HEAD_DIM), lambda i, j: (0, 0)),
        ],
        out_specs=pl.BlockSpec((TM, TN), lambda i, j: (i, j)),
        out_shape=jax.ShapeDtypeStruct((rows, 3 * D), BF16),
        scratch_shapes=[pltpu.VMEM((TM, D), BF16)],
        compiler_params=_params("parallel", "arbitrary"),
        name="qkv",
    )(h, mod, gains, w_qkv, head_gains)


ATT_QR = 8
ATT_KR = 16
ATT_QB = ATT_QR * GRID_W
ATT_KB = ATT_KR * GRID_W
N_DR = 2 * WIN_H - 1
ATT_PATTERNS = ((0, 0), (ATT_QR, ATT_QR - WIN_H // 2), (ROWS - ATT_QR, ROWS - ATT_KR))


def _attn_kernel(q_ref, k_ref, v_ref, qc_ref, kc_ref, vc_ref, t_ref, o_ref, oc_ref, bias_ref):
    scale = HEAD_DIM ** -0.5
    left = lax.broadcasted_iota(jnp.int32, (GRID_W, 2 * GRID_W), 1) < GRID_W
    neg = jnp.full((GRID_W, 2 * GRID_W), NEG_INF, F32)

    for p, (r0, kr_base) in enumerate(ATT_PATTERNS):
        for qr in range(ATT_QR):
            r = r0 + qr
            rstart = min(max(r - WIN_H // 2, 0), ROWS - WIN_H)
            for kp in range(ATT_KR // 2):
                halves = []
                for kr in (kr_base + 2 * kp, kr_base + 2 * kp + 1):
                    inside = rstart <= kr < rstart + WIN_H
                    halves.append(t_ref[0, kr - r + WIN_H - 1] if inside else None)
                a, b = halves
                if a is None and b is None:
                    blk = neg
                else:
                    blk = jnp.where(left, neg if a is None else a, neg if b is None else b)
                bias_ref[p, qr * GRID_W:(qr + 1) * GRID_W, kp * 2 * GRID_W:(kp + 1) * 2 * GRID_W] = blk

    kc = kc_ref[...]
    vc = vc_ref[...]

    def block(q0, k0, p):
        q = q_ref[pl.ds(q0, ATT_QB), :]
        k = k_ref[pl.ds(k0, ATT_KB), :]
        v = v_ref[pl.ds(k0, ATT_KB), :]
        s_win = _dot_nt(q, k) * scale + bias_ref[p]
        s_ctx = _dot_nt(q, kc) * scale
        m = jnp.maximum(jnp.max(s_win, axis=-1, keepdims=True), jnp.max(s_ctx, axis=-1, keepdims=True))
        p_win = jnp.exp(s_win - m)
        p_ctx = jnp.exp(s_ctx - m)
        denom = jnp.sum(p_win, axis=-1, keepdims=True) + jnp.sum(p_ctx, axis=-1, keepdims=True)
        o = _dot(p_win.astype(BF16), v) + _dot(p_ctx.astype(BF16), vc)
        o_ref[pl.ds(q0, ATT_QB), :] = (o / denom).astype(BF16)

    block(0, 0, 0)

    def interior(rb, carry):
        q0 = pl.multiple_of(rb * ATT_QB, ATT_QB)
        k0 = pl.multiple_of(rb * ATT_QB - (WIN_H // 2) * GRID_W, (WIN_H // 2) * GRID_W)
        block(q0, k0, 1)
        return carry

    lax.fori_loop(1, ROWS // ATT_QR - 1, interior, 0)
    block(SEQ - ATT_QB, SEQ - ATT_KB, 2)

    s = _dot_nt(qc_ref[...], kc) * scale
    pc = jnp.exp(s - jnp.max(s, axis=-1, keepdims=True))
    oc = _dot(pc.astype(BF16), vc) / jnp.sum(pc, axis=-1, keepdims=True)
    oc_ref[...] = oc.astype(BF16)


def _attention(qkv, bias_tab):
    lat_blk = (SEQ, HEAD_DIM)
    ctx_blk = (CTX, HEAD_DIM)
    ctx0 = N_LAT // CTX
    return pl.pallas_call(
        _attn_kernel,
        grid=(N_HEADS, BATCH),
        in_specs=[
            pl.BlockSpec(lat_blk, lambda h, b: (b, h)),
            pl.BlockSpec(lat_blk, lambda h, b: (b, N_HEADS + h)),
            pl.BlockSpec(lat_blk, lambda h, b: (b, 2 * N_HEADS + h)),
            pl.BlockSpec(ctx_blk, lambda h, b: (ctx0 + b, h)),
            pl.BlockSpec(ctx_blk, lambda h, b: (ctx0 + b, N_HEADS + h)),
            pl.BlockSpec(ctx_blk, lambda h, b: (ctx0 + b, 2 * N_HEADS + h)),
            pl.BlockSpec((1, N_DR, GRID_W, 2 * GRID_W), lambda h, b: (h, 0, 0, 0)),
        ],
        out_specs=[
            pl.BlockSpec(lat_blk, lambda h, b: (b, h)),
            pl.BlockSpec(ctx_blk, lambda h, b: (b, h)),
        ],
        out_shape=[
            jax.ShapeDtypeStruct((N_LAT, D), BF16),
            jax.ShapeDtypeStruct((N_CTX, D), BF16),
        ],
        scratch_shapes=[pltpu.VMEM((len(ATT_PATTERNS), ATT_QB, ATT_KB), F32)],
        compiler_params=_params("parallel", "parallel"),
        name="nat_attention",
    )(qkv, qkv, qkv, qkv, qkv, qkv, bias_tab)


def _attn_bias_table(rpb):
    qcol = np.arange(GRID_W)[:, None]
    kcol = np.arange(2 * GRID_W)[None, :] % GRID_W
    cstart = np.clip(qcol - WIN_W // 2, 0, GRID_W - WIN_W)
    col_valid = (kcol >= cstart) & (kcol < cstart + WIN_W)
    dc_idx = np.clip(kcol - qcol, 1 - WIN_W, WIN_W - 1) + (WIN_W - 1)
    onehot = (dc_idx[None] == np.arange(2 * WIN_W - 1)[:, None, None]) & col_valid[None]
    onehot = jnp.asarray(onehot.reshape(2 * WIN_W - 1, -1), F32)
    mask = jnp.asarray(np.where(col_valid, 0.0, NEG_INF).reshape(-1), F32)
    t = jnp.dot(rpb.reshape(N_HEADS * N_DR, 2 * WIN_W - 1), onehot, precision=HIGHEST) + mask
    return t.reshape(N_HEADS, N_DR, GRID_W, 2 * GRID_W)


def _proj_kernel(h_ref, mod_ref, g_ref, w_ref, o_ref, xn_ref):
    @pl.when(pl.program_id(1) == 0)
    def _():
        xn = _norm_mod(h_ref[...], g_ref[1:2, :], _mod_row(mod_ref, 4), _mod_row(mod_ref, 3))
        xn_ref[...] = xn.astype(BF16)

    o_ref[...] = _dot(xn_ref[...], w_ref[...])


def _proj(h, mod, gains, w, rows):
    return pl.pallas_call(
        _proj_kernel,
        grid=(rows // TM, D // TN),
        in_specs=[
            pl.BlockSpec((TM, D), lambda i, j: (i, 0)),
            pl.BlockSpec((1, N_MOD, D), lambda i, j: (_seg(i), 0, 0)),
            pl.BlockSpec((3, D), lambda i, j: (0, 0)),
            pl.BlockSpec((D, TN), lambda i, j: (0, j)),
        ],
        out_specs=pl.BlockSpec((TM, TN), lambda i, j: (i, j)),
        out_shape=jax.ShapeDtypeStruct((rows, D), F32),
        scratch_shapes=[pltpu.VMEM((TM, D), BF16)],
        compiler_params=_params("parallel", "arbitrary"),
        name="s5_in_proj",
    )(h, mod, gains, w)


S5_CL = SEQ // S5_Q
S5_CC = CTX // S5_Q
S5_HALF = LANES // C_GROUP


def _s5_kernel(ul_ref, uc_ref, w1_ref, mo_ref, aq_ref, yl_ref, yc_ref,
               ut_scr, xl_scr, xc_scr, s_scr, ssw_scr, hp_scr, yl_acc, yc_acc):
    d = pl.program_id(2)
    ns = 2 * C_STATE

    def lane_group(n):
        return lax.broadcasted_iota(jnp.int32, (n, LANES), 1) // C_GROUP

    def to_chunks(u_ref, x_scr, nch):
        grp = lane_group(nch)
        for t in range(S5_Q):
            ut_scr[t, :nch, :] = u_ref[pl.ds(t, nch, stride=S5_Q), :]
        for gl in range(S5_GB):
            for hf in range(S5_Q // S5_HALF):
                acc = None
                for tl in range(S5_HALF):
                    piece = ut_scr[hf * S5_HALF + tl, :nch, :]
                    shift = ((tl - gl) * C_GROUP) % LANES
                    if shift:
                        piece = pltpu.roll(piece, shift, 1)
                    acc = piece if acc is None else jnp.where(grp == tl, piece, acc)
                x_scr[gl, :, hf * LANES:(hf + 1) * LANES] = acc.astype(BF16)

    def from_chunks(y_acc, y_ref, nch):
        grp = lane_group(nch)
        for t in range(S5_Q):
            hf, tl = divmod(t, S5_HALF)
            acc = None
            for gl in range(S5_GB):
                piece = y_acc[gl, :, hf * LANES:(hf + 1) * LANES]
                shift = ((gl - tl) * C_GROUP) % LANES
                if shift:
                    piece = pltpu.roll(piece, shift, 1)
                acc = piece if acc is None else jnp.where(grp == gl, piece, acc)
            y_ref[pl.ds(t, nch, stride=S5_Q), :] = acc

    @pl.when(d == 0)
    def _():
        to_chunks(ul_ref, xl_scr, S5_CL)
        to_chunks(uc_ref, xc_scr, S5_CC)
        yl_acc[...] = jnp.zeros_like(yl_acc)
        yc_acc[...] = jnp.zeros_like(yc_acc)

    lat0 = S5_CC * S5_GB
    for j in range(S5_GB):
        w1 = w1_ref[0, j]
        rl = _dot(xl_scr[j], w1)
        rc = _dot(xc_scr[j], w1)
        yl_acc[j] += rl[:, :S5_XW]
        yc_acc[j] += rc[:, :S5_XW]
        s_scr[pl.ds(j, S5_CC, stride=S5_GB), :] = rc[:, S5_XW:S5_XW + ns]
        s_scr[pl.ds(lat0 + j, S5_CL, stride=S5_GB), :] = rl[:, S5_XW:S5_XW + ns]
        ssw_scr[pl.ds(j, S5_CC, stride=S5_GB), :] = rc[:, S5_XW + ns:]
        ssw_scr[pl.ds(lat0 + j, S5_CL, stride=S5_GB), :] = rl[:, S5_XW + ns:]

    a1 = aq_ref[0, 0]
    a2 = aq_ref[0, 1]
    a3 = aq_ref[0, 2]

    def step(pos, carry):
        hs, hsw = carry
        row = pl.multiple_of(pos * S5_GB, S5_GB)
        hp_scr[pl.ds(row, S5_GB), :] = hs
        s = s_scr[pl.ds(row, S5_GB), :]
        ssw = ssw_scr[pl.ds(row, S5_GB), :]
        return hs * a1 + hsw * a2 + s, hsw * a1 + hs * a3 + ssw

    def ctx_step(i, carry):
        return step(jnp.where(d == 0, i, S5_CC - 1 - i), carry)

    def lat_step(i, carry):
        return step(S5_CC + jnp.where(d == 0, i, S5_CL - 1 - i), carry)

    zero = jnp.zeros((S5_GB, ns), F32)
    carry = lax.fori_loop(0, S5_CC, ctx_step, (zero, zero))
    lax.fori_loop(0, S5_CL, lat_step, carry, unroll=4)

    for j in range(S5_GB):
        mo = mo_ref[0, j]
        hc = hp_scr[pl.ds(j, S5_CC, stride=S5_GB), :]
        hl = hp_scr[pl.ds(lat0 + j, S5_CL, stride=S5_GB), :]
        yc_acc[j] += _dot(hc.astype(BF16), mo)
        yl_acc[j] += _dot(hl.astype(BF16), mo)

    @pl.when(d == 1)
    def _():
        from_chunks(yl_acc, yl_ref, S5_CL)
        from_chunks(yc_acc, yc_ref, S5_CC)


def _s5_scan(u, w1, mo, aq):
    ns = 2 * C_STATE
    nrow = (S5_CC + S5_CL) * S5_GB
    ctx0 = N_LAT // CTX
    return pl.pallas_call(
        _s5_kernel,
        grid=(BATCH, C_GROUPS // S5_GB, 2),
        in_specs=[
            pl.BlockSpec((SEQ, LANES), lambda b, g, d: (b, g)),
            pl.BlockSpec((CTX, LANES), lambda b, g, d: (ctx0 + b, g)),
            pl.BlockSpec((1, S5_GB, S5_XW, S5_XW + 2 * ns), lambda b, g, d: (d, g, 0, 0)),
            pl.BlockSpec((1, S5_GB, ns, S5_XW), lambda b, g, d: (d, g, 0, 0)),
            pl.BlockSpec((1, 3, S5_GB, ns), lambda b, g, d: (d, 0, g, 0)),
        ],
        out_specs=[
            pl.BlockSpec((SEQ, LANES), lambda b, g, d: (b, g)),
            pl.BlockSpec((CTX, LANES), lambda b, g, d: (b, g)),
        ],
        out_shape=[
            jax.ShapeDtypeStruct((N_LAT, D), F32),
            jax.ShapeDtypeStruct((N_CTX, D), F32),
        ],
        scratch_shapes=[
            pltpu.VMEM((S5_Q, S5_CL, LANES), F32),
            pltpu.VMEM((S5_GB, S5_CL, S5_XW), BF16),
            pltpu.VMEM((S5_GB, S5_CC, S5_XW), BF16),
            pltpu.VMEM((nrow, ns), F32),
            pltpu.VMEM((nrow, ns), F32),
            pltpu.VMEM((nrow, ns), F32),
            pltpu.VMEM((S5_GB, S5_CL, S5_XW), F32),
            pltpu.VMEM((S5_GB, S5_CC, S5_XW), F32),
        ],
        compiler_params=_params("arbitrary", "arbitrary", "arbitrary"),
        name="s5_scan",
    )(u, u, w1, mo, aq)


def _s5_weights(a_re, a_im, log_dt, b_re, b_im, c_re, c_im):
    q = S5_Q
    a_re, a_im, b_re, b_im, c_re, c_im = (v.astype(F32) for v in (a_re, a_im, b_re, b_im, c_re, c_im))
    dt = jnp.exp(log_dt.astype(F32))[..., None]
    zr, zi = a_re * dt, a_im * dt

    def power(tau):
        mag = jnp.exp(zr[:, :, None] * tau)
        return mag * jnp.cos(zi[:, :, None] * tau), mag * jnp.sin(zi[:, :, None] * tau)

    ab_r, ab_i = jnp.exp(zr) * jnp.cos(zi), jnp.exp(zr) * jnp.sin(zi)
    den = a_re * a_re + a_im * a_im
    f_r = ((ab_r - 1.0) * a_re + ab_i * a_im) / den
    f_i = (ab_i * a_re - (ab_r - 1.0) * a_im) / den
    bb_r = f_r[..., None] * b_re - f_i[..., None] * b_im
    bb_i = f_r[..., None] * b_im + f_i[..., None] * b_re

    taus = jnp.arange(q + 1, dtype=F32)[:, None]
    pw_r, pw_i = power(taus)

    ca_r = c_re[:, :, None] * pw_r[:, :, :q, None] - c_im[:, :, None] * pw_i[:, :, :q, None]
    ca_i = c_re[:, :, None] * pw_i[:, :, :q, None] + c_im[:, :, None] * pw_r[:, :, :q, None]
    kmat = (jnp.einsum("dgtop,dgpi->dgito", ca_r, bb_r, precision=HIGHEST)
            - jnp.einsum("dgtop,dgpi->dgito", ca_i, bb_i, precision=HIGHEST))
    zpad = jnp.zeros_like(kmat[:, :, :, :q - 1])
    kflat = jnp.stack([jnp.concatenate([zpad[0], kmat[0]], axis=2),
                       jnp.concatenate([kmat[1, :, :, ::-1], zpad[1]], axis=2)])
    kflat = kflat.reshape(2, C_GROUPS, C_GROUP, (2 * q - 1) * C_GROUP)
    m_intra = jnp.stack([kflat[..., (q - 1 - s) * C_GROUP:(q - 1 - s) * C_GROUP + S5_XW] for s in range(q)],
                        axis=2).reshape(2, C_GROUPS, S5_XW, S5_XW)

    ps_r = jnp.stack([pw_r[0, :, q - 1::-1][:, :q], pw_r[1, :, :q]])
    ps_i = jnp.stack([pw_i[0, :, q - 1::-1][:, :q], pw_i[1, :, :q]])
    bt_r, bt_i = bb_r.transpose(0, 1, 3, 2), bb_i.transpose(0, 1, 3, 2)
    st_r = (ps_r[:, :, :, None] * bt_r[:, :, None] - ps_i[:, :, :, None] * bt_i[:, :, None])
    st_i = (ps_r[:, :, :, None] * bt_i[:, :, None] + ps_i[:, :, :, None] * bt_r[:, :, None])
    st_r = st_r.reshape(2, C_GROUPS, S5_XW, C_STATE)
    st_i = st_i.reshape(2, C_GROUPS, S5_XW, C_STATE)
    w1 = jnp.concatenate([m_intra, st_r, st_i, st_i, st_r], axis=-1).astype(BF16)

    col = np.arange(S5_XW)
    e_t = jnp.asarray(col[None, :] // C_GROUP == np.arange(q)[:, None], F32)
    e_o = jnp.asarray(col[None, :] % C_GROUP == np.arange(C_GROUP)[:, None], F32)
    po_r = jnp.stack([pw_r[0, :, 1:q + 1], pw_r[1, :, q:0:-1]])
    po_i = jnp.stack([pw_i[0, :, 1:q + 1], pw_i[1, :, q:0:-1]])
    xp_r = jnp.einsum("dgtp,tx->dgpx", po_r, e_t, precision=HIGHEST)
    xp_i = jnp.einsum("dgtp,tx->dgpx", po_i, e_t, precision=HIGHEST)
    xc_r = jnp.einsum("dgop,ox->dgpx", c_re, e_o, precision=HIGHEST)
    xc_i = jnp.einsum("dgop,ox->dgpx", c_im, e_o, precision=HIGHEST)
    m_out = jnp.concatenate([xc_r * xp_r - xc_i * xp_i, -(xc_r * xp_i + xc_i * xp_r)], axis=2).astype(BF16)

    ar, ai = pw_r[:, :, q], pw_i[:, :, q]
    aq3 = jnp.stack([jnp.concatenate([ar, ar], -1), jnp.concatenate([-ai, ai], -1),
                     jnp.concatenate([ai, -ai], -1)], axis=1)
    return w1, m_out, aq3


def _glu_kernel(u_ref, yl_ref, yc_ref, dsk_ref, wa_ref, wg_ref, h_ref, mod_ref, o_ref, z_ref, *, n_lat):
    i = pl.program_id(0)

    @pl.when(pl.program_id(1) == 0)
    def _():
        y = jnp.where(i >= n_lat, yc_ref[...], yl_ref[...])
        z_ref[...] = jax.nn.gelu(dsk_ref[...] * u_ref[...] + y).astype(BF16)

    z = z_ref[...]
    a = _dot(z, wa_ref[...])
    g = _dot(z, wg_ref[...])
    o_ref[...] = h_ref[...] + _mod_row(mod_ref, 5) * (a * jax.nn.sigmoid(g))


def _glu(u, y_lat, y_ctx, d_skip, w_glu, h, mod, rows):
    nj = D // TN
    n_lat = N_LAT // TM
    return pl.pallas_call(
        functools.partial(_glu_kernel, n_lat=n_lat),
        grid=(rows // TM, nj),
        in_specs=[pl.BlockSpec((TM, D), lambda i, j: (i, 0))] + _split_specs(n_lat) + [
            pl.BlockSpec((1, D), lambda i, j: (0, 0)),
            pl.BlockSpec((D, TN), lambda i, j: (0, j)),
            pl.BlockSpec((D, TN), lambda i, j: (0, nj + j)),
            pl.BlockSpec((TM, TN), lambda i, j: (i, j)),
            pl.BlockSpec((1, N_MOD, TN), lambda i, j: (_seg(i), 0, j)),
        ],
        out_specs=pl.BlockSpec((TM, TN), lambda i, j: (i, j)),
        out_shape=jax.ShapeDtypeStruct((rows, D), F32),
        scratch_shapes=[pltpu.VMEM((TM, D), BF16)],
        compiler_params=_params("parallel", "arbitrary"),
        name="s5_glu",
    )(u, y_lat, y_ctx, d_skip, w_glu, w_glu, h, mod)


def kernel(x, c, ctx, c_ctx, w_ada, b_ada, norm_g, ffn_w_gu, ffn_w_down, a_w_in, a_v_gain, a_w_s, a_b_s, a_w_out, b_w_qkv, b_q_gain, b_k_gain, b_rpb, b_w_out, c_w_in, c_a_re, c_a_im, c_log_dt, c_b_re, c_b_im, c_c_re, c_c_im, c_d, c_w_glu):
    h = jnp.concatenate([x.reshape(N_LAT, D), ctx.reshape(N_CTX, D)], axis=0).astype(F32)
    cond8 = jnp.concatenate([c, c_ctx[None], jnp.zeros((8 - BATCH - 1, D), c.dtype)], axis=0).astype(F32)
    mods = _adaln(cond8, w_ada.astype(F32), b_ada.astype(F32))
    norm_g = norm_g.astype(F32)
    ffn_w_gu = ffn_w_gu.astype(F32)
    ffn_w_down = ffn_w_down.astype(F32)

    for i in range(DEPTH):
        kind, j = i % 3, i // 3
        last = i == DEPTH - 1
        rows = N_LAT if last else N_ALL
        mod, gains = mods[i], norm_g[i]

        h = _ffn(h, mod, gains, ffn_w_gu[i, 0], ffn_w_down[i, 0], 0, rows)

        if kind == 0:
            z = _gmlp(h, mod, gains, a_w_in[j].astype(BF16), a_v_gain[j].astype(F32)[None],
                      a_w_s[j].astype(BF16), a_b_s[j].astype(F32).T, rows)
            h = _mm_res(z, None, a_w_out[j].astype(BF16), h, mod, rows)
        elif kind == 1:
            head_gains = jnp.stack([b_q_gain[j], b_k_gain[j]]).astype(F32)
            qkv = _qkv(h, mod, gains, b_w_qkv[j].astype(BF16), head_gains, rows)
            o_lat, o_ctx = _attention(qkv, _attn_bias_table(b_rpb[j].astype(F32)))
            h = _mm_res(o_lat, o_ctx, b_w_out[j].astype(BF16), h, mod, rows)
        else:
            u = _proj(h, mod, gains, c_w_in[j].astype(BF16), rows)
            w1, mo, aq = _s5_weights(c_a_re[j], c_a_im[j], c_log_dt[j], c_b_re[j], c_b_im[j],
                                     c_c_re[j], c_c_im[j])
            y_lat, y_ctx = _s5_scan(u, w1, mo, aq)
            h = _glu(u, y_lat, y_ctx, c_d[j].astype(F32)[None], c_w_glu[j].astype(BF16), h, mod, rows)

        h = _ffn(h, mod, gains, ffn_w_gu[i, 1], ffn_w_down[i, 1], 2, rows)

    return h[:N_LAT].reshape(BATCH, SEQ, D).astype(x.dtype)
```

```python
import functools

import numpy as np
import jax
import jax.numpy as jnp
from jax import lax
from jax.experimental import pallas as pl
from jax.experimental.pallas import tpu as pltpu

F32 = jnp.float32
BF16 = jnp.bfloat16
HIGHEST = lax.Precision.HIGHEST

D = 2048
BATCH = 2
SEQ = 4096
CTX = 256
DEPTH = 4
N_LAT = BATCH * SEQ
N_CTX = BATCH * CTX
N_ALL = N_LAT + N_CTX
N_MOD = 9
D_FF = 5632
RMS_EPS = 1e-6
NEG_INF = -1e30
GRID_W = 64
ROWS = SEQ // GRID_W
CHUNK = 128
A_GROUPS = 16
N_HEADS = 16
HEAD_DIM = 128
WIN_H = 8
WIN_W = 16
C_GROUP = 16
C_GROUPS = D // C_GROUP
C_STATE = 64
LANES = 128
NORM_ROWS = 16
NORM_UNROLL = 4

TM = 512
TM_F = 1024
TF = 256
TN = 512
ADA_TN = 1024
S5_Q = 16
S5_GB = LANES // C_GROUP
S5_XW = S5_Q * C_GROUP
VMEM_LIMIT = 56 * 1024 * 1024


def _params(*sem):
    return pltpu.CompilerParams(dimension_semantics=sem, vmem_limit_bytes=VMEM_LIMIT)


def _seg(i, tm=TM):
    return jnp.minimum((i * tm) // SEQ, 2)


def _mod_row(mod_ref, r):
    return mod_ref[0, r:r + 1, :]


def _norm_mod_store(h_ref, xn_ref, m, g_ref, mod_ref, s):
    gs = g_ref[s:s + 1, :] * (1.0 + _mod_row(mod_ref, 3 * s + 1))
    shift = _mod_row(mod_ref, 3 * s)

    def chunk(r, carry):
        rows = pl.ds(pl.multiple_of(r * NORM_ROWS, NORM_ROWS), NORM_ROWS)
        x = h_ref[rows, :]
        ms = jnp.mean(x * x, axis=-1, keepdims=True)
        xn_ref[rows, :] = (x * lax.rsqrt(ms + RMS_EPS) * gs + shift).astype(BF16)
        return carry

    lax.fori_loop(0, m // NORM_ROWS, chunk, 0, unroll=NORM_UNROLL)


def _dot(a, b):
    return jnp.dot(a, b, preferred_element_type=F32)


def _dot_nt(a, b):
    return lax.dot_general(a, b, (((1,), (1,)), ((), ())), preferred_element_type=F32)


def _ada_kernel(c_ref, w_ref, b_ref, o_ref):
    c = c_ref[...]
    a = (c * jax.nn.sigmoid(c)).astype(BF16)
    o_ref[0] = _dot(a, w_ref[0].astype(BF16)) + b_ref[0]


def _adaln(cond8, w_ada, b_ada):
    n = N_MOD * D
    out = pl.pallas_call(
        _ada_kernel,
        grid=(DEPTH, n // ADA_TN),
        in_specs=[
            pl.BlockSpec((8, D), lambda l, j: (0, 0)),
            pl.BlockSpec((1, D, ADA_TN), lambda l, j: (l, 0, j)),
            pl.BlockSpec((1, 1, ADA_TN), lambda l, j: (l, 0, j)),
        ],
        out_specs=pl.BlockSpec((1, 8, ADA_TN), lambda l, j: (l, 0, j)),
        out_shape=jax.ShapeDtypeStruct((DEPTH, 8, n), F32),
        compiler_params=_params("parallel", "parallel"),
        name="adaln",
    )(cond8, w_ada, b_ada.reshape(DEPTH, 1, n))
    return out[:, :3].reshape(DEPTH, 3, N_MOD, D)


def _ffn_kernel(h_ref, mod_ref, g_ref, wg_ref, wu_ref, wd_ref, o_ref, xn_ref, *, s, nk, n_full, tail):
    i = pl.program_id(0)
    k = pl.program_id(1)

    def body(m):
        @pl.when(k == 0)
        def _():
            _norm_mod_store(h_ref, xn_ref, m, g_ref, mod_ref, s)
            o_ref[:m, :] = jnp.zeros((m, D), F32)

        xn = xn_ref[:m, :]
        g = _dot(xn, wg_ref[0, 0].astype(BF16))
        u = _dot(xn, wu_ref[0, 0].astype(BF16))
        a = (g * jax.nn.sigmoid(g)) * u
        o_ref[:m, :] += _dot(a.astype(BF16), wd_ref[0, 0].astype(BF16))

        @pl.when(k == nk - 1)
        def _():
            o_ref[:m, :] = h_ref[:m, :] + (0.5 * _mod_row(mod_ref, 3 * s + 2)) * o_ref[:m, :]

    if tail:
        pl.when(i < n_full)(lambda: body(TM_F))
        pl.when(i >= n_full)(lambda: body(tail))
    else:
        body(TM_F)


def _ffn(h, mod, gains, w_gu, w_down, layer, half, rows):
    s = 2 * half
    nk = D_FF // TF
    n_full, tail = divmod(rows, TM_F)
    return pl.pallas_call(
        functools.partial(_ffn_kernel, s=s, nk=nk, n_full=n_full, tail=tail),
        grid=(n_full + (1 if tail else 0), nk),
        in_specs=[
            pl.BlockSpec((TM_F, D), lambda i, k: (i, 0)),
            pl.BlockSpec((1, N_MOD, D), lambda i, k: (_seg(i, TM_F), 0, 0)),
            pl.BlockSpec((3, D), lambda i, k: (0, 0)),
            pl.BlockSpec((1, 1, D, TF), lambda i, k: (layer, half, 0, k)),
            pl.BlockSpec((1, 1, D, TF), lambda i, k: (layer, half, 0, nk + k)),
            pl.BlockSpec((1, 1, TF, D), lambda i, k: (layer, half, k, 0)),
        ],
        out_specs=pl.BlockSpec((TM_F, D), lambda i, k: (i, 0), pipeline_mode=pl.Buffered(1)),
        out_shape=jax.ShapeDtypeStruct((rows, D), F32),
        scratch_shapes=[pltpu.VMEM((TM_F, D), BF16)],
        compiler_params=_params("arbitrary", "arbitrary"),
        name="ffn",
    )(h, mod, gains, w_gu, w_gu, w_down)


def _mm_res_kernel(*refs, n_lat):
    if n_lat is None:
        z_ref, w_ref, h_ref, mod_ref, o_ref = refs
        o_ref[...] = h_ref[...] + _mod_row(mod_ref, 5) * _dot(z_ref[...], w_ref[...])
        return
    zl_ref, zc_ref, w_ref, h_ref, mod_ref, o_ref = refs
    i = pl.program_id(0)

    @pl.when(i < n_lat)
    def _():
        o_ref[...] = h_ref[...] + _mod_row(mod_ref, 5) * _dot(zl_ref[...], w_ref[...])

    @pl.when(i >= n_lat)
    def _():
        o_ref[...] = h_ref[...] + _mod_row(mod_ref, 5) * _dot(zc_ref[...], w_ref[...])


def _split_specs(n_lat):
    return [pl.BlockSpec((TM, D), lambda i, *_: (jnp.minimum(i, n_lat - 1), 0)),
            pl.BlockSpec((TM, D), lambda i, *_: (jnp.maximum(i - n_lat, 0), 0))]


def _mm_res(z, z_ctx, w, h, mod, rows):
    n_lat = None if z_ctx is None else N_LAT // TM
    z_specs = [pl.BlockSpec((TM, D), lambda i: (i, 0))] if z_ctx is None else _split_specs(n_lat)
    zs = (z,) if z_ctx is None else (z, z_ctx)
    return pl.pallas_call(
        functools.partial(_mm_res_kernel, n_lat=n_lat),
        grid=(rows // TM,),
        in_specs=z_specs + [
            pl.BlockSpec((D, D), lambda i: (0, 0)),
            pl.BlockSpec((TM, D), lambda i: (i, 0)),
            pl.BlockSpec((1, N_MOD, D), lambda i: (_seg(i), 0, 0)),
        ],
        out_specs=pl.BlockSpec((TM, D), lambda i: (i, 0)),
        out_shape=jax.ShapeDtypeStruct((rows, D), F32),
        compiler_params=_params("parallel"),
        name="mm_res",
    )(*zs, w, h, mod)


def _gmlp_kernel(h_ref, mod_ref, g_ref, win_ref, vg_ref, ws_ref, bst_ref, z_ref, xn_ref, y_ref, *, nj):
    j = pl.program_id(1)

    @pl.when(j == 0)
    def _():
        _norm_mod_store(h_ref, xn_ref, TM, g_ref, mod_ref, 1)

    y_ref[j] = jax.nn.gelu(_dot(xn_ref[...], win_ref[...]))

    @pl.when(j == nj - 1)
    def _():
        half = nj // 2
        ssq = jnp.zeros((TM, 1), F32)
        for jj in range(half, nj):
            yv = y_ref[jj]
            ssq = ssq + jnp.sum(yv * yv, axis=-1, keepdims=True)
        inv = lax.rsqrt(ssq / D + RMS_EPS)
        gpb = TN // CHUNK
        for cb in range(half):
            v = (y_ref[half + cb] * inv * vg_ref[:, cb * TN:(cb + 1) * TN]).astype(BF16)
            u = y_ref[cb]
            for gg in range(gpb):
                grp = cb * gpb + gg
                cs = slice(gg * CHUNK, (gg + 1) * CHUNK)
                for c in range(TM // CHUNK):
                    rs = slice(c * CHUNK, (c + 1) * CHUNK)
                    sg = _dot(ws_ref[grp], v[rs, cs]) + bst_ref[:, grp:grp + 1]
                    z_ref[rs, grp * CHUNK:(grp + 1) * CHUNK] = (u[rs, cs] * sg).astype(BF16)


def _gmlp(h, mod, gains, w_in, v_gain, w_s, b_s_t, rows):
    nj = (2 * D) // TN
    return pl.pallas_call(
        functools.partial(_gmlp_kernel, nj=nj),
        grid=(rows // TM, nj),
        in_specs=[
            pl.BlockSpec((TM, D), lambda i, j: (i, 0)),
            pl.BlockSpec((1, N_MOD, D), lambda i, j: (_seg(i), 0, 0)),
            pl.BlockSpec((3, D), lambda i, j: (0, 0)),
            pl.BlockSpec((D, TN), lambda i, j: (0, j)),
            pl.BlockSpec((1, D), lambda i, j: (0, 0)),
            pl.BlockSpec((A_GROUPS, CHUNK, CHUNK), lambda i, j: (0, 0, 0)),
            pl.BlockSpec((CHUNK, A_GROUPS), lambda i, j: (0, 0)),
        ],
        out_specs=pl.BlockSpec((TM, D), lambda i, j: (i, 0)),
        out_shape=jax.ShapeDtypeStruct((rows, D), BF16),
        scratch_shapes=[pltpu.VMEM((TM, D), BF16), pltpu.VMEM((nj, TM, TN), F32)],
        compiler_params=_params("parallel", "arbitrary"),
        name="gmlp",
    )(h, mod, gains, w_in, v_gain, w_s, b_s_t)


def _qkv_kernel(h_ref, mod_ref, g_ref, w_ref, hg_ref, o_ref, xn_ref):
    j = pl.program_id(1)

    @pl.when(j == 0)
    def _():
        _norm_mod_store(h_ref, xn_ref, TM, g_ref, mod_ref, 1)

    y = _dot(xn_ref[...], w_ref[...])
    part = j // (D // TN)

    @pl.when(part < 2)
    def _():
        gain = hg_ref[pl.ds(part, 1), :]
        for hh in range(TN // HEAD_DIM):
            cs = slice(hh * HEAD_DIM, (hh + 1) * HEAD_DIM)
            yh = y[:, cs]
            ms = jnp.mean(yh * yh, axis=-1, keepdims=True)
            o_ref[:, cs] = (yh * lax.rsqrt(ms + RMS_EPS) * gain).astype(BF16)

    @pl.when(part == 2)
    def _():
        o_ref[...] = y.astype(BF16)


def _qkv(h, mod, gains, w_qkv, head_gains, rows):
    return pl.pallas_call(
        _qkv_kernel,
        grid=(rows // TM, (3 * D) // TN),
        in_specs=[
            pl.BlockSpec((TM, D), lambda i, j: (i, 0)),
            pl.BlockSpec((1, N_MOD, D), lambda i, j: (_seg(i), 0, 0)),
            pl.BlockSpec((3, D), lambda i, j: (0, 0)),
            pl.BlockSpec((D, TN), lambda i, j: (0, j)),
            pl.BlockSpec((2, HEAD_DIM), lambda i, j: (0, 0)),
        ],
        out_specs=pl.BlockSpec((TM, TN), lambda i, j: (i, j)),
        out_shape=jax.ShapeDtypeStruct((rows, 3 * D), BF16),
        scratch_shapes=[pltpu.VMEM((TM, D), BF16)],
        compiler_params=_params("parallel", "arbitrary"),
        name="qkv",
    )(h, mod, gains, w_qkv, head_gains)


ATT_QR = 8
ATT_KR = 16
ATT_QB = ATT_QR * GRID_W
ATT_KB = ATT_KR * GRID_W
N_DR = 2 * WIN_H - 1
ATT_PATTERNS = ((0, 0), (ATT_QR, ATT_QR - WIN_H // 2), (ROWS - ATT_QR, ROWS - ATT_KR))


def _attn_kernel(q_ref, k_ref, v_ref, qc_ref, kc_ref, vc_ref, t_ref, o_ref, oc_ref, bias_ref):
    scale = HEAD_DIM ** -0.5
    left = lax.broadcasted_iota(jnp.int32, (GRID_W, 2 * GRID_W), 1) < GRID_W
    neg = jnp.full((GRID_W, 2 * GRID_W), NEG_INF, F32)

    for p, (r0, kr_base) in enumerate(ATT_PATTERNS):
        for qr in range(ATT_QR):
            r = r0 + qr
            rstart = min(max(r - WIN_H // 2, 0), ROWS - WIN_H)
            for kp in range(ATT_KR // 2):
                halves = []
                for kr in (kr_base + 2 * kp, kr_base + 2 * kp + 1):
                    inside = rstart <= kr < rstart + WIN_H
                    halves.append(t_ref[0, kr - r + WIN_H - 1] if inside else None)
                a, b = halves
                if a is None and b is None:
                    blk = neg
                else:
                    blk = jnp.where(left, neg if a is None else a, neg if b is None else b)
                bias_ref[p, qr * GRID_W:(qr + 1) * GRID_W, kp * 2 * GRID_W:(kp + 1) * 2 * GRID_W] = blk

    kc = kc_ref[...]
    vc = vc_ref[...]

    def block(q0, k0, p):
        q = q_ref[pl.ds(q0, ATT_QB), :]
        k = k_ref[pl.ds(k0, ATT_KB), :]
        v = v_ref[pl.ds(k0, ATT_KB), :]
        s_win = _dot_nt(q, k) * scale + bias_ref[p]
        s_ctx = _dot_nt(q, kc) * scale
        m = jnp.maximum(jnp.max(s_win, axis=-1, keepdims=True), jnp.max(s_ctx, axis=-1, keepdims=True))
        p_win = jnp.exp(s_win - m)
        p_ctx = jnp.exp(s_ctx - m)
        denom = jnp.sum(p_win, axis=-1, keepdims=True) + jnp.sum(p_ctx, axis=-1, keepdims=True)
        o = _dot(p_win.astype(BF16), v) + _dot(p_ctx.astype(BF16), vc)
        o_ref[pl.ds(q0, ATT_QB), :] = (o / denom).astype(BF16)

    block(0, 0, 0)

    def interior(rb, carry):
        q0 = pl.multiple_of(rb * ATT_QB, ATT_QB)
        k0 = pl.multiple_of(rb * ATT_QB - (WIN_H // 2) * GRID_W, (WIN_H // 2) * GRID_W)
        block(q0, k0, 1)
        return carry

    lax.fori_loop(1, ROWS // ATT_QR - 1, interior, 0)
    block(SEQ - ATT_QB, SEQ - ATT_KB, 2)

    s = _dot_nt(qc_ref[...], kc) * scale
    pc = jnp.exp(s - jnp.max(s, axis=-1, keepdims=True))
    oc = _dot(pc.astype(BF16), vc) / jnp.sum(pc, axis=-1, keepdims=True)
    oc_ref[...] = oc.astype(BF16)


def _attention(qkv, bias_tab):
    lat_blk = (SEQ, HEAD_DIM)
    ctx_blk = (CTX, HEAD_DIM)
    ctx0 = N_LAT // CTX
    return pl.pallas_call(
        _attn_kernel,
        grid=(N_HEADS, BATCH),
        in_specs=[
            pl.BlockSpec(lat_blk, lambda h, b: (b, h)),
            pl.BlockSpec(lat_blk, lambda h, b: (b, N_HEADS + h)),
            pl.BlockSpec(lat_blk, lambda h, b: (b, 2 * N_HEADS + h)),
            pl.BlockSpec(ctx_blk, lambda h, b: (ctx0 + b, h)),
            pl.BlockSpec(ctx_blk, lambda h, b: (ctx0 + b, N_HEADS + h)),
            pl.BlockSpec(ctx_blk, lambda h, b: (ctx0 + b, 2 * N_HEADS + h)),
            pl.BlockSpec((1, N_DR, GRID_W, 2 * GRID_W), lambda h, b: (h, 0, 0, 0)),
        ],
        out_specs=[
            pl.BlockSpec(lat_blk, lambda h, b: (b, h)),
            pl.BlockSpec(ctx_blk, lambda h, b: (b, h)),
        ],
        out_shape=[
            jax.ShapeDtypeStruct((N_LAT, D), BF16),
            jax.ShapeDtypeStruct((N_CTX, D), BF16),
        ],
        scratch_shapes=[pltpu.VMEM((len(ATT_PATTERNS), ATT_QB, ATT_KB), F32)],
        compiler_params=_params("parallel", "parallel"),
        name="nat_attention",
    )(qkv, qkv, qkv, qkv, qkv, qkv, bias_tab)


def _attn_bias_table(rpb):
    qcol = np.arange(GRID_W)[:, None]
    kcol = np.arange(2 * GRID_W)[None, :] % GRID_W
    cstart = np.clip(qcol - WIN_W // 2, 0, GRID_W - WIN_W)
    col_valid = (kcol >= cstart) & (kcol < cstart + WIN_W)
    dc_idx = np.clip(kcol - qcol, 1 - WIN_W, WIN_W - 1) + (WIN_W - 1)
    onehot = (dc_idx[None] == np.arange(2 * WIN_W - 1)[:, None, None]) & col_valid[None]
    onehot = jnp.asarray(onehot.reshape(2 * WIN_W - 1, -1), F32)
    mask = jnp.asarray(np.where(col_valid, 0.0, NEG_INF).reshape(-1), F32)
    t = jnp.dot(rpb.reshape(N_HEADS * N_DR, 2 * WIN_W - 1), onehot, precision=HIGHEST) + mask
    return t.reshape(N_HEADS, N_DR, GRID_W, 2 * GRID_W)


def _proj_kernel(h_ref, mod_ref, g_ref, w_ref, o_ref, xn_ref):
    @pl.when(pl.program_id(1) == 0)
    def _():
        _norm_mod_store(h_ref, xn_ref, TM, g_ref, mod_ref, 1)

    o_ref[...] = _dot(xn_ref[...], w_ref[...])


def _proj(h, mod, gains, w, rows):
    return pl.pallas_call(
        _proj_kernel,
        grid=(rows // TM, D // TN),
        in_specs=[
            pl.BlockSpec((TM, D), lambda i, j: (i, 0)),
            pl.BlockSpec((1, N_MOD, D), lambda i, j: (_seg(i), 0, 0)),
            pl.BlockSpec((3, D), lambda i, j: (0, 0)),
            pl.BlockSpec((D, TN), lambda i, j: (0, j)),
        ],
        out_specs=pl.BlockSpec((TM, TN), lambda i, j: (i, j)),
        out_shape=jax.ShapeDtypeStruct((rows, D), F32),
        scratch_shapes=[pltpu.VMEM((TM, D), BF16)],
        compiler_params=_params("parallel", "arbitrary"),
        name="s5_in_proj",
    )(h, mod, gains, w)


S5_CL = SEQ // S5_Q
S5_CC = CTX // S5_Q
S5_HALF = LANES // C_GROUP


def _s5_kernel(ul_ref, uc_ref, mi_ref, ms_ref, mo_ref, aq_ref, yl_ref, yc_ref,
               ut_scr, xl_scr, xc_scr, s_scr, ssw_scr, hp_scr, yl_acc, yc_acc):
    d = pl.program_id(2)
    ns = 2 * C_STATE

    def lane_group(n):
        return lax.broadcasted_iota(jnp.int32, (n, LANES), 1) // C_GROUP

    def to_chunks(u_ref, x_scr, nch):
        grp = lane_group(nch)
        for t in range(S5_Q):
            ut_scr[t, :nch, :] = u_ref[pl.ds(t, nch, stride=S5_Q), :]
        for gl in range(S5_GB):
            for hf in range(S5_Q // S5_HALF):
                acc = None
                for tl in range(S5_HALF):
                    piece = ut_scr[hf * S5_HALF + tl, :nch, :]
                    shift = ((tl - gl) * C_GROUP) % LANES
                    if shift:
                        piece = pltpu.roll(piece, shift, 1)
                    acc = piece if acc is None else jnp.where(grp == tl, piece, acc)
                x_scr[gl, :, hf * LANES:(hf + 1) * LANES] = acc.astype(BF16)

    def from_chunks(y_acc, y_ref, nch):
        grp = lane_group(nch)
        for t in range(S5_Q):
            hf, tl = divmod(t, S5_HALF)
            acc = None
            for gl in range(S5_GB):
                piece = y_acc[gl, :, hf * LANES:(hf + 1) * LANES]
                shift = ((gl - tl) * C_GROUP) % LANES
                if shift:
                    piece = pltpu.roll(piece, shift, 1)
                acc = piece if acc is None else jnp.where(grp == gl, piece, acc)
            y_ref[pl.ds(t, nch, stride=S5_Q), :] = acc

    @pl.when(d == 0)
    def _():
        to_chunks(ul_ref, xl_scr, S5_CL)
        to_chunks(uc_ref, xc_scr, S5_CC)
        yl_acc[...] = jnp.zeros_like(yl_acc)
        yc_acc[...] = jnp.zeros_like(yc_acc)

    lat0 = S5_CC * S5_GB
    for j in range(S5_GB):
        xl, xc = xl_scr[j], xc_scr[j]
        yl_acc[j] += _dot(xl, mi_ref[0, j])
        yc_acc[j] += _dot(xc, mi_ref[0, j])
        sl = _dot(xl, ms_ref[0, j])
        sc = _dot(xc, ms_ref[0, j])
        s_scr[pl.ds(j, S5_CC, stride=S5_GB), :] = sc[:, :ns]
        s_scr[pl.ds(lat0 + j, S5_CL, stride=S5_GB), :] = sl[:, :ns]
        ssw_scr[pl.ds(j, S5_CC, stride=S5_GB), :] = sc[:, ns:]
        ssw_scr[pl.ds(lat0 + j, S5_CL, stride=S5_GB), :] = sl[:, ns:]

    a1 = aq_ref[0, 0]
    a2 = aq_ref[0, 1]
    a3 = aq_ref[0, 2]

    def step(pos, carry):
        hs, hsw = carry
        row = pl.multiple_of(pos * S5_GB, S5_GB)
        hp_scr[pl.ds(row, S5_GB), :] = hs
        s = s_scr[pl.ds(row, S5_GB), :]
        ssw = ssw_scr[pl.ds(row, S5_GB), :]
        return hs * a1 + hsw * a2 + s, hsw * a1 + hs * a3 + ssw

    def ctx_step(i, carry):
        return step(jnp.where(d == 0, i, S5_CC - 1 - i), carry)

    def lat_step(i, carry):
        return step(S5_CC + jnp.where(d == 0, i, S5_CL - 1 - i), carry)

    zero = jnp.zeros((S5_GB, ns), F32)
    carry = lax.fori_loop(0, S5_CC, ctx_step, (zero, zero))
    lax.fori_loop(0, S5_CL, lat_step, carry, unroll=4)

    for j in range(S5_GB):
        mo = mo_ref[0, j]
        hc = hp_scr[pl.ds(j, S5_CC, stride=S5_GB), :]
        hl = hp_scr[pl.ds(lat0 + j, S5_CL, stride=S5_GB), :]
        yc_acc[j] += _dot(hc.astype(BF16), mo)
        yl_acc[j] += _dot(hl.astype(BF16), mo)

    @pl.when(d == 1)
    def _():
        from_chunks(yl_acc, yl_ref, S5_CL)
        from_chunks(yc_acc, yc_ref, S5_CC)


def _s5_scan(u, mi, ms, mo, aq):
    ns = 2 * C_STATE
    nrow = (S5_CC + S5_CL) * S5_GB
    ctx0 = N_LAT // CTX
    return pl.pallas_call(
        _s5_kernel,
        grid=(BATCH, C_GROUPS // S5_GB, 2),
        in_specs=[
            pl.BlockSpec((SEQ, LANES), lambda b, g, d: (b, g)),
            pl.BlockSpec((CTX, LANES), lambda b, g, d: (ctx0 + b, g)),
            pl.BlockSpec((1, S5_GB, S5_XW, S5_XW), lambda b, g, d: (d, g, 0, 0)),
            pl.BlockSpec((1, S5_GB, S5_XW, 2 * ns), lambda b, g, d: (d, g, 0, 0)),
            pl.BlockSpec((1, S5_GB, ns, S5_XW), lambda b, g, d: (d, g, 0, 0)),
            pl.BlockSpec((1, 3, S5_GB, ns), lambda b, g, d: (d, 0, g, 0)),
        ],
        out_specs=[
            pl.BlockSpec((SEQ, LANES), lambda b, g, d: (b, g)),
            pl.BlockSpec((CTX, LANES), lambda b, g, d: (b, g)),
        ],
        out_shape=[
            jax.ShapeDtypeStruct((N_LAT, D), F32),
            jax.ShapeDtypeStruct((N_CTX, D), F32),
        ],
        scratch_shapes=[
            pltpu.VMEM((S5_Q, S5_CL, LANES), F32),
            pltpu.VMEM((S5_GB, S5_CL, S5_XW), BF16),
            pltpu.VMEM((S5_GB, S5_CC, S5_XW), BF16),
            pltpu.VMEM((nrow, ns), F32),
            pltpu.VMEM((nrow, ns), F32),
            pltpu.VMEM((nrow, ns), F32),
            pltpu.VMEM((S5_GB, S5_CL, S5_XW), F32),
            pltpu.VMEM((S5_GB, S5_CC, S5_XW), F32),
        ],
        compiler_params=_params("arbitrary", "arbitrary", "arbitrary"),
        name="s5_scan",
    )(u, u, mi, ms, mo, aq)


def _s5_weights(a_re, a_im, log_dt, b_re, b_im, c_re, c_im):
    q = S5_Q
    a_re, a_im, b_re, b_im, c_re, c_im = (v.astype(F32) for v in (a_re, a_im, b_re, b_im, c_re, c_im))
    dt = jnp.exp(log_dt.astype(F32))[..., None]
    zr, zi = a_re * dt, a_im * dt
    ab_r, ab_i = jnp.exp(zr) * jnp.cos(zi), jnp.exp(zr) * jnp.sin(zi)
    den = a_re * a_re + a_im * a_im
    f_r = ((ab_r - 1.0) * a_re + ab_i * a_im) / den
    f_i = (ab_i * a_re - (ab_r - 1.0) * a_im) / den
    bb_r = f_r[..., None] * b_re - f_i[..., None] * b_im
    bb_i = f_r[..., None] * b_im + f_i[..., None] * b_re

    taus = jnp.arange(q + 1, dtype=F32)[:, None]
    mag = jnp.exp(zr[:, :, None] * taus)
    pw_r, pw_i = mag * jnp.cos(zi[:, :, None] * taus), mag * jnp.sin(zi[:, :, None] * taus)

    col = np.arange(S5_XW)
    e_t = jnp.asarray(col[None, :] // C_GROUP == np.arange(q)[:, None], F32)
    e_o = jnp.asarray(col[None, :] % C_GROUP == np.arange(C_GROUP)[:, None], F32)
    xc_r = jnp.einsum("dgop,ox->dgpx", c_re, e_o, precision=HIGHEST)
    xc_i = jnp.einsum("dgop,ox->dgpx", c_im, e_o, precision=HIGHEST)

    def readout(p_r, p_i):
        xp_r = jnp.einsum("dgtp,tx->dgpx", p_r, e_t, precision=HIGHEST)
        xp_i = jnp.einsum("dgtp,tx->dgpx", p_i, e_t, precision=HIGHEST)
        return xc_r * xp_r - xc_i * xp_i, xc_r * xp_i + xc_i * xp_r

    wo_r, wo_i = readout(jnp.stack([pw_r[0, :, 1:q + 1], pw_r[1, :, q:0:-1]]),
                         jnp.stack([pw_i[0, :, 1:q + 1], pw_i[1, :, q:0:-1]]))
    m_out = jnp.concatenate([wo_r, -wo_i], axis=2).astype(BF16)

    ck_r, ck_i = readout(jnp.stack([pw_r[0, :, :q], pw_r[1, :, q - 1::-1]]),
                         jnp.stack([pw_i[0, :, :q], pw_i[1, :, q - 1::-1]]))
    kmat = jnp.sum(bb_r[:, :, :, :, None] * ck_r[:, :, :, None, :]
                   - bb_i[:, :, :, :, None] * ck_i[:, :, :, None, :], axis=2)
    zpad = jnp.zeros_like(kmat[0, :, :, :(q - 1) * C_GROUP])
    kflat = jnp.stack([jnp.concatenate([zpad, kmat[0]], axis=-1), jnp.concatenate([kmat[1], zpad], axis=-1)])
    m_intra = jnp.stack([kflat[..., (q - 1 - s) * C_GROUP:(q - 1 - s) * C_GROUP + S5_XW] for s in range(q)],
                        axis=2).reshape(2, C_GROUPS, S5_XW, S5_XW).astype(BF16)

    ps_r = jnp.stack([pw_r[0, :, q - 1::-1], pw_r[1, :, :q]])
    ps_i = jnp.stack([pw_i[0, :, q - 1::-1], pw_i[1, :, :q]])
    bt_r, bt_i = bb_r.transpose(0, 1, 3, 2), bb_i.transpose(0, 1, 3, 2)
    p4_r = jnp.concatenate([ps_r] * 4, axis=-1)[:, :, :, None]
    p4_i = jnp.concatenate([ps_i] * 4, axis=-1)[:, :, :, None]
    b4_a = jnp.concatenate([bt_r, bt_i, bt_i, bt_r], axis=-1)[:, :, None]
    b4_b = jnp.concatenate([-bt_i, bt_r, bt_r, -bt_i], axis=-1)[:, :, None]
    m_state = (p4_r * b4_a + p4_i * b4_b).reshape(2, C_GROUPS, S5_XW, 4 * C_STATE).astype(BF16)

    ar, ai = pw_r[:, :, q], pw_i[:, :, q]
    aq3 = jnp.stack([jnp.concatenate([ar, ar], -1), jnp.concatenate([-ai, ai], -1),
                     jnp.concatenate([ai, -ai], -1)], axis=1)
    return m_intra, m_state, m_out, aq3


def _glu_kernel(u_ref, yl_ref, yc_ref, dsk_ref, wa_ref, wg_ref, h_ref, mod_ref, o_ref, z_ref, *, n_lat):
    i = pl.program_id(0)

    @pl.when(pl.program_id(1) == 0)
    def _():
        y = jnp.where(i >= n_lat, yc_ref[...], yl_ref[...])
        z_ref[...] = jax.nn.gelu(dsk_ref[...] * u_ref[...] + y).astype(BF16)

    z = z_ref[...]
    a = _dot(z, wa_ref[...])
    g = _dot(z, wg_ref[...])
    o_ref[...] = h_ref[...] + _mod_row(mod_ref, 5) * (a * jax.nn.sigmoid(g))


def _glu(u, y_lat, y_ctx, d_skip, w_glu, h, mod, rows):
    nj = D // TN
    n_lat = N_LAT // TM
    return pl.pallas_call(
        functools.partial(_glu_kernel, n_lat=n_lat),
        grid=(rows // TM, nj),
        in_specs=[pl.BlockSpec((TM, D), lambda i, j: (i, 0))] + _split_specs(n_lat) + [
            pl.BlockSpec((1, D), lambda i, j: (0, 0)),
            pl.BlockSpec((D, TN), lambda i, j: (0, j)),
            pl.BlockSpec((D, TN), lambda i, j: (0, nj + j)),
            pl.BlockSpec((TM, TN), lambda i, j: (i, j)),
            pl.BlockSpec((1, N_MOD, TN), lambda i, j: (_seg(i), 0, j)),
        ],
        out_specs=pl.BlockSpec((TM, TN), lambda i, j: (i, j)),
        out_shape=jax.ShapeDtypeStruct((rows, D), F32),
        scratch_shapes=[pltpu.VMEM((TM, D), BF16)],
        compiler_params=_params("parallel", "arbitrary"),
        name="s5_glu",
    )(u, y_lat, y_ctx, d_skip, w_glu, w_glu, h, mod)


def kernel(x, c, ctx, c_ctx, w_ada, b_ada, norm_g, ffn_w_gu, ffn_w_down, a_w_in, a_v_gain, a_w_s, a_b_s, a_w_out, b_w_qkv, b_q_gain, b_k_gain, b_rpb, b_w_out, c_w_in, c_a_re, c_a_im, c_log_dt, c_b_re, c_b_im, c_c_re, c_c_im, c_d, c_w_glu):
    h = jnp.concatenate([x.reshape(N_LAT, D), ctx.reshape(N_CTX, D)], axis=0).astype(F32)
    cond8 = jnp.concatenate([c, c_ctx[None], jnp.zeros((8 - BATCH - 1, D), c.dtype)], axis=0).astype(F32)
    mods = _adaln(cond8, w_ada.astype(F32), b_ada.astype(F32))
    norm_g = norm_g.astype(F32)
    ffn_w_gu = ffn_w_gu.astype(F32)
    ffn_w_down = ffn_w_down.astype(F32)

    for i in range(DEPTH):
        kind, j = i % 3, i // 3
        last = i == DEPTH - 1
        rows = N_LAT if last else N_ALL
        mod, gains = mods[i], norm_g[i]

        h = _ffn(h, mod, gains, ffn_w_gu, ffn_w_down, i, 0, rows)

        if kind == 0:
            z = _gmlp(h, mod, gains, a_w_in[j].astype(BF16), a_v_gain[j].astype(F32)[None],
                      a_w_s[j].astype(BF16), a_b_s[j].astype(F32).T, rows)
            h = _mm_res(z, None, a_w_out[j].astype(BF16), h, mod, rows)
        elif kind == 1:
            head_gains = jnp.stack([b_q_gain[j], b_k_gain[j]]).astype(F32)
            qkv = _qkv(h, mod, gains, b_w_qkv[j].astype(BF16), head_gains, rows)
            o_lat, o_ctx = _attention(qkv, _attn_bias_table(b_rpb[j].astype(F32)))
            h = _mm_res(o_lat, o_ctx, b_w_out[j].astype(BF16), h, mod, rows)
        else:
            u = _proj(h, mod, gains, c_w_in[j].astype(BF16), rows)
            y_lat, y_ctx = _s5_scan(u, *_s5_weights(c_a_re[j], c_a_im[j], c_log_dt[j], c_b_re[j], c_b_im[j],
                                                    c_c_re[j], c_c_im[j]))
            h = _glu(u, y_lat, y_ctx, c_d[j].astype(F32)[None], c_w_glu[j].astype(BF16), h, mod, rows)

        h = _ffn(h, mod, gains, ffn_w_gu, ffn_w_down, i, 1, rows)

    return h[:N_LAT].reshape(BATCH, SEQ, D).astype(x.dtype)
```

```python
import functools

import numpy as np
import jax
import jax.numpy as jnp
from jax import lax
from jax.experimental import pallas as pl
from jax.experimental.pallas import tpu as pltpu

F32 = jnp.float32
BF16 = jnp.bfloat16
HIGHEST = lax.Precision.HIGHEST

D = 2048
BATCH = 2
SEQ = 4096
CTX = 256
DEPTH = 4
N_LAT = BATCH * SEQ
N_CTX = BATCH * CTX
N_ALL = N_LAT + N_CTX
N_MOD = 9
D_FF = 5632
RMS_EPS = 1e-6
NEG_INF = -1e30
GRID_W = 64
ROWS = SEQ // GRID_W
CHUNK = 128
A_GROUPS = 16
N_HEADS = 16
HEAD_DIM = 128
WIN_H = 8
WIN_W = 16
C_GROUP = 16
C_GROUPS = D // C_GROUP
C_STATE = 64
LANES = 128
NORM_ROWS = 16
NORM_UNROLL = 4

TM = 512
TM_F = 1024
TF = 256
TN = 512
ADA_TN = 1024
S5_Q = 16
S5_GB = LANES // C_GROUP
S5_XW = S5_Q * C_GROUP
VMEM_LIMIT = 56 * 1024 * 1024


def _params(*sem):
    return pltpu.CompilerParams(dimension_semantics=sem, vmem_limit_bytes=VMEM_LIMIT)


def _seg(i, tm=TM):
    return jnp.minimum((i * tm) // SEQ, 2)


def _mod_row(mod_ref, r):
    return mod_ref[0, r:r + 1, :]


def _norm_mod_store(h_ref, xn_ref, m, g_ref, mod_ref, s):
    gs = g_ref[s:s + 1, :] * (1.0 + _mod_row(mod_ref, 3 * s + 1))
    shift = _mod_row(mod_ref, 3 * s)

    def chunk(r, carry):
        rows = pl.ds(pl.multiple_of(r * NORM_ROWS, NORM_ROWS), NORM_ROWS)
        x = h_ref[rows, :]
        ms = jnp.mean(x * x, axis=-1, keepdims=True)
        xn_ref[rows, :] = (x * lax.rsqrt(ms + RMS_EPS) * gs + shift).astype(BF16)
        return carry

    lax.fori_loop(0, m // NORM_ROWS, chunk, 0, unroll=NORM_UNROLL)


def _dot(a, b):
    return jnp.dot(a, b, preferred_element_type=F32)


def _dot_nt(a, b):
    return lax.dot_general(a, b, (((1,), (1,)), ((), ())), preferred_element_type=F32)


def _ada_kernel(c_ref, w_ref, b_ref, o_ref):
    c = c_ref[...]
    a = (c * jax.nn.sigmoid(c)).astype(BF16)
    o_ref[0] = _dot(a, w_ref[0].astype(BF16)) + b_ref[0]


def _adaln(cond8, w_ada, b_ada):
    n = N_MOD * D
    out = pl.pallas_call(
        _ada_kernel,
        grid=(DEPTH, n // ADA_TN),
        in_specs=[
            pl.BlockSpec((8, D), lambda l, j: (0, 0)),
            pl.BlockSpec((1, D, ADA_TN), lambda l, j: (l, 0, j)),
            pl.BlockSpec((1, 1, ADA_TN), lambda l, j: (l, 0, j)),
        ],
        out_specs=pl.BlockSpec((1, 8, ADA_TN), lambda l, j: (l, 0, j)),
        out_shape=jax.ShapeDtypeStruct((DEPTH, 8, n), F32),
        compiler_params=_params("parallel", "parallel"),
        name="adaln",
    )(cond8, w_ada, b_ada.reshape(DEPTH, 1, n))
    return out[:, :3].reshape(DEPTH, 3, N_MOD, D)


def _ffn_kernel(h_ref, mod_ref, g_ref, wg_ref, wu_ref, wd_ref, o_ref, xn_ref, *, s, nk, n_full, tail):
    i = pl.program_id(0)
    k = pl.program_id(1)

    def body(m):
        @pl.when(k == 0)
        def _():
            _norm_mod_store(h_ref, xn_ref, m, g_ref, mod_ref, s)
            o_ref[:m, :] = jnp.zeros((m, D), F32)

        xn = xn_ref[:m, :]
        g = _dot(xn, wg_ref[0, 0].astype(BF16))
        u = _dot(xn, wu_ref[0, 0].astype(BF16))
        a = (g * jax.nn.sigmoid(g)) * u
        o_ref[:m, :] += _dot(a.astype(BF16), wd_ref[0, 0].astype(BF16))

        @pl.when(k == nk - 1)
        def _():
            o_ref[:m, :] = h_ref[:m, :] + (0.5 * _mod_row(mod_ref, 3 * s + 2)) * o_ref[:m, :]

    if tail:
        pl.when(i < n_full)(lambda: body(TM_F))
        pl.when(i >= n_full)(lambda: body(tail))
    else:
        body(TM_F)


def _ffn(h, mod, gains, w_gu, w_down, layer, half, rows):
    s = 2 * half
    nk = D_FF // TF
    n_full, tail = divmod(rows, TM_F)
    return pl.pallas_call(
        functools.partial(_ffn_kernel, s=s, nk=nk, n_full=n_full, tail=tail),
        grid=(n_full + (1 if tail else 0), nk),
        in_specs=[
            pl.BlockSpec((TM_F, D), lambda i, k: (i, 0)),
            pl.BlockSpec((1, N_MOD, D), lambda i, k: (_seg(i, TM_F), 0, 0)),
            pl.BlockSpec((3, D), lambda i, k: (0, 0)),
            pl.BlockSpec((1, 1, D, TF), lambda i, k: (layer, half, 0, k)),
            pl.BlockSpec((1, 1, D, TF), lambda i, k: (layer, half, 0, nk + k)),
            pl.BlockSpec((1, 1, TF, D), lambda i, k: (layer, half, k, 0)),
        ],
        out_specs=pl.BlockSpec((TM_F, D), lambda i, k: (i, 0), pipeline_mode=pl.Buffered(1)),
        out_shape=jax.ShapeDtypeStruct((rows, D), F32),
        scratch_shapes=[pltpu.VMEM((TM_F, D), BF16)],
        compiler_params=_params("arbitrary", "arbitrary"),
        name="ffn",
    )(h, mod, gains, w_gu, w_gu, w_down)


def _mm_res_kernel(*refs, n_lat):
    if n_lat is None:
        z_ref, w_ref, h_ref, mod_ref, o_ref = refs
        o_ref[...] = h_ref[...] + _mod_row(mod_ref, 5) * _dot(z_ref[...], w_ref[...])
        return
    zl_ref, zc_ref, w_ref, h_ref, mod_ref, o_ref = refs
    i = pl.program_id(0)

    @pl.when(i < n_lat)
    def _():
        o_ref[...] = h_ref[...] + _mod_row(mod_ref, 5) * _dot(zl_ref[...], w_ref[...])

    @pl.when(i >= n_lat)
    def _():
        o_ref[...] = h_ref[...] + _mod_row(mod_ref, 5) * _dot(zc_ref[...], w_ref[...])


def _split_specs(n_lat, tm=TM):
    return [pl.BlockSpec((tm, D), lambda i, *_: (jnp.minimum(i, n_lat - 1), 0)),
            pl.BlockSpec((tm, D), lambda i, *_: (jnp.maximum(i - n_lat, 0), 0))]


def _mm_res(z, z_ctx, w, h, mod, rows):
    n_lat = None if z_ctx is None else N_LAT // TM
    z_specs = [pl.BlockSpec((TM, D), lambda i: (i, 0))] if z_ctx is None else _split_specs(n_lat)
    zs = (z,) if z_ctx is None else (z, z_ctx)
    return pl.pallas_call(
        functools.partial(_mm_res_kernel, n_lat=n_lat),
        grid=(rows // TM,),
        in_specs=z_specs + [
            _resident((D, D)),
            pl.BlockSpec((TM, D), lambda i: (i, 0)),
            pl.BlockSpec((1, N_MOD, D), lambda i: (_seg(i), 0, 0)),
        ],
        out_specs=pl.BlockSpec((TM, D), lambda i: (i, 0)),
        out_shape=jax.ShapeDtypeStruct((rows, D), F32),
        compiler_params=_params("parallel"),
        name="mm_res",
    )(*zs, w, h, mod)


def _gmlp_kernel(h_ref, mod_ref, g_ref, win_ref, vg_ref, ws_ref, bst_ref, z_ref, xn_ref, y_ref, *, nj):
    _norm_mod_store(h_ref, xn_ref, TM, g_ref, mod_ref, 1)
    xn = xn_ref[...]
    for j in range(nj):
        y_ref[j] = jax.nn.gelu(_dot(xn, win_ref[:, j * TN:(j + 1) * TN]))

    half = nj // 2
    ssq = jnp.zeros((TM, 1), F32)
    for jj in range(half, nj):
        yv = y_ref[jj]
        ssq = ssq + jnp.sum(yv * yv, axis=-1, keepdims=True)
    inv = lax.rsqrt(ssq / D + RMS_EPS)
    gpb = TN // CHUNK
    for cb in range(half):
        v = (y_ref[half + cb] * inv * vg_ref[:, cb * TN:(cb + 1) * TN]).astype(BF16)
        u = y_ref[cb]
        for gg in range(gpb):
            grp = cb * gpb + gg
            cs = slice(gg * CHUNK, (gg + 1) * CHUNK)
            for c in range(TM // CHUNK):
                rs = slice(c * CHUNK, (c + 1) * CHUNK)
                sg = _dot(ws_ref[grp], v[rs, cs]) + bst_ref[:, grp:grp + 1]
                z_ref[rs, grp * CHUNK:(grp + 1) * CHUNK] = (u[rs, cs] * sg).astype(BF16)


def _resident(shape):
    return pl.BlockSpec(shape, lambda *_: (0,) * len(shape), pipeline_mode=pl.Buffered(1))


def _gmlp(h, mod, gains, w_in, v_gain, w_s, b_s_t, rows):
    nj = (2 * D) // TN
    return pl.pallas_call(
        functools.partial(_gmlp_kernel, nj=nj),
        grid=(rows // TM,),
        in_specs=[
            pl.BlockSpec((TM, D), lambda i: (i, 0)),
            pl.BlockSpec((1, N_MOD, D), lambda i: (_seg(i), 0, 0)),
            pl.BlockSpec((3, D), lambda i: (0, 0)),
            _resident((D, 2 * D)),
            pl.BlockSpec((1, D), lambda i: (0, 0)),
            _resident((A_GROUPS, CHUNK, CHUNK)),
            pl.BlockSpec((CHUNK, A_GROUPS), lambda i: (0, 0)),
        ],
        out_specs=pl.BlockSpec((TM, D), lambda i: (i, 0)),
        out_shape=jax.ShapeDtypeStruct((rows, D), BF16),
        scratch_shapes=[pltpu.VMEM((TM, D), BF16), pltpu.VMEM((nj, TM, TN), F32)],
        compiler_params=_params("arbitrary"),
        name="gmlp",
    )(h, mod, gains, w_in, v_gain, w_s, b_s_t)


def _qkv_kernel(h_ref, mod_ref, g_ref, w_ref, hg_ref, o_ref, xn_ref):
    _norm_mod_store(h_ref, xn_ref, TM, g_ref, mod_ref, 1)
    xn = xn_ref[...]
    for j in range((3 * D) // TN):
        y = _dot(xn, w_ref[:, j * TN:(j + 1) * TN])
        part = j // (D // TN)
        if part == 2:
            o_ref[:, j * TN:(j + 1) * TN] = y.astype(BF16)
            continue
        gain = hg_ref[part:part + 1, :]
        for hh in range(TN // HEAD_DIM):
            yh = y[:, hh * HEAD_DIM:(hh + 1) * HEAD_DIM]
            ms = jnp.mean(yh * yh, axis=-1, keepdims=True)
            c0 = j * TN + hh * HEAD_DIM
            o_ref[:, c0:c0 + HEAD_DIM] = (yh * lax.rsqrt(ms + RMS_EPS) * gain).astype(BF16)


def _qkv(h, mod, gains, w_qkv, head_gains, rows):
    return pl.pallas_call(
        _qkv_kernel,
        grid=(rows // TM,),
        in_specs=[
            pl.BlockSpec((TM, D), lambda i: (i, 0)),
            pl.BlockSpec((1, N_MOD, D), lambda i: (_seg(i), 0, 0)),
            pl.BlockSpec((3, D), lambda i: (0, 0)),
            _resident((D, 3 * D)),
            pl.BlockSpec((2, HEAD_DIM), lambda i: (0, 0)),
        ],
        out_specs=pl.BlockSpec((TM, 3 * D), lambda i: (i, 0)),
        out_shape=jax.ShapeDtypeStruct((rows, 3 * D), BF16),
        scratch_shapes=[pltpu.VMEM((TM, D), BF16)],
        compiler_params=_params("arbitrary"),
        name="qkv",
    )(h, mod, gains, w_qkv, head_gains)


ATT_QR = 8
ATT_KR = 16
ATT_QB = ATT_QR * GRID_W
ATT_KB = ATT_KR * GRID_W
N_DR = 2 * WIN_H - 1
ATT_PATTERNS = ((0, 0), (ATT_QR, ATT_QR - WIN_H // 2), (ROWS - ATT_QR, ROWS - ATT_KR))


def _attn_kernel(q_ref, k_ref, v_ref, qc_ref, kc_ref, vc_ref, t_ref, o_ref, oc_ref, bias_ref):
    scale = HEAD_DIM ** -0.5
    left = lax.broadcasted_iota(jnp.int32, (GRID_W, 2 * GRID_W), 1) < GRID_W
    neg = jnp.full((GRID_W, 2 * GRID_W), NEG_INF, F32)

    for p, (r0, kr_base) in enumerate(ATT_PATTERNS):
        for qr in range(ATT_QR):
            r = r0 + qr
            rstart = min(max(r - WIN_H // 2, 0), ROWS - WIN_H)
            for kp in range(ATT_KR // 2):
                halves = []
                for kr in (kr_base + 2 * kp, kr_base + 2 * kp + 1):
                    inside = rstart <= kr < rstart + WIN_H
                    halves.append(t_ref[0, kr - r + WIN_H - 1] if inside else None)
                a, b = halves
                if a is None and b is None:
                    blk = neg
                else:
                    blk = jnp.where(left, neg if a is None else a, neg if b is None else b)
                bias_ref[p, qr * GRID_W:(qr + 1) * GRID_W, kp * 2 * GRID_W:(kp + 1) * 2 * GRID_W] = blk

    kc = kc_ref[...]
    vc = vc_ref[...]

    def block(q0, k0, p):
        q = q_ref[pl.ds(q0, ATT_QB), :]
        k = k_ref[pl.ds(k0, ATT_KB), :]
        v = v_ref[pl.ds(k0, ATT_KB), :]
        s_win = _dot_nt(q, k) * scale + bias_ref[p]
        s_ctx = _dot_nt(q, kc) * scale
        m = jnp.maximum(jnp.max(s_win, axis=-1, keepdims=True), jnp.max(s_ctx, axis=-1, keepdims=True))
        p_win = jnp.exp(s_win - m)
        p_ctx = jnp.exp(s_ctx - m)
        denom = jnp.sum(p_win, axis=-1, keepdims=True) + jnp.sum(p_ctx, axis=-1, keepdims=True)
        o = _dot(p_win.astype(BF16), v) + _dot(p_ctx.astype(BF16), vc)
        o_ref[pl.ds(q0, ATT_QB), :] = (o / denom).astype(BF16)

    block(0, 0, 0)
    for rb in range(1, ROWS // ATT_QR - 1):
        block(rb * ATT_QB, rb * ATT_QB - (WIN_H // 2) * GRID_W, 1)
    block(SEQ - ATT_QB, SEQ - ATT_KB, 2)

    s = _dot_nt(qc_ref[...], kc) * scale
    pc = jnp.exp(s - jnp.max(s, axis=-1, keepdims=True))
    oc = _dot(pc.astype(BF16), vc) / jnp.sum(pc, axis=-1, keepdims=True)
    oc_ref[...] = oc.astype(BF16)


def _attention(qkv, bias_tab):
    lat_blk = (SEQ, HEAD_DIM)
    ctx_blk = (CTX, HEAD_DIM)
    ctx0 = N_LAT // CTX
    return pl.pallas_call(
        _attn_kernel,
        grid=(N_HEADS, BATCH),
        in_specs=[
            pl.BlockSpec(lat_blk, lambda h, b: (b, h)),
            pl.BlockSpec(lat_blk, lambda h, b: (b, N_HEADS + h)),
            pl.BlockSpec(lat_blk, lambda h, b: (b, 2 * N_HEADS + h)),
            pl.BlockSpec(ctx_blk, lambda h, b: (ctx0 + b, h)),
            pl.BlockSpec(ctx_blk, lambda h, b: (ctx0 + b, N_HEADS + h)),
            pl.BlockSpec(ctx_blk, lambda h, b: (ctx0 + b, 2 * N_HEADS + h)),
            pl.BlockSpec((1, N_DR, GRID_W, 2 * GRID_W), lambda h, b: (h, 0, 0, 0)),
        ],
        out_specs=[
            pl.BlockSpec(lat_blk, lambda h, b: (b, h)),
            pl.BlockSpec(ctx_blk, lambda h, b: (b, h)),
        ],
        out_shape=[
            jax.ShapeDtypeStruct((N_LAT, D), BF16),
            jax.ShapeDtypeStruct((N_CTX, D), BF16),
        ],
        scratch_shapes=[pltpu.VMEM((len(ATT_PATTERNS), ATT_QB, ATT_KB), F32)],
        compiler_params=_params("parallel", "parallel"),
        name="nat_attention",
    )(qkv, qkv, qkv, qkv, qkv, qkv, bias_tab)


def _attn_bias_table(rpb):
    qcol = np.arange(GRID_W)[:, None]
    kcol = np.arange(2 * GRID_W)[None, :] % GRID_W
    cstart = np.clip(qcol - WIN_W // 2, 0, GRID_W - WIN_W)
    col_valid = (kcol >= cstart) & (kcol < cstart + WIN_W)
    dc_idx = np.clip(kcol - qcol, 1 - WIN_W, WIN_W - 1) + (WIN_W - 1)
    onehot = (dc_idx[None] == np.arange(2 * WIN_W - 1)[:, None, None]) & col_valid[None]
    onehot = jnp.asarray(onehot.reshape(2 * WIN_W - 1, -1), F32)
    mask = jnp.asarray(np.where(col_valid, 0.0, NEG_INF).reshape(-1), F32)
    t = jnp.dot(rpb.reshape(N_HEADS * N_DR, 2 * WIN_W - 1), onehot, precision=HIGHEST) + mask
    return t.reshape(N_HEADS, N_DR, GRID_W, 2 * GRID_W)


def _proj_kernel(h_ref, mod_ref, g_ref, w_ref, o_ref, xn_ref):
    _norm_mod_store(h_ref, xn_ref, TM, g_ref, mod_ref, 1)
    xn = xn_ref[...]
    for j in range(D // TN):
        o_ref[:, j * TN:(j + 1) * TN] = _dot(xn, w_ref[:, j * TN:(j + 1) * TN])


def _proj(h, mod, gains, w, rows):
    return pl.pallas_call(
        _proj_kernel,
        grid=(rows // TM,),
        in_specs=[
            pl.BlockSpec((TM, D), lambda i: (i, 0)),
            pl.BlockSpec((1, N_MOD, D), lambda i: (_seg(i), 0, 0)),
            pl.BlockSpec((3, D), lambda i: (0, 0)),
            _resident((D, D)),
        ],
        out_specs=pl.BlockSpec((TM, D), lambda i: (i, 0)),
        out_shape=jax.ShapeDtypeStruct((rows, D), F32),
        scratch_shapes=[pltpu.VMEM((TM, D), BF16)],
        compiler_params=_params("arbitrary"),
        name="s5_in_proj",
    )(h, mod, gains, w)


S5_CL = SEQ // S5_Q
S5_CC = CTX // S5_Q
S5_HALF = LANES // C_GROUP


def _s5_kernel(ul_ref, uc_ref, mi_ref, ms_ref, mo_ref, aq_ref, yl_ref, yc_ref,
               xl_scr, xc_scr, s_scr, ssw_scr, hp_scr, yl_acc, yc_acc):
    d = pl.program_id(2)
    ns = 2 * C_STATE

    def granule_transpose(rows):
        n = rows[0].shape[0]
        granule = lax.broadcasted_iota(jnp.int32, (n, LANES), 1) // C_GROUP
        rows = list(rows)
        k = S5_GB // 2
        while k:
            hi = (granule & k) != 0
            for i in range(S5_GB):
                if i & k:
                    continue
                a, b = rows[i], rows[i + k]
                rows[i] = jnp.where(hi, pltpu.roll(b, k * C_GROUP, 1), a)
                rows[i + k] = jnp.where(hi, b, pltpu.roll(a, LANES - k * C_GROUP, 1))
            k //= 2
        return rows

    def to_chunks(u_ref, x_scr, nch):
        for hf in range(S5_Q // S5_HALF):
            by_token = [u_ref[pl.ds(hf * S5_HALF + tl, nch, stride=S5_Q), :] for tl in range(S5_HALF)]
            for gl, x in enumerate(granule_transpose(by_token)):
                x_scr[gl, :, hf * LANES:(hf + 1) * LANES] = x.astype(BF16)

    def from_chunks(y_acc, y_ref, nch):
        for hf in range(S5_Q // S5_HALF):
            by_group = [y_acc[gl, :, hf * LANES:(hf + 1) * LANES] for gl in range(S5_GB)]
            for tl, y in enumerate(granule_transpose(by_group)):
                y_ref[pl.ds(hf * S5_HALF + tl, nch, stride=S5_Q), :] = y

    @pl.when(d == 0)
    def _():
        to_chunks(ul_ref, xl_scr, S5_CL)
        to_chunks(uc_ref, xc_scr, S5_CC)
        yl_acc[...] = jnp.zeros_like(yl_acc)
        yc_acc[...] = jnp.zeros_like(yc_acc)

    lat0 = S5_CC * S5_GB
    for j in range(S5_GB):
        xl, xc = xl_scr[j], xc_scr[j]
        yl_acc[j] += _dot(xl, mi_ref[0, j])
        yc_acc[j] += _dot(xc, mi_ref[0, j])
        sl = _dot(xl, ms_ref[0, j])
        sc = _dot(xc, ms_ref[0, j])
        s_scr[pl.ds(j, S5_CC, stride=S5_GB), :] = sc[:, :ns]
        s_scr[pl.ds(lat0 + j, S5_CL, stride=S5_GB), :] = sl[:, :ns]
        ssw_scr[pl.ds(j, S5_CC, stride=S5_GB), :] = sc[:, ns:]
        ssw_scr[pl.ds(lat0 + j, S5_CL, stride=S5_GB), :] = sl[:, ns:]

    a1 = aq_ref[0, 0]
    a2 = aq_ref[0, 1]
    a3 = aq_ref[0, 2]

    def step(pos, carry):
        hs, hsw = carry
        row = pl.multiple_of(pos * S5_GB, S5_GB)
        hp_scr[pl.ds(row, S5_GB), :] = hs
        s = s_scr[pl.ds(row, S5_GB), :]
        ssw = ssw_scr[pl.ds(row, S5_GB), :]
        return hs * a1 + hsw * a2 + s, hsw * a1 + hs * a3 + ssw

    def ctx_step(i, carry):
        return step(jnp.where(d == 0, i, S5_CC - 1 - i), carry)

    def lat_step(i, carry):
        return step(S5_CC + jnp.where(d == 0, i, S5_CL - 1 - i), carry)

    zero = jnp.zeros((S5_GB, ns), F32)
    carry = lax.fori_loop(0, S5_CC, ctx_step, (zero, zero))
    lax.fori_loop(0, S5_CL, lat_step, carry, unroll=4)

    for j in range(S5_GB):
        mo = mo_ref[0, j]
        hc = hp_scr[pl.ds(j, S5_CC, stride=S5_GB), :]
        hl = hp_scr[pl.ds(lat0 + j, S5_CL, stride=S5_GB), :]
        yc_acc[j] += _dot(hc.astype(BF16), mo)
        yl_acc[j] += _dot(hl.astype(BF16), mo)

    @pl.when(d == 1)
    def _():
        from_chunks(yl_acc, yl_ref, S5_CL)
        from_chunks(yc_acc, yc_ref, S5_CC)


def _s5_scan(u, mi, ms, mo, aq):
    ns = 2 * C_STATE
    nrow = (S5_CC + S5_CL) * S5_GB
    ctx0 = N_LAT // CTX
    return pl.pallas_call(
        _s5_kernel,
        grid=(BATCH, C_GROUPS // S5_GB, 2),
        in_specs=[
            pl.BlockSpec((SEQ, LANES), lambda b, g, d: (b, g)),
            pl.BlockSpec((CTX, LANES), lambda b, g, d: (ctx0 + b, g)),
            pl.BlockSpec((1, S5_GB, S5_XW, S5_XW), lambda b, g, d: (d, g, 0, 0)),
            pl.BlockSpec((1, S5_GB, S5_XW, 2 * ns), lambda b, g, d: (d, g, 0, 0)),
            pl.BlockSpec((1, S5_GB, ns, S5_XW), lambda b, g, d: (d, g, 0, 0)),
            pl.BlockSpec((1, 3, S5_GB, ns), lambda b, g, d: (d, 0, g, 0)),
        ],
        out_specs=[
            pl.BlockSpec((SEQ, LANES), lambda b, g, d: (b, g)),
            pl.BlockSpec((CTX, LANES), lambda b, g, d: (b, g)),
        ],
        out_shape=[
            jax.ShapeDtypeStruct((N_LAT, D), F32),
            jax.ShapeDtypeStruct((N_CTX, D), F32),
        ],
        scratch_shapes=[
            pltpu.VMEM((S5_GB, S5_CL, S5_XW), BF16),
            pltpu.VMEM((S5_GB, S5_CC, S5_XW), BF16),
            pltpu.VMEM((nrow, ns), F32),
            pltpu.VMEM((nrow, ns), F32),
            pltpu.VMEM((nrow, ns), F32),
            pltpu.VMEM((S5_GB, S5_CL, S5_XW), F32),
            pltpu.VMEM((S5_GB, S5_CC, S5_XW), F32),
        ],
        compiler_params=_params("arbitrary", "arbitrary", "arbitrary"),
        name="s5_scan",
    )(u, u, mi, ms, mo, aq)


def _s5_weights(a_re, a_im, log_dt, b_re, b_im, c_re, c_im):
    q = S5_Q
    a_re, a_im, b_re, b_im, c_re, c_im = (v.astype(F32) for v in (a_re, a_im, b_re, b_im, c_re, c_im))
    dt = jnp.exp(log_dt.astype(F32))[..., None]
    zr, zi = a_re * dt, a_im * dt
    ab_r, ab_i = jnp.exp(zr) * jnp.cos(zi), jnp.exp(zr) * jnp.sin(zi)
    den = a_re * a_re + a_im * a_im
    f_r = ((ab_r - 1.0) * a_re + ab_i * a_im) / den
    f_i = (ab_i * a_re - (ab_r - 1.0) * a_im) / den
    bb_r = f_r[..., None] * b_re - f_i[..., None] * b_im
    bb_i = f_r[..., None] * b_im + f_i[..., None] * b_re

    taus = jnp.arange(q + 1, dtype=F32)[:, None]
    mag = jnp.exp(zr[:, :, None] * taus)
    pw_r, pw_i = mag * jnp.cos(zi[:, :, None] * taus), mag * jnp.sin(zi[:, :, None] * taus)

    col = np.arange(S5_XW)
    e_t = jnp.asarray(col[None, :] // C_GROUP == np.arange(q)[:, None], F32)
    e_o = jnp.asarray(col[None, :] % C_GROUP == np.arange(C_GROUP)[:, None], F32)
    xc_r = jnp.einsum("dgop,ox->dgpx", c_re, e_o, precision=HIGHEST)
    xc_i = jnp.einsum("dgop,ox->dgpx", c_im, e_o, precision=HIGHEST)

    def readout(p_r, p_i):
        xp_r = jnp.einsum("dgtp,tx->dgpx", p_r, e_t, precision=HIGHEST)
        xp_i = jnp.einsum("dgtp,tx->dgpx", p_i, e_t, precision=HIGHEST)
        return xc_r * xp_r - xc_i * xp_i, xc_r * xp_i + xc_i * xp_r

    wo_r, wo_i = readout(jnp.stack([pw_r[0, :, 1:q + 1], pw_r[1, :, q:0:-1]]),
                         jnp.stack([pw_i[0, :, 1:q + 1], pw_i[1, :, q:0:-1]]))
    m_out = jnp.concatenate([wo_r, -wo_i], axis=2).astype(BF16)

    ck_r, ck_i = readout(jnp.stack([pw_r[0, :, :q], pw_r[1, :, q - 1::-1]]),
                         jnp.stack([pw_i[0, :, :q], pw_i[1, :, q - 1::-1]]))
    kmat = jnp.sum(bb_r[:, :, :, :, None] * ck_r[:, :, :, None, :]
                   - bb_i[:, :, :, :, None] * ck_i[:, :, :, None, :], axis=2)
    zpad = jnp.zeros_like(kmat[0, :, :, :(q - 1) * C_GROUP])
    kflat = jnp.stack([jnp.concatenate([zpad, kmat[0]], axis=-1), jnp.concatenate([kmat[1], zpad], axis=-1)])
    m_intra = jnp.stack([kflat[..., (q - 1 - s) * C_GROUP:(q - 1 - s) * C_GROUP + S5_XW] for s in range(q)],
                        axis=2).reshape(2, C_GROUPS, S5_XW, S5_XW).astype(BF16)

    ps_r = jnp.stack([pw_r[0, :, q - 1::-1], pw_r[1, :, :q]])
    ps_i = jnp.stack([pw_i[0, :, q - 1::-1], pw_i[1, :, :q]])
    bt_r, bt_i = bb_r.transpose(0, 1, 3, 2), bb_i.transpose(0, 1, 3, 2)
    p4_r = jnp.concatenate([ps_r] * 4, axis=-1)[:, :, :, None]
    p4_i = jnp.concatenate([ps_i] * 4, axis=-1)[:, :, :, None]
    b4_a = jnp.concatenate([bt_r, bt_i, bt_i, bt_r], axis=-1)[:, :, None]
    b4_b = jnp.concatenate([-bt_i, bt_r, bt_r, -bt_i], axis=-1)[:, :, None]
    m_state = (p4_r * b4_a + p4_i * b4_b).reshape(2, C_GROUPS, S5_XW, 4 * C_STATE).astype(BF16)

    ar, ai = pw_r[:, :, q], pw_i[:, :, q]
    aq3 = jnp.stack([jnp.concatenate([ar, ar], -1), jnp.concatenate([-ai, ai], -1),
                     jnp.concatenate([ai, -ai], -1)], axis=1)
    return m_intra, m_state, m_out, aq3


TM_GLU = 256


def _glu_kernel(u_ref, yl_ref, yc_ref, dsk_ref, w_ref, h_ref, mod_ref, o_ref, *, n_lat):
    y = jnp.where(pl.program_id(0) >= n_lat, yc_ref[...], yl_ref[...])
    z = jax.nn.gelu(dsk_ref[...] * u_ref[...] + y).astype(BF16)
    gate = _mod_row(mod_ref, 5)
    for j in range(D // TN):
        cs = slice(j * TN, (j + 1) * TN)
        a = _dot(z, w_ref[:, cs])
        g = _dot(z, w_ref[:, D + j * TN:D + (j + 1) * TN])
        o_ref[:, cs] = h_ref[:, cs] + gate[:, cs] * (a * jax.nn.sigmoid(g))


def _glu(u, y_lat, y_ctx, d_skip, w_glu, h, mod, rows):
    n_lat = N_LAT // TM_GLU
    tile = pl.BlockSpec((TM_GLU, D), lambda i: (i, 0))
    return pl.pallas_call(
        functools.partial(_glu_kernel, n_lat=n_lat),
        grid=(rows // TM_GLU,),
        in_specs=[tile] + _split_specs(n_lat, TM_GLU) + [
            pl.BlockSpec((1, D), lambda i: (0, 0)),
            _resident((D, 2 * D)),
            tile,
            pl.BlockSpec((1, N_MOD, D), lambda i: (_seg(i, TM_GLU), 0, 0)),
        ],
        out_specs=tile,
        out_shape=jax.ShapeDtypeStruct((rows, D), F32),
        compiler_params=_params("arbitrary"),
        name="s5_glu",
    )(u, y_lat, y_ctx, d_skip, w_glu, h, mod)


def kernel(x, c, ctx, c_ctx, w_ada, b_ada, norm_g, ffn_w_gu, ffn_w_down, a_w_in, a_v_gain, a_w_s, a_b_s, a_w_out, b_w_qkv, b_q_gain, b_k_gain, b_rpb, b_w_out, c_w_in, c_a_re, c_a_im, c_log_dt, c_b_re, c_b_im, c_c_re, c_c_im, c_d, c_w_glu):
    h = jnp.concatenate([x.reshape(N_LAT, D), ctx.reshape(N_CTX, D)], axis=0).astype(F32)
    cond8 = jnp.concatenate([c, c_ctx[None], jnp.zeros((8 - BATCH - 1, D), c.dtype)], axis=0).astype(F32)
    mods = _adaln(cond8, w_ada.astype(F32), b_ada.astype(F32))
    norm_g = norm_g.astype(F32)
    ffn_w_gu = ffn_w_gu.astype(F32)
    ffn_w_down = ffn_w_down.astype(F32)

    for i in range(DEPTH):
        kind, j = i % 3, i // 3
        last = i == DEPTH - 1
        rows = N_LAT if last else N_ALL
        mod, gains = mods[i], norm_g[i]

        h = _ffn(h, mod, gains, ffn_w_gu, ffn_w_down, i, 0, rows)

        if kind == 0:
            z = _gmlp(h, mod, gains, a_w_in[j].astype(BF16), a_v_gain[j].astype(F32)[None],
                      a_w_s[j].astype(BF16), a_b_s[j].astype(F32).T, rows)
            h = _mm_res(z, None, a_w_out[j].astype(BF16), h, mod, rows)
        elif kind == 1:
            head_gains = jnp.stack([b_q_gain[j], b_k_gain[j]]).astype(F32)
            qkv = _qkv(h, mod, gains, b_w_qkv[j].astype(BF16), head_gains, rows)
            o_lat, o_ctx = _attention(qkv, _attn_bias_table(b_rpb[j].astype(F32)))
            h = _mm_res(o_lat, o_ctx, b_w_out[j].astype(BF16), h, mod, rows)
        else:
            u = _proj(h, mod, gains, c_w_in[j].astype(BF16), rows)
            y_lat, y_ctx = _s5_scan(u, *_s5_weights(c_a_re[j], c_a_im[j], c_log_dt[j], c_b_re[j], c_b_im[j],
                                                    c_c_re[j], c_c_im[j]))
            h = _glu(u, y_lat, y_ctx, c_d[j].astype(F32)[None], c_w_glu[j].astype(BF16), h, mod, rows)

        h = _ffn(h, mod, gains, ffn_w_gu, ffn_w_down, i, 1, rows)

    return h[:N_LAT].reshape(BATCH, SEQ, D).astype(x.dtype)
```

```python
import functools

import numpy as np
import jax
import jax.numpy as jnp
from jax import lax
from jax.experimental import pallas as pl
from jax.experimental.pallas import tpu as pltpu

F32 = jnp.float32
BF16 = jnp.bfloat16
HIGHEST = lax.Precision.HIGHEST

D = 2048
BATCH = 2
SEQ = 4096
CTX = 256
DEPTH = 4
N_LAT = BATCH * SEQ
N_CTX = BATCH * CTX
N_ALL = N_LAT + N_CTX
N_MOD = 9
D_FF = 5632
RMS_EPS = 1e-6
NEG_INF = -1e30
GRID_W = 64
ROWS = SEQ // GRID_W
CHUNK = 128
A_GROUPS = 16
N_HEADS = 16
HEAD_DIM = 128
WIN_H = 8
WIN_W = 16
C_GROUP = 16
C_GROUPS = D // C_GROUP
C_STATE = 64
LANES = 128
NORM_ROWS = 16
NORM_UNROLL = 4

TM = 512
TM_F = 1024
TF = 512
FF_SPLIT = 2
TN = 512
ADA_TN = 1024
S5_Q = 16
S5_GB = LANES // C_GROUP
S5_XW = S5_Q * C_GROUP
VMEM_LIMIT = 56 * 1024 * 1024


def _params(*sem):
    return pltpu.CompilerParams(dimension_semantics=sem, vmem_limit_bytes=VMEM_LIMIT)


def _seg(i, tm=TM):
    return jnp.minimum((i * tm) // SEQ, 2)


def _mod_row(mod_ref, r):
    return mod_ref[0, r:r + 1, :]


def _norm_mod_store(h_ref, xn_ref, m, g_ref, mod_ref, s):
    gs = g_ref[s:s + 1, :] * (1.0 + _mod_row(mod_ref, 3 * s + 1))
    shift = _mod_row(mod_ref, 3 * s)

    def chunk(r, carry):
        rows = pl.ds(pl.multiple_of(r * NORM_ROWS, NORM_ROWS), NORM_ROWS)
        x = h_ref[rows, :]
        ms = jnp.mean(x * x, axis=-1, keepdims=True)
        xn_ref[rows, :] = (x * lax.rsqrt(ms + RMS_EPS) * gs + shift).astype(BF16)
        return carry

    lax.fori_loop(0, m // NORM_ROWS, chunk, 0, unroll=NORM_UNROLL)


def _dot(a, b):
    return jnp.dot(a, b, preferred_element_type=F32)


def _dot_nt(a, b):
    return lax.dot_general(a, b, (((1,), (1,)), ((), ())), preferred_element_type=F32)


def _ada_kernel(c_ref, w_ref, b_ref, o_ref):
    c = c_ref[...]
    a = (c * jax.nn.sigmoid(c)).astype(BF16)
    o_ref[0] = _dot(a, w_ref[0].astype(BF16)) + b_ref[0]


def _adaln(cond8, w_ada, b_ada):
    n = N_MOD * D
    out = pl.pallas_call(
        _ada_kernel,
        grid=(DEPTH, n // ADA_TN),
        in_specs=[
            pl.BlockSpec((8, D), lambda l, j: (0, 0)),
            pl.BlockSpec((1, D, ADA_TN), lambda l, j: (l, 0, j)),
            pl.BlockSpec((1, 1, ADA_TN), lambda l, j: (l, 0, j)),
        ],
        out_specs=pl.BlockSpec((1, 8, ADA_TN), lambda l, j: (l, 0, j)),
        out_shape=jax.ShapeDtypeStruct((DEPTH, 8, n), F32),
        compiler_params=_params("parallel", "parallel"),
        name="adaln",
    )(cond8, w_ada, b_ada.reshape(DEPTH, 1, n))
    return out[:, :3].reshape(DEPTH, 3, N_MOD, D)


def _prenorm_kernel(h_ref, mod_ref, g_ref, o_ref, *, s):
    _norm_mod_store(h_ref, o_ref, TM, g_ref, mod_ref, s)


def _prenorm(h, mod, gains, s, rows):
    return pl.pallas_call(
        functools.partial(_prenorm_kernel, s=s),
        grid=(rows // TM,),
        in_specs=[
            pl.BlockSpec((TM, D), lambda i: (i, 0)),
            pl.BlockSpec((1, N_MOD, D), lambda i: (_seg(i), 0, 0)),
            pl.BlockSpec((3, D), lambda i: (0, 0)),
        ],
        out_specs=pl.BlockSpec((TM, D), lambda i: (i, 0)),
        out_shape=jax.ShapeDtypeStruct((rows, D), BF16),
        compiler_params=_params("parallel"),
        name="ffn_prenorm",
    )(h, mod, gains)


def _ffn_up_kernel(xn_ref, wg_ref, wu_ref, o_ref, *, subtiles):
    wg = wg_ref[0, 0].astype(BF16)
    wu = wu_ref[0, 0].astype(BF16)
    for r0, m in subtiles:
        xn = xn_ref[r0:r0 + m, :]
        g = _dot(xn, wg)
        u = _dot(xn, wu)
        o_ref[r0:r0 + m, :] = ((g * jax.nn.sigmoid(g)) * u).astype(BF16)


def _ffn_up(xn, w_gu, layer, half, rows):
    nk = D_FF // TF
    hr = rows // FF_SPLIT
    subtiles = [(r0, min(TM_F, hr - r0)) for r0 in range(0, hr, TM_F)]
    return pl.pallas_call(
        functools.partial(_ffn_up_kernel, subtiles=subtiles),
        grid=(FF_SPLIT, nk),
        in_specs=[
            pl.BlockSpec((hr, D), lambda r, k: (r, 0), pipeline_mode=pl.Buffered(1)),
            pl.BlockSpec((1, 1, D, TF), lambda r, k: (layer, half, 0, k)),
            pl.BlockSpec((1, 1, D, TF), lambda r, k: (layer, half, 0, nk + k)),
        ],
        out_specs=pl.BlockSpec((hr, TF), lambda r, k: (r, k)),
        out_shape=jax.ShapeDtypeStruct((rows, D_FF), BF16),
        compiler_params=_params("arbitrary", "arbitrary"),
        name="ffn_up",
    )(xn, w_gu, w_gu)


def _ffn_down_kernel(a_ref, wd_ref, h_ref, mod_ref, o_ref, *, s):
    a = a_ref[...]
    gate = 0.5 * _mod_row(mod_ref, 3 * s + 2)
    for j in range(D // TN):
        cs = slice(j * TN, (j + 1) * TN)
        o_ref[:, cs] = h_ref[:, cs] + gate[:, cs] * _dot(a, wd_ref[0, 0, :, cs])


def _ffn_down(act, w_down, h, mod, layer, half, s, rows):
    tile = pl.BlockSpec((TM, D), lambda i: (i, 0))
    return pl.pallas_call(
        functools.partial(_ffn_down_kernel, s=s),
        grid=(rows // TM,),
        in_specs=[
            pl.BlockSpec((TM, D_FF), lambda i: (i, 0)),
            pl.BlockSpec((1, 1, D_FF, D), lambda i: (layer, half, 0, 0), pipeline_mode=pl.Buffered(1)),
            tile,
            pl.BlockSpec((1, N_MOD, D), lambda i: (_seg(i), 0, 0)),
        ],
        out_specs=tile,
        out_shape=jax.ShapeDtypeStruct((rows, D), F32),
        compiler_params=_params("arbitrary"),
        name="ffn_down",
    )(act, w_down, h, mod)


def _ffn(h, mod, gains, w_gu, w_down_bf16, layer, half, rows):
    s = 2 * half
    xn = _prenorm(h, mod, gains, s, rows)
    act = _ffn_up(xn, w_gu, layer, half, rows)
    return _ffn_down(act, w_down_bf16, h, mod, layer, half, s, rows)


def _mm_res_kernel(*refs, n_lat):
    if n_lat is None:
        z_ref, w_ref, h_ref, mod_ref, o_ref = refs
        o_ref[...] = h_ref[...] + _mod_row(mod_ref, 5) * _dot(z_ref[...], w_ref[...])
        return
    zl_ref, zc_ref, w_ref, h_ref, mod_ref, o_ref = refs
    i = pl.program_id(0)

    @pl.when(i < n_lat)
    def _():
        o_ref[...] = h_ref[...] + _mod_row(mod_ref, 5) * _dot(zl_ref[...], w_ref[...])

    @pl.when(i >= n_lat)
    def _():
        o_ref[...] = h_ref[...] + _mod_row(mod_ref, 5) * _dot(zc_ref[...], w_ref[...])


def _split_specs(n_lat, tm=TM):
    return [pl.BlockSpec((tm, D), lambda i, *_: (jnp.minimum(i, n_lat - 1), 0)),
            pl.BlockSpec((tm, D), lambda i, *_: (jnp.maximum(i - n_lat, 0), 0))]


def _mm_res(z, z_ctx, w, h, mod, rows):
    n_lat = None if z_ctx is None else N_LAT // TM
    z_specs = [pl.BlockSpec((TM, D), lambda i: (i, 0))] if z_ctx is None else _split_specs(n_lat)
    zs = (z,) if z_ctx is None else (z, z_ctx)
    return pl.pallas_call(
        functools.partial(_mm_res_kernel, n_lat=n_lat),
        grid=(rows // TM,),
        in_specs=z_specs + [
            _resident((D, D)),
            pl.BlockSpec((TM, D), lambda i: (i, 0)),
            pl.BlockSpec((1, N_MOD, D), lambda i: (_seg(i), 0, 0)),
        ],
        out_specs=pl.BlockSpec((TM, D), lambda i: (i, 0)),
        out_shape=jax.ShapeDtypeStruct((rows, D), F32),
        compiler_params=_params("parallel"),
        name="mm_res",
    )(*zs, w, h, mod)


def _gmlp_kernel(h_ref, mod_ref, g_ref, win_ref, vg_ref, ws_ref, bst_ref, z_ref, xn_ref, y_ref, *, nj):
    _norm_mod_store(h_ref, xn_ref, TM, g_ref, mod_ref, 1)
    xn = xn_ref[...]
    for j in range(nj):
        y_ref[j] = jax.nn.gelu(_dot(xn, win_ref[:, j * TN:(j + 1) * TN]))

    half = nj // 2
    ssq = jnp.zeros((TM, 1), F32)
    for jj in range(half, nj):
        yv = y_ref[jj]
        ssq = ssq + jnp.sum(yv * yv, axis=-1, keepdims=True)
    inv = lax.rsqrt(ssq / D + RMS_EPS)
    gpb = TN // CHUNK
    for cb in range(half):
        v = (y_ref[half + cb] * inv * vg_ref[:, cb * TN:(cb + 1) * TN]).astype(BF16)
        u = y_ref[cb]
        for gg in range(gpb):
            grp = cb * gpb + gg
            cs = slice(gg * CHUNK, (gg + 1) * CHUNK)
            for c in range(TM // CHUNK):
                rs = slice(c * CHUNK, (c + 1) * CHUNK)
                sg = _dot(ws_ref[grp], v[rs, cs]) + bst_ref[:, grp:grp + 1]
                z_ref[rs, grp * CHUNK:(grp + 1) * CHUNK] = (u[rs, cs] * sg).astype(BF16)


def _resident(shape):
    return pl.BlockSpec(shape, lambda *_: (0,) * len(shape), pipeline_mode=pl.Buffered(1))


def _gmlp(h, mod, gains, w_in, v_gain, w_s, b_s_t, rows):
    nj = (2 * D) // TN
    return pl.pallas_call(
        functools.partial(_gmlp_kernel, nj=nj),
        grid=(rows // TM,),
        in_specs=[
            pl.BlockSpec((TM, D), lambda i: (i, 0)),
            pl.BlockSpec((1, N_MOD, D), lambda i: (_seg(i), 0, 0)),
            pl.BlockSpec((3, D), lambda i: (0, 0)),
            _resident((D, 2 * D)),
            pl.BlockSpec((1, D), lambda i: (0, 0)),
            _resident((A_GROUPS, CHUNK, CHUNK)),
            pl.BlockSpec((CHUNK, A_GROUPS), lambda i: (0, 0)),
        ],
        out_specs=pl.BlockSpec((TM, D), lambda i: (i, 0)),
        out_shape=jax.ShapeDtypeStruct((rows, D), BF16),
        scratch_shapes=[pltpu.VMEM((TM, D), BF16), pltpu.VMEM((nj, TM, TN), F32)],
        compiler_params=_params("arbitrary"),
        name="gmlp",
    )(h, mod, gains, w_in, v_gain, w_s, b_s_t)


def _qkv_kernel(h_ref, mod_ref, g_ref, w_ref, hg_ref, o_ref, xn_ref):
    _norm_mod_store(h_ref, xn_ref, TM, g_ref, mod_ref, 1)
    xn = xn_ref[...]
    for j in range((3 * D) // TN):
        y = _dot(xn, w_ref[:, j * TN:(j + 1) * TN])
        part = j // (D // TN)
        if part == 2:
            o_ref[:, j * TN:(j + 1) * TN] = y.astype(BF16)
            continue
        gain = hg_ref[part:part + 1, :]
        for hh in range(TN // HEAD_DIM):
            yh = y[:, hh * HEAD_DIM:(hh + 1) * HEAD_DIM]
            ms = jnp.mean(yh * yh, axis=-1, keepdims=True)
            c0 = j * TN + hh * HEAD_DIM
            o_ref[:, c0:c0 + HEAD_DIM] = (yh * lax.rsqrt(ms + RMS_EPS) * gain).astype(BF16)


def _qkv(h, mod, gains, w_qkv, head_gains, rows):
    return pl.pallas_call(
        _qkv_kernel,
        grid=(rows // TM,),
        in_specs=[
            pl.BlockSpec((TM, D), lambda i: (i, 0)),
            pl.BlockSpec((1, N_MOD, D), lambda i: (_seg(i), 0, 0)),
            pl.BlockSpec((3, D), lambda i: (0, 0)),
            _resident((D, 3 * D)),
            pl.BlockSpec((2, HEAD_DIM), lambda i: (0, 0)),
        ],
        out_specs=pl.BlockSpec((TM, 3 * D), lambda i: (i, 0)),
        out_shape=jax.ShapeDtypeStruct((rows, 3 * D), BF16),
        scratch_shapes=[pltpu.VMEM((TM, D), BF16)],
        compiler_params=_params("arbitrary"),
        name="qkv",
    )(h, mod, gains, w_qkv, head_gains)


ATT_QR = 8
ATT_KR = 16
ATT_QB = ATT_QR * GRID_W
ATT_KB = ATT_KR * GRID_W
N_DR = 2 * WIN_H - 1
ATT_PATTERNS = ((0, 0), (ATT_QR, ATT_QR - WIN_H // 2), (ROWS - ATT_QR, ROWS - ATT_KR))


def _attn_kernel(q_ref, k_ref, v_ref, qc_ref, kc_ref, vc_ref, t_ref, o_ref, oc_ref, bias_ref):
    scale = HEAD_DIM ** -0.5
    left = lax.broadcasted_iota(jnp.int32, (GRID_W, 2 * GRID_W), 1) < GRID_W
    neg = jnp.full((GRID_W, 2 * GRID_W), NEG_INF, F32)

    for p, (r0, kr_base) in enumerate(ATT_PATTERNS):
        for qr in range(ATT_QR):
            r = r0 + qr
            rstart = min(max(r - WIN_H // 2, 0), ROWS - WIN_H)
            for kp in range(ATT_KR // 2):
                halves = []
                for kr in (kr_base + 2 * kp, kr_base + 2 * kp + 1):
                    inside = rstart <= kr < rstart + WIN_H
                    halves.append(t_ref[0, kr - r + WIN_H - 1] if inside else None)
                a, b = halves
                if a is None and b is None:
                    blk = neg
                else:
                    blk = jnp.where(left, neg if a is None else a, neg if b is None else b)
                bias_ref[p, qr * GRID_W:(qr + 1) * GRID_W, kp * 2 * GRID_W:(kp + 1) * 2 * GRID_W] = blk

    kc = kc_ref[...]
    vc = vc_ref[...]

    def block(q0, k0, p):
        q = q_ref[pl.ds(q0, ATT_QB), :]
        k = k_ref[pl.ds(k0, ATT_KB), :]
        v = v_ref[pl.ds(k0, ATT_KB), :]
        s_win = _dot_nt(q, k) * scale + bias_ref[p]
        s_ctx = _dot_nt(q, kc) * scale
        m = jnp.maximum(jnp.max(s_win, axis=-1, keepdims=True), jnp.max(s_ctx, axis=-1, keepdims=True))
        p_win = jnp.exp(s_win - m)
        p_ctx = jnp.exp(s_ctx - m)
        denom = jnp.sum(p_win, axis=-1, keepdims=True) + jnp.sum(p_ctx, axis=-1, keepdims=True)
        o = _dot(p_win.astype(BF16), v) + _dot(p_ctx.astype(BF16), vc)
        o_ref[pl.ds(q0, ATT_QB), :] = (o / denom).astype(BF16)

    block(0, 0, 0)
    for rb in range(1, ROWS // ATT_QR - 1):
        block(rb * ATT_QB, rb * ATT_QB - (WIN_H // 2) * GRID_W, 1)
    block(SEQ - ATT_QB, SEQ - ATT_KB, 2)

    s = _dot_nt(qc_ref[...], kc) * scale
    pc = jnp.exp(s - jnp.max(s, axis=-1, keepdims=True))
    oc = _dot(pc.astype(BF16), vc) / jnp.sum(pc, axis=-1, keepdims=True)
    oc_ref[...] = oc.astype(BF16)


def _attention(qkv, bias_tab):
    lat_blk = (SEQ, HEAD_DIM)
    ctx_blk = (CTX, HEAD_DIM)
    ctx0 = N_LAT // CTX
    return pl.pallas_call(
        _attn_kernel,
        grid=(N_HEADS, BATCH),
        in_specs=[
            pl.BlockSpec(lat_blk, lambda h, b: (b, h)),
            pl.BlockSpec(lat_blk, lambda h, b: (b, N_HEADS + h)),
            pl.BlockSpec(lat_blk, lambda h, b: (b, 2 * N_HEADS + h)),
            pl.BlockSpec(ctx_blk, lambda h, b: (ctx0 + b, h)),
            pl.BlockSpec(ctx_blk, lambda h, b: (ctx0 + b, N_HEADS + h)),
            pl.BlockSpec(ctx_blk, lambda h, b: (ctx0 + b, 2 * N_HEADS + h)),
            pl.BlockSpec((1, N_DR, GRID_W, 2 * GRID_W), lambda h, b: (h, 0, 0, 0)),
        ],
        out_specs=[
            pl.BlockSpec(lat_blk, lambda h, b: (b, h)),
            pl.BlockSpec(ctx_blk, lambda h, b: (b, h)),
        ],
        out_shape=[
            jax.ShapeDtypeStruct((N_LAT, D), BF16),
            jax.ShapeDtypeStruct((N_CTX, D), BF16),
        ],
        scratch_shapes=[pltpu.VMEM((len(ATT_PATTERNS), ATT_QB, ATT_KB), F32)],
        compiler_params=_params("parallel", "parallel"),
        name="nat_attention",
    )(qkv, qkv, qkv, qkv, qkv, qkv, bias_tab)


def _attn_bias_table(rpb):
    qcol = np.arange(GRID_W)[:, None]
    kcol = np.arange(2 * GRID_W)[None, :] % GRID_W
    cstart = np.clip(qcol - WIN_W // 2, 0, GRID_W - WIN_W)
    col_valid = (kcol >= cstart) & (kcol < cstart + WIN_W)
    dc_idx = np.clip(kcol - qcol, 1 - WIN_W, WIN_W - 1) + (WIN_W - 1)
    onehot = (dc_idx[None] == np.arange(2 * WIN_W - 1)[:, None, None]) & col_valid[None]
    onehot = jnp.asarray(onehot.reshape(2 * WIN_W - 1, -1), F32)
    mask = jnp.asarray(np.where(col_valid, 0.0, NEG_INF).reshape(-1), F32)
    t = jnp.dot(rpb.reshape(N_HEADS * N_DR, 2 * WIN_W - 1), onehot, precision=HIGHEST) + mask
    return t.reshape(N_HEADS, N_DR, GRID_W, 2 * GRID_W)


def _proj_kernel(h_ref, mod_ref, g_ref, w_ref, o_ref, xn_ref):
    _norm_mod_store(h_ref, xn_ref, TM, g_ref, mod_ref, 1)
    xn = xn_ref[...]
    for j in range(D // TN):
        o_ref[:, j * TN:(j + 1) * TN] = _dot(xn, w_ref[:, j * TN:(j + 1) * TN])


def _proj(h, mod, gains, w, rows):
    return pl.pallas_call(
        _proj_kernel,
        grid=(rows // TM,),
        in_specs=[
            pl.BlockSpec((TM, D), lambda i: (i, 0)),
            pl.BlockSpec((1, N_MOD, D), lambda i: (_seg(i), 0, 0)),
            pl.BlockSpec((3, D), lambda i: (0, 0)),
            _resident((D, D)),
        ],
        out_specs=pl.BlockSpec((TM, D), lambda i: (i, 0)),
        out_shape=jax.ShapeDtypeStruct((rows, D), F32),
        scratch_shapes=[pltpu.VMEM((TM, D), BF16)],
        compiler_params=_params("arbitrary"),
        name="s5_in_proj",
    )(h, mod, gains, w)


S5_CL = SEQ // S5_Q
S5_CC = CTX // S5_Q
S5_HALF = LANES // C_GROUP


def _s5_kernel(ul_ref, uc_ref, mi_ref, ms_ref, mo_ref, aq_ref, yl_ref, yc_ref,
               xl_scr, xc_scr, s_scr, ssw_scr, hp_scr, yl_acc, yc_acc):
    d = pl.program_id(2)
    ns = 2 * C_STATE

    def granule_transpose(rows):
        n = rows[0].shape[0]
        granule = lax.broadcasted_iota(jnp.int32, (n, LANES), 1) // C_GROUP
        rows = list(rows)
        k = S5_GB // 2
        while k:
            hi = (granule & k) != 0
            for i in range(S5_GB):
                if i & k:
                    continue
                a, b = rows[i], rows[i + k]
                rows[i] = jnp.where(hi, pltpu.roll(b, k * C_GROUP, 1), a)
                rows[i + k] = jnp.where(hi, b, pltpu.roll(a, LANES - k * C_GROUP, 1))
            k //= 2
        return rows

    def to_chunks(u_ref, x_scr, nch):
        for hf in range(S5_Q // S5_HALF):
            by_token = [u_ref[pl.ds(hf * S5_HALF + tl, nch, stride=S5_Q), :] for tl in range(S5_HALF)]
            for gl, x in enumerate(granule_transpose(by_token)):
                x_scr[gl, :, hf * LANES:(hf + 1) * LANES] = x.astype(BF16)

    def from_chunks(y_acc, y_ref, nch):
        for hf in range(S5_Q // S5_HALF):
            by_group = [y_acc[gl, :, hf * LANES:(hf + 1) * LANES] for gl in range(S5_GB)]
            for tl, y in enumerate(granule_transpose(by_group)):
                y_ref[pl.ds(hf * S5_HALF + tl, nch, stride=S5_Q), :] = y

    @pl.when(d == 0)
    def _():
        to_chunks(ul_ref, xl_scr, S5_CL)
        to_chunks(uc_ref, xc_scr, S5_CC)
        yl_acc[...] = jnp.zeros_like(yl_acc)
        yc_acc[...] = jnp.zeros_like(yc_acc)

    lat0 = S5_CC * S5_GB
    for j in range(S5_GB):
        xl, xc = xl_scr[j], xc_scr[j]
        yl_acc[j] += _dot(xl, mi_ref[0, j])
        yc_acc[j] += _dot(xc, mi_ref[0, j])
        sl = _dot(xl, ms_ref[0, j])
        sc = _dot(xc, ms_ref[0, j])
        s_scr[pl.ds(j, S5_CC, stride=S5_GB), :] = sc[:, :ns]
        s_scr[pl.ds(lat0 + j, S5_CL, stride=S5_GB), :] = sl[:, :ns]
        ssw_scr[pl.ds(j, S5_CC, stride=S5_GB), :] = sc[:, ns:]
        ssw_scr[pl.ds(lat0 + j, S5_CL, stride=S5_GB), :] = sl[:, ns:]

    a1 = aq_ref[0, 0]
    a2 = aq_ref[0, 1]
    a3 = aq_ref[0, 2]

    def step(pos, carry):
        hs, hsw = carry
        row = pl.multiple_of(pos * S5_GB, S5_GB)
        hp_scr[pl.ds(row, S5_GB), :] = hs
        s = s_scr[pl.ds(row, S5_GB), :]
        ssw = ssw_scr[pl.ds(row, S5_GB), :]
        return hs * a1 + hsw * a2 + s, hsw * a1 + hs * a3 + ssw

    def ctx_step(i, carry):
        return step(jnp.where(d == 0, i, S5_CC - 1 - i), carry)

    def lat_step(i, carry):
        return step(S5_CC + jnp.where(d == 0, i, S5_CL - 1 - i), carry)

    zero = jnp.zeros((S5_GB, ns), F32)
    carry = lax.fori_loop(0, S5_CC, ctx_step, (zero, zero))
    lax.fori_loop(0, S5_CL, lat_step, carry, unroll=4)

    for j in range(S5_GB):
        mo = mo_ref[0, j]
        hc = hp_scr[pl.ds(j, S5_CC, stride=S5_GB), :]
        hl = hp_scr[pl.ds(lat0 + j, S5_CL, stride=S5_GB), :]
        yc_acc[j] += _dot(hc.astype(BF16), mo)
        yl_acc[j] += _dot(hl.astype(BF16), mo)

    @pl.when(d == 1)
    def _():
        from_chunks(yl_acc, yl_ref, S5_CL)
        from_chunks(yc_acc, yc_ref, S5_CC)


def _s5_scan(u, mi, ms, mo, aq):
    ns = 2 * C_STATE
    nrow = (S5_CC + S5_CL) * S5_GB
    ctx0 = N_LAT // CTX
    return pl.pallas_call(
        _s5_kernel,
        grid=(BATCH, C_GROUPS // S5_GB, 2),
        in_specs=[
            pl.BlockSpec((SEQ, LANES), lambda b, g, d: (b, g)),
            pl.BlockSpec((CTX, LANES), lambda b, g, d: (ctx0 + b, g)),
            pl.BlockSpec((1, S5_GB, S5_XW, S5_XW), lambda b, g, d: (d, g, 0, 0)),
            pl.BlockSpec((1, S5_GB, S5_XW, 2 * ns), lambda b, g, d: (d, g, 0, 0)),
            pl.BlockSpec((1, S5_GB, ns, S5_XW), lambda b, g, d: (d, g, 0, 0)),
            pl.BlockSpec((1, 3, S5_GB, ns), lambda b, g, d: (d, 0, g, 0)),
        ],
        out_specs=[
            pl.BlockSpec((SEQ, LANES), lambda b, g, d: (b, g)),
            pl.BlockSpec((CTX, LANES), lambda b, g, d: (b, g)),
        ],
        out_shape=[
            jax.ShapeDtypeStruct((N_LAT, D), F32),
            jax.ShapeDtypeStruct((N_CTX, D), F32),
        ],
        scratch_shapes=[
            pltpu.VMEM((S5_GB, S5_CL, S5_XW), BF16),
            pltpu.VMEM((S5_GB, S5_CC, S5_XW), BF16),
            pltpu.VMEM((nrow, ns), F32),
            pltpu.VMEM((nrow, ns), F32),
            pltpu.VMEM((nrow, ns), F32),
            pltpu.VMEM((S5_GB, S5_CL, S5_XW), F32),
            pltpu.VMEM((S5_GB, S5_CC, S5_XW), F32),
        ],
        compiler_params=_params("arbitrary", "arbitrary", "arbitrary"),
        name="s5_scan",
    )(u, u, mi, ms, mo, aq)


def _s5_weights(a_re, a_im, log_dt, b_re, b_im, c_re, c_im):
    q = S5_Q
    a_re, a_im, b_re, b_im, c_re, c_im = (v.astype(F32) for v in (a_re, a_im, b_re, b_im, c_re, c_im))
    dt = jnp.exp(log_dt.astype(F32))[..., None]
    zr, zi = a_re * dt, a_im * dt
    ab_r, ab_i = jnp.exp(zr) * jnp.cos(zi), jnp.exp(zr) * jnp.sin(zi)
    den = a_re * a_re + a_im * a_im
    f_r = ((ab_r - 1.0) * a_re + ab_i * a_im) / den
    f_i = (ab_i * a_re - (ab_r - 1.0) * a_im) / den
    bb_r = f_r[..., None] * b_re - f_i[..., None] * b_im
    bb_i = f_r[..., None] * b_im + f_i[..., None] * b_re

    taus = jnp.arange(q + 1, dtype=F32)[:, None]
    mag = jnp.exp(zr[:, :, None] * taus)
    pw_r, pw_i = mag * jnp.cos(zi[:, :, None] * taus), mag * jnp.sin(zi[:, :, None] * taus)

    col = np.arange(S5_XW)
    e_t = jnp.asarray(col[None, :] // C_GROUP == np.arange(q)[:, None], F32)
    e_o = jnp.asarray(col[None, :] % C_GROUP == np.arange(C_GROUP)[:, None], F32)
    xc_r = jnp.einsum("dgop,ox->dgpx", c_re, e_o, precision=HIGHEST)
    xc_i = jnp.einsum("dgop,ox->dgpx", c_im, e_o, precision=HIGHEST)

    def readout(p_r, p_i):
        xp_r = jnp.einsum("dgtp,tx->dgpx", p_r, e_t, precision=HIGHEST)
        xp_i = jnp.einsum("dgtp,tx->dgpx", p_i, e_t, precision=HIGHEST)
        return xc_r * xp_r - xc_i * xp_i, xc_r * xp_i + xc_i * xp_r

    wo_r, wo_i = readout(jnp.stack([pw_r[0, :, 1:q + 1], pw_r[1, :, q:0:-1]]),
                         jnp.stack([pw_i[0, :, 1:q + 1], pw_i[1, :, q:0:-1]]))
    m_out = jnp.concatenate([wo_r, -wo_i], axis=2).astype(BF16)

    ck_r, ck_i = readout(jnp.stack([pw_r[0, :, :q], pw_r[1, :, q - 1::-1]]),
                         jnp.stack([pw_i[0, :, :q], pw_i[1, :, q - 1::-1]]))
    kmat = jnp.sum(bb_r[:, :, :, :, None] * ck_r[:, :, :, None, :]
                   - bb_i[:, :, :, :, None] * ck_i[:, :, :, None, :], axis=2)
    zpad = jnp.zeros_like(kmat[0, :, :, :(q - 1) * C_GROUP])
    kflat = jnp.stack([jnp.concatenate([zpad, kmat[0]], axis=-1), jnp.concatenate([kmat[1], zpad], axis=-1)])
    m_intra = jnp.stack([kflat[..., (q - 1 - s) * C_GROUP:(q - 1 - s) * C_GROUP + S5_XW] for s in range(q)],
                        axis=2).reshape(2, C_GROUPS, S5_XW, S5_XW).astype(BF16)

    ps_r = jnp.stack([pw_r[0, :, q - 1::-1], pw_r[1, :, :q]])
    ps_i = jnp.stack([pw_i[0, :, q - 1::-1], pw_i[1, :, :q]])
    bt_r, bt_i = bb_r.transpose(0, 1, 3, 2), bb_i.transpose(0, 1, 3, 2)
    p4_r = jnp.concatenate([ps_r] * 4, axis=-1)[:, :, :, None]
    p4_i = jnp.concatenate([ps_i] * 4, axis=-1)[:, :, :, None]
    b4_a = jnp.concatenate([bt_r, bt_i, bt_i, bt_r], axis=-1)[:, :, None]
    b4_b = jnp.concatenate([-bt_i, bt_r, bt_r, -bt_i], axis=-1)[:, :, None]
    m_state = (p4_r * b4_a + p4_i * b4_b).reshape(2, C_GROUPS, S5_XW, 4 * C_STATE).astype(BF16)

    ar, ai = pw_r[:, :, q], pw_i[:, :, q]
    aq3 = jnp.stack([jnp.concatenate([ar, ar], -1), jnp.concatenate([-ai, ai], -1),
                     jnp.concatenate([ai, -ai], -1)], axis=1)
    return m_intra, m_state, m_out, aq3


TM_GLU = 256


def _glu_kernel(u_ref, yl_ref, yc_ref, dsk_ref, w_ref, h_ref, mod_ref, o_ref, *, n_lat):
    y = jnp.where(pl.program_id(0) >= n_lat, yc_ref[...], yl_ref[...])
    z = jax.nn.gelu(dsk_ref[...] * u_ref[...] + y).astype(BF16)
    gate = _mod_row(mod_ref, 5)
    for j in range(D // TN):
        cs = slice(j * TN, (j + 1) * TN)
        a = _dot(z, w_ref[:, cs])
        g = _dot(z, w_ref[:, D + j * TN:D + (j + 1) * TN])
        o_ref[:, cs] = h_ref[:, cs] + gate[:, cs] * (a * jax.nn.sigmoid(g))


def _glu(u, y_lat, y_ctx, d_skip, w_glu, h, mod, rows):
    n_lat = N_LAT // TM_GLU
    tile = pl.BlockSpec((TM_GLU, D), lambda i: (i, 0))
    return pl.pallas_call(
        functools.partial(_glu_kernel, n_lat=n_lat),
        grid=(rows // TM_GLU,),
        in_specs=[tile] + _split_specs(n_lat, TM_GLU) + [
            pl.BlockSpec((1, D), lambda i: (0, 0)),
            _resident((D, 2 * D)),
            tile,
            pl.BlockSpec((1, N_MOD, D), lambda i: (_seg(i, TM_GLU), 0, 0)),
        ],
        out_specs=tile,
        out_shape=jax.ShapeDtypeStruct((rows, D), F32),
        compiler_params=_params("arbitrary"),
        name="s5_glu",
    )(u, y_lat, y_ctx, d_skip, w_glu, h, mod)


def kernel(x, c, ctx, c_ctx, w_ada, b_ada, norm_g, ffn_w_gu, ffn_w_down, a_w_in, a_v_gain, a_w_s, a_b_s, a_w_out, b_w_qkv, b_q_gain, b_k_gain, b_rpb, b_w_out, c_w_in, c_a_re, c_a_im, c_log_dt, c_b_re, c_b_im, c_c_re, c_c_im, c_d, c_w_glu):
    h = jnp.concatenate([x.reshape(N_LAT, D), ctx.reshape(N_CTX, D)], axis=0).astype(F32)
    cond8 = jnp.concatenate([c, c_ctx[None], jnp.zeros((8 - BATCH - 1, D), c.dtype)], axis=0).astype(F32)
    mods = _adaln(cond8, w_ada.astype(F32), b_ada.astype(F32))
    norm_g = norm_g.astype(F32)
    ffn_w_gu = ffn_w_gu.astype(F32)
    ffn_w_down = ffn_w_down.astype(BF16)

    for i in range(DEPTH):
        kind, j = i % 3, i // 3
        last = i == DEPTH - 1
        rows = N_LAT if last else N_ALL
        mod, gains = mods[i], norm_g[i]

        h = _ffn(h, mod, gains, ffn_w_gu, ffn_w_down, i, 0, rows)

        if kind == 0:
            z = _gmlp(h, mod, gains, a_w_in[j].astype(BF16), a_v_gain[j].astype(F32)[None],
                      a_w_s[j].astype(BF16), a_b_s[j].astype(F32).T, rows)
            h = _mm_res(z, None, a_w_out[j].astype(BF16), h, mod, rows)
        elif kind == 1:
            head_gains = jnp.stack([b_q_gain[j], b_k_gain[j]]).astype(F32)
            qkv = _qkv(h, mod, gains, b_w_qkv[j].astype(BF16), head_gains, rows)
            o_lat, o_ctx = _attention(qkv, _attn_bias_table(b_rpb[j].astype(F32)))
            h = _mm_res(o_lat, o_ctx, b_w_out[j].astype(BF16), h, mod, rows)
        else:
            u = _proj(h, mod, gains, c_w_in[j].astype(BF16), rows)
            y_lat, y_ctx = _s5_scan(u, *_s5_weights(c_a_re[j], c_a_im[j], c_log_dt[j], c_b_re[j], c_b_im[j],
                                                    c_c_re[j], c_c_im[j]))
            h = _glu(u, y_lat, y_ctx, c_d[j].astype(F32)[None], c_w_glu[j].astype(BF16), h, mod, rows)

        h = _ffn(h, mod, gains, ffn_w_gu, ffn_w_down, i, 1, rows)

    return h[:N_LAT].reshape(BATCH, SEQ, D).astype(x.dtype)
```

```python
import functools

import numpy as np
import jax
import jax.numpy as jnp
from jax import lax
from jax.experimental import pallas as pl
from jax.experimental.pallas import tpu as pltpu

F32 = jnp.float32
BF16 = jnp.bfloat16
HIGHEST = lax.Precision.HIGHEST

D = 2048
BATCH = 2
SEQ = 4096
CTX = 256
DEPTH = 4
N_LAT = BATCH * SEQ
N_CTX = BATCH * CTX
N_ALL = N_LAT + N_CTX
N_MOD = 9
D_FF = 5632
RMS_EPS = 1e-6
NEG_INF = -1e30
GRID_W = 64
ROWS = SEQ // GRID_W
CHUNK = 128
A_GROUPS = 16
N_HEADS = 16
HEAD_DIM = 128
WIN_H = 8
WIN_W = 16
C_GROUP = 16
C_GROUPS = D // C_GROUP
C_STATE = 64
LANES = 128
NORM_ROWS = 16
NORM_UNROLL = 4

TM = 512
TM_F = 1024
TF = 512
FF_SPLIT = 2
TM_E = 256
TN = 512
ADA_TN = 1024
S5_Q = 16
S5_GB = LANES // C_GROUP
S5_XW = S5_Q * C_GROUP
VMEM_LIMIT = 56 * 1024 * 1024


def _params(*sem):
    return pltpu.CompilerParams(dimension_semantics=sem, vmem_limit_bytes=VMEM_LIMIT)


def _seg(i, tm=TM):
    return jnp.minimum((i * tm) // SEQ, 2)


def _mod_row(mod_ref, r):
    return mod_ref[0, r:r + 1, :]


def _norm_mod_store(h_ref, xn_ref, m, g_ref, mod_ref, s):
    gs = g_ref[s:s + 1, :] * (1.0 + _mod_row(mod_ref, 3 * s + 1))
    shift = _mod_row(mod_ref, 3 * s)

    def chunk(r, carry):
        rows = pl.ds(pl.multiple_of(r * NORM_ROWS, NORM_ROWS), NORM_ROWS)
        x = h_ref[rows, :]
        ms = jnp.mean(x * x, axis=-1, keepdims=True)
        xn_ref[rows, :] = (x * lax.rsqrt(ms + RMS_EPS) * gs + shift).astype(BF16)
        return carry

    lax.fori_loop(0, m // NORM_ROWS, chunk, 0, unroll=NORM_UNROLL)


def _dot(a, b):
    return jnp.dot(a, b, preferred_element_type=F32)


def _dot_nt(a, b):
    return lax.dot_general(a, b, (((1,), (1,)), ((), ())), preferred_element_type=F32)


def _ada_kernel(c_ref, w_ref, b_ref, o_ref):
    c = c_ref[...]
    a = (c * jax.nn.sigmoid(c)).astype(BF16)
    o_ref[0] = _dot(a, w_ref[0].astype(BF16)) + b_ref[0]


def _adaln(cond8, w_ada, b_ada):
    n = N_MOD * D
    out = pl.pallas_call(
        _ada_kernel,
        grid=(DEPTH, n // ADA_TN),
        in_specs=[
            pl.BlockSpec((8, D), lambda l, j: (0, 0)),
            pl.BlockSpec((1, D, ADA_TN), lambda l, j: (l, 0, j)),
            pl.BlockSpec((1, 1, ADA_TN), lambda l, j: (l, 0, j)),
        ],
        out_specs=pl.BlockSpec((1, 8, ADA_TN), lambda l, j: (l, 0, j)),
        out_shape=jax.ShapeDtypeStruct((DEPTH, 8, n), F32),
        compiler_params=_params("parallel", "parallel"),
        name="adaln",
    )(cond8, w_ada, b_ada.reshape(DEPTH, 1, n))
    return out[:, :3].reshape(DEPTH, 3, N_MOD, D)


def _prenorm_kernel(h_ref, mod_ref, g_ref, o_ref, *, s):
    _norm_mod_store(h_ref, o_ref, TM, g_ref, mod_ref, s)


def _prenorm(h, mod, gains, s, rows):
    return pl.pallas_call(
        functools.partial(_prenorm_kernel, s=s),
        grid=(rows // TM,),
        in_specs=[
            pl.BlockSpec((TM, D), lambda i: (i, 0)),
            pl.BlockSpec((1, N_MOD, D), lambda i: (_seg(i), 0, 0)),
            pl.BlockSpec((3, D), lambda i: (0, 0)),
        ],
        out_specs=pl.BlockSpec((TM, D), lambda i: (i, 0)),
        out_shape=jax.ShapeDtypeStruct((rows, D), BF16),
        compiler_params=_params("parallel"),
        name="ffn_prenorm",
    )(h, mod, gains)


def _ffn_up_kernel(xn_ref, wg_ref, wu_ref, o_ref, *, subtiles):
    wg = wg_ref[0, 0].astype(BF16)
    wu = wu_ref[0, 0].astype(BF16)
    for r0, m in subtiles:
        xn = xn_ref[r0:r0 + m, :]
        g = _dot(xn, wg)
        u = _dot(xn, wu)
        o_ref[r0:r0 + m, :] = ((g * jax.nn.sigmoid(g)) * u).astype(BF16)


def _ffn_up(xn, w_gu, layer, half, rows):
    nk = D_FF // TF
    hr = rows // FF_SPLIT
    subtiles = [(r0, min(TM_F, hr - r0)) for r0 in range(0, hr, TM_F)]
    return pl.pallas_call(
        functools.partial(_ffn_up_kernel, subtiles=subtiles),
        grid=(FF_SPLIT, nk),
        in_specs=[
            pl.BlockSpec((hr, D), lambda r, k: (r, 0), pipeline_mode=pl.Buffered(1)),
            pl.BlockSpec((1, 1, D, TF), lambda r, k: (layer, half, 0, k)),
            pl.BlockSpec((1, 1, D, TF), lambda r, k: (layer, half, 0, nk + k)),
        ],
        out_specs=pl.BlockSpec((hr, TF), lambda r, k: (r, k)),
        out_shape=jax.ShapeDtypeStruct((rows, D_FF), BF16),
        compiler_params=_params("arbitrary", "arbitrary"),
        name="ffn_up",
    )(xn, w_gu, w_gu)


def _ffn_down_kernel(a_ref, wd_ref, h_ref, mod_ref, o_ref, *, s):
    a = a_ref[...]
    gate = 0.5 * _mod_row(mod_ref, 3 * s + 2)
    for j in range(D // TN):
        cs = slice(j * TN, (j + 1) * TN)
        o_ref[:, cs] = h_ref[:, cs] + gate[:, cs] * _dot(a, wd_ref[0, 0, :, cs])


def _ffn_down(act, w_down, h, mod, layer, half, s, rows):
    tile = pl.BlockSpec((TM, D), lambda i: (i, 0))
    return pl.pallas_call(
        functools.partial(_ffn_down_kernel, s=s),
        grid=(rows // TM,),
        in_specs=[
            pl.BlockSpec((TM, D_FF), lambda i: (i, 0)),
            pl.BlockSpec((1, 1, D_FF, D), lambda i: (layer, half, 0, 0), pipeline_mode=pl.Buffered(1)),
            tile,
            pl.BlockSpec((1, N_MOD, D), lambda i: (_seg(i), 0, 0)),
        ],
        out_specs=tile,
        out_shape=jax.ShapeDtypeStruct((rows, D), F32),
        compiler_params=_params("arbitrary"),
        name="ffn_down",
    )(act, w_down, h, mod)


def _emit_tile(y_buf, o_ref, xn_ref, gn_ref, modn_ref, sn, tm):
    gs = gn_ref[sn:sn + 1, :] * (1.0 + _mod_row(modn_ref, 3 * sn + 1))
    shift = _mod_row(modn_ref, 3 * sn)
    for r in range(tm // NORM_ROWS):
        rows = slice(r * NORM_ROWS, (r + 1) * NORM_ROWS)
        y = y_buf[rows, :]
        o_ref[rows, :] = y
        ms = jnp.mean(y * y, axis=-1, keepdims=True)
        xn_ref[rows, :] = (y * lax.rsqrt(ms + RMS_EPS) * gs + shift).astype(BF16)


def _lagged_steps(n, compute, emit, y_a, y_b):
    j = pl.program_id(0)
    last = y_a if (n - 1) % 2 == 0 else y_b

    @pl.when(j == 0)
    def _():
        compute(y_a)

    @pl.when((j > 0) & (j < n) & (j % 2 == 1))
    def _():
        emit(y_a)
        compute(y_b)

    @pl.when((j > 0) & (j < n) & (j % 2 == 0))
    def _():
        emit(y_b)
        compute(y_a)

    @pl.when(j == n)
    def _():
        emit(last)


def _lag_specs(n, tm):
    cur = lambda j: jnp.minimum(j, n - 1)
    prev = lambda j: jnp.maximum(j - 1, 0)
    return cur, prev


def _ffn_down_emit_kernel(a_ref, wd_ref, h_ref, mod_ref, gn_ref, modn_ref, o_ref, xn_ref, y_a, y_b,
                          *, s, sn, n):
    def compute(y_buf):
        a = a_ref[...]
        gate = 0.5 * _mod_row(mod_ref, 3 * s + 2)
        for c in range(D // TN):
            cs = slice(c * TN, (c + 1) * TN)
            y_buf[:, cs] = h_ref[:, cs] + gate[:, cs] * _dot(a, wd_ref[0, 0, :, cs])

    def emit(y_buf):
        _emit_tile(y_buf, o_ref, xn_ref, gn_ref, modn_ref, sn, TM_E)

    _lagged_steps(n, compute, emit, y_a, y_b)


def _ffn_down_emit(act, w_down, h, mod, layer, half, s, gains_n, mod_n, sn, rows):
    n = rows // TM_E
    cur, prev = _lag_specs(n, TM_E)
    return pl.pallas_call(
        functools.partial(_ffn_down_emit_kernel, s=s, sn=sn, n=n),
        grid=(n + 1,),
        in_specs=[
            pl.BlockSpec((TM_E, D_FF), lambda j: (cur(j), 0)),
            pl.BlockSpec((1, 1, D_FF, D), lambda j: (layer, half, 0, 0), pipeline_mode=pl.Buffered(1)),
            pl.BlockSpec((TM_E, D), lambda j: (cur(j), 0)),
            pl.BlockSpec((1, N_MOD, D), lambda j: (_seg(cur(j), TM_E), 0, 0)),
            pl.BlockSpec((3, D), lambda j: (0, 0)),
            pl.BlockSpec((1, N_MOD, D), lambda j: (_seg(prev(j), TM_E), 0, 0)),
        ],
        out_specs=[pl.BlockSpec((TM_E, D), lambda j: (prev(j), 0))] * 2,
        out_shape=[jax.ShapeDtypeStruct((rows, D), F32), jax.ShapeDtypeStruct((rows, D), BF16)],
        scratch_shapes=[pltpu.VMEM((TM_E, D), F32)] * 2,
        compiler_params=_params("arbitrary"),
        name="ffn_down_emit",
    )(act, w_down, h, mod, gains_n, mod_n)


def _mm_res_kernel(*refs, n, n_lat):
    if n_lat is None:
        z_ref, w_ref, h_ref, mod_ref, gn_ref, modn_ref, o_ref, xn_ref, y_a, y_b = refs
    else:
        zl_ref, zc_ref, w_ref, h_ref, mod_ref, gn_ref, modn_ref, o_ref, xn_ref, y_a, y_b = refs

    def compute(y_buf):
        if n_lat is None:
            z = z_ref[...]
        else:
            z = jnp.where(pl.program_id(0) >= n_lat, zc_ref[...], zl_ref[...])
        gate = _mod_row(mod_ref, 5)
        for c in range(D // TN):
            cs = slice(c * TN, (c + 1) * TN)
            y_buf[:, cs] = h_ref[:, cs] + gate[:, cs] * _dot(z, w_ref[:, cs])

    def emit(y_buf):
        _emit_tile(y_buf, o_ref, xn_ref, gn_ref, modn_ref, 2, TM_E)

    _lagged_steps(n, compute, emit, y_a, y_b)


def _split_specs(n_lat, tm, tile_of=lambda j: j):
    return [pl.BlockSpec((tm, D), lambda j: (jnp.minimum(tile_of(j), n_lat - 1), 0)),
            pl.BlockSpec((tm, D), lambda j: (jnp.maximum(tile_of(j) - n_lat, 0), 0))]


def _mm_res(z, z_ctx, w, h, mod, gains, rows):
    n = rows // TM_E
    cur, prev = _lag_specs(n, TM_E)
    n_lat = None if z_ctx is None else N_LAT // TM_E
    if z_ctx is None:
        z_specs, zs = [pl.BlockSpec((TM_E, D), lambda j: (cur(j), 0))], (z,)
    else:
        z_specs, zs = _split_specs(n_lat, TM_E, cur), (z, z_ctx)
    return pl.pallas_call(
        functools.partial(_mm_res_kernel, n=n, n_lat=n_lat),
        grid=(n + 1,),
        in_specs=z_specs + [
            _resident((D, D)),
            pl.BlockSpec((TM_E, D), lambda j: (cur(j), 0)),
            pl.BlockSpec((1, N_MOD, D), lambda j: (_seg(cur(j), TM_E), 0, 0)),
            pl.BlockSpec((3, D), lambda j: (0, 0)),
            pl.BlockSpec((1, N_MOD, D), lambda j: (_seg(prev(j), TM_E), 0, 0)),
        ],
        out_specs=[pl.BlockSpec((TM_E, D), lambda j: (prev(j), 0))] * 2,
        out_shape=[jax.ShapeDtypeStruct((rows, D), F32), jax.ShapeDtypeStruct((rows, D), BF16)],
        scratch_shapes=[pltpu.VMEM((TM_E, D), F32)] * 2,
        compiler_params=_params("arbitrary"),
        name="mm_res",
    )(*zs, w, h, mod, gains, mod)


def _gmlp_kernel(xn_ref, win_ref, vg_ref, ws_ref, bst_ref, z_ref, y_ref, *, nj):
    xn = xn_ref[...]
    for j in range(nj):
        y_ref[j] = jax.nn.gelu(_dot(xn, win_ref[:, j * TN:(j + 1) * TN]))

    half = nj // 2
    ssq = jnp.zeros((TM, 1), F32)
    for jj in range(half, nj):
        yv = y_ref[jj]
        ssq = ssq + jnp.sum(yv * yv, axis=-1, keepdims=True)
    inv = lax.rsqrt(ssq / D + RMS_EPS)
    gpb = TN // CHUNK
    for cb in range(half):
        v = (y_ref[half + cb] * inv * vg_ref[:, cb * TN:(cb + 1) * TN]).astype(BF16)
        u = y_ref[cb]
        for gg in range(gpb):
            grp = cb * gpb + gg
            cs = slice(gg * CHUNK, (gg + 1) * CHUNK)
            for c in range(TM // CHUNK):
                rs = slice(c * CHUNK, (c + 1) * CHUNK)
                sg = _dot(ws_ref[grp], v[rs, cs]) + bst_ref[:, grp:grp + 1]
                z_ref[rs, grp * CHUNK:(grp + 1) * CHUNK] = (u[rs, cs] * sg).astype(BF16)


def _resident(shape):
    return pl.BlockSpec(shape, lambda *_: (0,) * len(shape), pipeline_mode=pl.Buffered(1))


def _gmlp(xn, w_in, v_gain, w_s, b_s_t, rows):
    nj = (2 * D) // TN
    return pl.pallas_call(
        functools.partial(_gmlp_kernel, nj=nj),
        grid=(rows // TM,),
        in_specs=[
            pl.BlockSpec((TM, D), lambda i: (i, 0)),
            _resident((D, 2 * D)),
            pl.BlockSpec((1, D), lambda i: (0, 0)),
            _resident((A_GROUPS, CHUNK, CHUNK)),
            pl.BlockSpec((CHUNK, A_GROUPS), lambda i: (0, 0)),
        ],
        out_specs=pl.BlockSpec((TM, D), lambda i: (i, 0)),
        out_shape=jax.ShapeDtypeStruct((rows, D), BF16),
        scratch_shapes=[pltpu.VMEM((nj, TM, TN), F32)],
        compiler_params=_params("arbitrary"),
        name="gmlp",
    )(xn, w_in, v_gain, w_s, b_s_t)


def _qkv_kernel(xn_ref, w_ref, hg_ref, o_ref):
    xn = xn_ref[...]
    for j in range((3 * D) // TN):
        y = _dot(xn, w_ref[:, j * TN:(j + 1) * TN])
        part = j // (D // TN)
        if part == 2:
            o_ref[:, j * TN:(j + 1) * TN] = y.astype(BF16)
            continue
        gain = hg_ref[part:part + 1, :]
        for hh in range(TN // HEAD_DIM):
            yh = y[:, hh * HEAD_DIM:(hh + 1) * HEAD_DIM]
            ms = jnp.mean(yh * yh, axis=-1, keepdims=True)
            c0 = j * TN + hh * HEAD_DIM
            o_ref[:, c0:c0 + HEAD_DIM] = (yh * lax.rsqrt(ms + RMS_EPS) * gain).astype(BF16)


def _qkv(xn, w_qkv, head_gains, rows):
    return pl.pallas_call(
        _qkv_kernel,
        grid=(rows // TM,),
        in_specs=[
            pl.BlockSpec((TM, D), lambda i: (i, 0)),
            _resident((D, 3 * D)),
            pl.BlockSpec((2, HEAD_DIM), lambda i: (0, 0)),
        ],
        out_specs=pl.BlockSpec((TM, 3 * D), lambda i: (i, 0)),
        out_shape=jax.ShapeDtypeStruct((rows, 3 * D), BF16),
        compiler_params=_params("arbitrary"),
        name="qkv",
    )(xn, w_qkv, head_gains)


ATT_QR = 8
ATT_KR = 16
ATT_QB = ATT_QR * GRID_W
ATT_KB = ATT_KR * GRID_W
N_DR = 2 * WIN_H - 1
ATT_PATTERNS = ((0, 0), (ATT_QR, ATT_QR - WIN_H // 2), (ROWS - ATT_QR, ROWS - ATT_KR))


def _attn_kernel(q_ref, k_ref, v_ref, qc_ref, kc_ref, vc_ref, t_ref, o_ref, oc_ref, bias_ref):
    scale = HEAD_DIM ** -0.5
    left = lax.broadcasted_iota(jnp.int32, (GRID_W, 2 * GRID_W), 1) < GRID_W
    neg = jnp.full((GRID_W, 2 * GRID_W), NEG_INF, F32)

    for p, (r0, kr_base) in enumerate(ATT_PATTERNS):
        for qr in range(ATT_QR):
            r = r0 + qr
            rstart = min(max(r - WIN_H // 2, 0), ROWS - WIN_H)
            for kp in range(ATT_KR // 2):
                halves = []
                for kr in (kr_base + 2 * kp, kr_base + 2 * kp + 1):
                    inside = rstart <= kr < rstart + WIN_H
                    halves.append(t_ref[0, kr - r + WIN_H - 1] if inside else None)
                a, b = halves
                if a is None and b is None:
                    blk = neg
                else:
                    blk = jnp.where(left, neg if a is None else a, neg if b is None else b)
                bias_ref[p, qr * GRID_W:(qr + 1) * GRID_W, kp * 2 * GRID_W:(kp + 1) * 2 * GRID_W] = blk

    kc = kc_ref[...]
    vc = vc_ref[...]

    def block(q0, k0, p):
        q = q_ref[pl.ds(q0, ATT_QB), :]
        k = k_ref[pl.ds(k0, ATT_KB), :]
        v = v_ref[pl.ds(k0, ATT_KB), :]
        s_win = _dot_nt(q, k) * scale + bias_ref[p]
        s_ctx = _dot_nt(q, kc) * scale
        m = jnp.maximum(jnp.max(s_win, axis=-1, keepdims=True), jnp.max(s_ctx, axis=-1, keepdims=True))
        p_win = jnp.exp(s_win - m)
        p_ctx = jnp.exp(s_ctx - m)
        denom = jnp.sum(p_win, axis=-1, keepdims=True) + jnp.sum(p_ctx, axis=-1, keepdims=True)
        o = _dot(p_win.astype(BF16), v) + _dot(p_ctx.astype(BF16), vc)
        o_ref[pl.ds(q0, ATT_QB), :] = (o / denom).astype(BF16)

    block(0, 0, 0)
    for rb in range(1, ROWS // ATT_QR - 1):
        block(rb * ATT_QB, rb * ATT_QB - (WIN_H // 2) * GRID_W, 1)
    block(SEQ - ATT_QB, SEQ - ATT_KB, 2)

    s = _dot_nt(qc_ref[...], kc) * scale
    pc = jnp.exp(s - jnp.max(s, axis=-1, keepdims=True))
    oc = _dot(pc.astype(BF16), vc) / jnp.sum(pc, axis=-1, keepdims=True)
    oc_ref[...] = oc.astype(BF16)


def _attention(qkv, bias_tab):
    lat_blk = (SEQ, HEAD_DIM)
    ctx_blk = (CTX, HEAD_DIM)
    ctx0 = N_LAT // CTX
    return pl.pallas_call(
        _attn_kernel,
        grid=(N_HEADS, BATCH),
        in_specs=[
            pl.BlockSpec(lat_blk, lambda h, b: (b, h)),
            pl.BlockSpec(lat_blk, lambda h, b: (b, N_HEADS + h)),
            pl.BlockSpec(lat_blk, lambda h, b: (b, 2 * N_HEADS + h)),
            pl.BlockSpec(ctx_blk, lambda h, b: (ctx0 + b, h)),
            pl.BlockSpec(ctx_blk, lambda h, b: (ctx0 + b, N_HEADS + h)),
            pl.BlockSpec(ctx_blk, lambda h, b: (ctx0 + b, 2 * N_HEADS + h)),
            pl.BlockSpec((1, N_DR, GRID_W, 2 * GRID_W), lambda h, b: (h, 0, 0, 0)),
        ],
        out_specs=[
            pl.BlockSpec(lat_blk, lambda h, b: (b, h)),
            pl.BlockSpec(ctx_blk, lambda h, b: (b, h)),
        ],
        out_shape=[
            jax.ShapeDtypeStruct((N_LAT, D), BF16),
            jax.ShapeDtypeStruct((N_CTX, D), BF16),
        ],
        scratch_shapes=[pltpu.VMEM((len(ATT_PATTERNS), ATT_QB, ATT_KB), F32)],
        compiler_params=_params("parallel", "parallel"),
        name="nat_attention",
    )(qkv, qkv, qkv, qkv, qkv, qkv, bias_tab)


def _attn_bias_table(rpb):
    qcol = np.arange(GRID_W)[:, None]
    kcol = np.arange(2 * GRID_W)[None, :] % GRID_W
    cstart = np.clip(qcol - WIN_W // 2, 0, GRID_W - WIN_W)
    col_valid = (kcol >= cstart) & (kcol < cstart + WIN_W)
    dc_idx = np.clip(kcol - qcol, 1 - WIN_W, WIN_W - 1) + (WIN_W - 1)
    onehot = (dc_idx[None] == np.arange(2 * WIN_W - 1)[:, None, None]) & col_valid[None]
    onehot = jnp.asarray(onehot.reshape(2 * WIN_W - 1, -1), F32)
    mask = jnp.asarray(np.where(col_valid, 0.0, NEG_INF).reshape(-1), F32)
    t = jnp.dot(rpb.reshape(N_HEADS * N_DR, 2 * WIN_W - 1), onehot, precision=HIGHEST) + mask
    return t.reshape(N_HEADS, N_DR, GRID_W, 2 * GRID_W)


def _proj_kernel(xn_ref, w_ref, o_ref):
    xn = xn_ref[...]
    for j in range(D // TN):
        o_ref[:, j * TN:(j + 1) * TN] = _dot(xn, w_ref[:, j * TN:(j + 1) * TN])


def _proj(xn, w, rows):
    return pl.pallas_call(
        _proj_kernel,
        grid=(rows // TM,),
        in_specs=[
            pl.BlockSpec((TM, D), lambda i: (i, 0)),
            _resident((D, D)),
        ],
        out_specs=pl.BlockSpec((TM, D), lambda i: (i, 0)),
        out_shape=jax.ShapeDtypeStruct((rows, D), F32),
        compiler_params=_params("arbitrary"),
        name="s5_in_proj",
    )(xn, w)


S5_CL = SEQ // S5_Q
S5_CC = CTX // S5_Q
S5_HALF = LANES // C_GROUP


def _s5_kernel(ul_ref, uc_ref, mi_ref, ms_ref, mo_ref, aq_ref, yl_ref, yc_ref,
               xl_scr, xc_scr, s_scr, ssw_scr, hp_scr, yl_acc, yc_acc):
    d = pl.program_id(2)
    ns = 2 * C_STATE

    def granule_transpose(rows):
        n = rows[0].shape[0]
        granule = lax.broadcasted_iota(jnp.int32, (n, LANES), 1) // C_GROUP
        rows = list(rows)
        k = S5_GB // 2
        while k:
            hi = (granule & k) != 0
            for i in range(S5_GB):
                if i & k:
                    continue
                a, b = rows[i], rows[i + k]
                rows[i] = jnp.where(hi, pltpu.roll(b, k * C_GROUP, 1), a)
                rows[i + k] = jnp.where(hi, b, pltpu.roll(a, LANES - k * C_GROUP, 1))
            k //= 2
        return rows

    def to_chunks(u_ref, x_scr, nch):
        for hf in range(S5_Q // S5_HALF):
            by_token = [u_ref[pl.ds(hf * S5_HALF + tl, nch, stride=S5_Q), :] for tl in range(S5_HALF)]
            for gl, x in enumerate(granule_transpose(by_token)):
                x_scr[gl, :, hf * LANES:(hf + 1) * LANES] = x.astype(BF16)

    def from_chunks(y_acc, y_ref, nch):
        for hf in range(S5_Q // S5_HALF):
            by_group = [y_acc[gl, :, hf * LANES:(hf + 1) * LANES] for gl in range(S5_GB)]
            for tl, y in enumerate(granule_transpose(by_group)):
                y_ref[pl.ds(hf * S5_HALF + tl, nch, stride=S5_Q), :] = y

    @pl.when(d == 0)
    def _():
        to_chunks(ul_ref, xl_scr, S5_CL)
        to_chunks(uc_ref, xc_scr, S5_CC)
        yl_acc[...] = jnp.zeros_like(yl_acc)
        yc_acc[...] = jnp.zeros_like(yc_acc)

    lat0 = S5_CC * S5_GB
    for j in range(S5_GB):
        xl, xc = xl_scr[j], xc_scr[j]
        yl_acc[j] += _dot(xl, mi_ref[0, j])
        yc_acc[j] += _dot(xc, mi_ref[0, j])
        sl = _dot(xl, ms_ref[0, j])
        sc = _dot(xc, ms_ref[0, j])
        s_scr[pl.ds(j, S5_CC, stride=S5_GB), :] = sc[:, :ns]
        s_scr[pl.ds(lat0 + j, S5_CL, stride=S5_GB), :] = sl[:, :ns]
        ssw_scr[pl.ds(j, S5_CC, stride=S5_GB), :] = sc[:, ns:]
        ssw_scr[pl.ds(lat0 + j, S5_CL, stride=S5_GB), :] = sl[:, ns:]

    a1 = aq_ref[0, 0]
    a2 = aq_ref[0, 1]
    a3 = aq_ref[0, 2]

    def step(pos, carry):
        hs, hsw = carry
        row = pl.multiple_of(pos * S5_GB, S5_GB)
        hp_scr[pl.ds(row, S5_GB), :] = hs
        s = s_scr[pl.ds(row, S5_GB), :]
        ssw = ssw_scr[pl.ds(row, S5_GB), :]
        return hs * a1 + hsw * a2 + s, hsw * a1 + hs * a3 + ssw

    def ctx_step(i, carry):
        return step(jnp.where(d == 0, i, S5_CC - 1 - i), carry)

    def lat_step(i, carry):
        return step(S5_CC + jnp.where(d == 0, i, S5_CL - 1 - i), carry)

    zero = jnp.zeros((S5_GB, ns), F32)
    carry = lax.fori_loop(0, S5_CC, ctx_step, (zero, zero))
    lax.fori_loop(0, S5_CL, lat_step, carry, unroll=4)

    for j in range(S5_GB):
        mo = mo_ref[0, j]
        hc = hp_scr[pl.ds(j, S5_CC, stride=S5_GB), :]
        hl = hp_scr[pl.ds(lat0 + j, S5_CL, stride=S5_GB), :]
        yc_acc[j] += _dot(hc.astype(BF16), mo)
        yl_acc[j] += _dot(hl.astype(BF16), mo)

    @pl.when(d == 1)
    def _():
        from_chunks(yl_acc, yl_ref, S5_CL)
        from_chunks(yc_acc, yc_ref, S5_CC)


def _s5_scan(u, mi, ms, mo, aq):
    ns = 2 * C_STATE
    nrow = (S5_CC + S5_CL) * S5_GB
    ctx0 = N_LAT // CTX
    return pl.pallas_call(
        _s5_kernel,
        grid=(BATCH, C_GROUPS // S5_GB, 2),
        in_specs=[
            pl.BlockSpec((SEQ, LANES), lambda b, g, d: (b, g)),
            pl.BlockSpec((CTX, LANES), lambda b, g, d: (ctx0 + b, g)),
            pl.BlockSpec((1, S5_GB, S5_XW, S5_XW), lambda b, g, d: (d, g, 0, 0)),
            pl.BlockSpec((1, S5_GB, S5_XW, 2 * ns), lambda b, g, d: (d, g, 0, 0)),
            pl.BlockSpec((1, S5_GB, ns, S5_XW), lambda b, g, d: (d, g, 0, 0)),
            pl.BlockSpec((1, 3, S5_GB, ns), lambda b, g, d: (d, 0, g, 0)),
        ],
        out_specs=[
            pl.BlockSpec((SEQ, LANES), lambda b, g, d: (b, g)),
            pl.BlockSpec((CTX, LANES), lambda b, g, d: (b, g)),
        ],
        out_shape=[
            jax.ShapeDtypeStruct((N_LAT, D), F32),
            jax.ShapeDtypeStruct((N_CTX, D), F32),
        ],
        scratch_shapes=[
            pltpu.VMEM((S5_GB, S5_CL, S5_XW), BF16),
            pltpu.VMEM((S5_GB, S5_CC, S5_XW), BF16),
            pltpu.VMEM((nrow, ns), F32),
            pltpu.VMEM((nrow, ns), F32),
            pltpu.VMEM((nrow, ns), F32),
            pltpu.VMEM((S5_GB, S5_CL, S5_XW), F32),
            pltpu.VMEM((S5_GB, S5_CC, S5_XW), F32),
        ],
        compiler_params=_params("arbitrary", "arbitrary", "arbitrary"),
        name="s5_scan",
    )(u, u, mi, ms, mo, aq)


def _s5_weights(a_re, a_im, log_dt, b_re, b_im, c_re, c_im):
    q = S5_Q
    a_re, a_im, b_re, b_im, c_re, c_im = (v.astype(F32) for v in (a_re, a_im, b_re, b_im, c_re, c_im))
    dt = jnp.exp(log_dt.astype(F32))[..., None]
    zr, zi = a_re * dt, a_im * dt
    ab_r, ab_i = jnp.exp(zr) * jnp.cos(zi), jnp.exp(zr) * jnp.sin(zi)
    den = a_re * a_re + a_im * a_im
    f_r = ((ab_r - 1.0) * a_re + ab_i * a_im) / den
    f_i = (ab_i * a_re - (ab_r - 1.0) * a_im) / den
    bb_r = f_r[..., None] * b_re - f_i[..., None] * b_im
    bb_i = f_r[..., None] * b_im + f_i[..., None] * b_re

    taus = jnp.arange(q + 1, dtype=F32)[:, None]
    mag = jnp.exp(zr[:, :, None] * taus)
    pw_r, pw_i = mag * jnp.cos(zi[:, :, None] * taus), mag * jnp.sin(zi[:, :, None] * taus)

    col = np.arange(S5_XW)
    e_t = jnp.asarray(col[None, :] // C_GROUP == np.arange(q)[:, None], F32)
    e_o = jnp.asarray(col[None, :] % C_GROUP == np.arange(C_GROUP)[:, None], F32)
    xc_r = jnp.einsum("dgop,ox->dgpx", c_re, e_o, precision=HIGHEST)
    xc_i = jnp.einsum("dgop,ox->dgpx", c_im, e_o, precision=HIGHEST)

    def readout(p_r, p_i):
        xp_r = jnp.einsum("dgtp,tx->dgpx", p_r, e_t, precision=HIGHEST)
        xp_i = jnp.einsum("dgtp,tx->dgpx", p_i, e_t, precision=HIGHEST)
        return xc_r * xp_r - xc_i * xp_i, xc_r * xp_i + xc_i * xp_r

    wo_r, wo_i = readout(jnp.stack([pw_r[0, :, 1:q + 1], pw_r[1, :, q:0:-1]]),
                         jnp.stack([pw_i[0, :, 1:q + 1], pw_i[1, :, q:0:-1]]))
    m_out = jnp.concatenate([wo_r, -wo_i], axis=2).astype(BF16)

    ck_r, ck_i = readout(jnp.stack([pw_r[0, :, :q], pw_r[1, :, q - 1::-1]]),
                         jnp.stack([pw_i[0, :, :q], pw_i[1, :, q - 1::-1]]))
    kmat = jnp.sum(bb_r[:, :, :, :, None] * ck_r[:, :, :, None, :]
                   - bb_i[:, :, :, :, None] * ck_i[:, :, :, None, :], axis=2)
    zpad = jnp.zeros_like(kmat[0, :, :, :(q - 1) * C_GROUP])
    kflat = jnp.stack([jnp.concatenate([zpad, kmat[0]], axis=-1), jnp.concatenate([kmat[1], zpad], axis=-1)])
    m_intra = jnp.stack([kflat[..., (q - 1 - s) * C_GROUP:(q - 1 - s) * C_GROUP + S5_XW] for s in range(q)],
                        axis=2).reshape(2, C_GROUPS, S5_XW, S5_XW).astype(BF16)

    ps_r = jnp.stack([pw_r[0, :, q - 1::-1], pw_r[1, :, :q]])
    ps_i = jnp.stack([pw_i[0, :, q - 1::-1], pw_i[1, :, :q]])
    bt_r, bt_i = bb_r.transpose(0, 1, 3, 2), bb_i.transpose(0, 1, 3, 2)
    p4_r = jnp.concatenate([ps_r] * 4, axis=-1)[:, :, :, None]
    p4_i = jnp.concatenate([ps_i] * 4, axis=-1)[:, :, :, None]
    b4_a = jnp.concatenate([bt_r, bt_i, bt_i, bt_r], axis=-1)[:, :, None]
    b4_b = jnp.concatenate([-bt_i, bt_r, bt_r, -bt_i], axis=-1)[:, :, None]
    m_state = (p4_r * b4_a + p4_i * b4_b).reshape(2, C_GROUPS, S5_XW, 4 * C_STATE).astype(BF16)

    ar, ai = pw_r[:, :, q], pw_i[:, :, q]
    aq3 = jnp.stack([jnp.concatenate([ar, ar], -1), jnp.concatenate([-ai, ai], -1),
                     jnp.concatenate([ai, -ai], -1)], axis=1)
    return m_intra, m_state, m_out, aq3


def _glu_kernel(u_ref, yl_ref, yc_ref, dsk_ref, w_ref, h_ref, mod_ref, gn_ref, modn_ref, o_ref, xn_ref,
                y_a, y_b, *, n, n_lat):
    def compute(y_buf):
        y = jnp.where(pl.program_id(0) >= n_lat, yc_ref[...], yl_ref[...])
        z = jax.nn.gelu(dsk_ref[...] * u_ref[...] + y).astype(BF16)
        gate = _mod_row(mod_ref, 5)
        for c in range(D // TN):
            cs = slice(c * TN, (c + 1) * TN)
            a = _dot(z, w_ref[:, cs])
            g = _dot(z, w_ref[:, D + c * TN:D + (c + 1) * TN])
            y_buf[:, cs] = h_ref[:, cs] + gate[:, cs] * (a * jax.nn.sigmoid(g))

    def emit(y_buf):
        _emit_tile(y_buf, o_ref, xn_ref, gn_ref, modn_ref, 2, TM_E)

    _lagged_steps(n, compute, emit, y_a, y_b)


def _glu(u, y_lat, y_ctx, d_skip, w_glu, h, mod, gains, rows):
    n = rows // TM_E
    cur, prev = _lag_specs(n, TM_E)
    n_lat = N_LAT // TM_E
    tile = pl.BlockSpec((TM_E, D), lambda j: (cur(j), 0))
    return pl.pallas_call(
        functools.partial(_glu_kernel, n=n, n_lat=n_lat),
        grid=(n + 1,),
        in_specs=[tile] + _split_specs(n_lat, TM_E, cur) + [
            pl.BlockSpec((1, D), lambda j: (0, 0)),
            _resident((D, 2 * D)),
            tile,
            pl.BlockSpec((1, N_MOD, D), lambda j: (_seg(cur(j), TM_E), 0, 0)),
            pl.BlockSpec((3, D), lambda j: (0, 0)),
            pl.BlockSpec((1, N_MOD, D), lambda j: (_seg(prev(j), TM_E), 0, 0)),
        ],
        out_specs=[pl.BlockSpec((TM_E, D), lambda j: (prev(j), 0))] * 2,
        out_shape=[jax.ShapeDtypeStruct((rows, D), F32), jax.ShapeDtypeStruct((rows, D), BF16)],
        scratch_shapes=[pltpu.VMEM((TM_E, D), F32)] * 2,
        compiler_params=_params("arbitrary"),
        name="s5_glu",
    )(u, y_lat, y_ctx, d_skip, w_glu, h, mod, gains, mod)


def kernel(x, c, ctx, c_ctx, w_ada, b_ada, norm_g, ffn_w_gu, ffn_w_down, a_w_in, a_v_gain, a_w_s, a_b_s, a_w_out, b_w_qkv, b_q_gain, b_k_gain, b_rpb, b_w_out, c_w_in, c_a_re, c_a_im, c_log_dt, c_b_re, c_b_im, c_c_re, c_c_im, c_d, c_w_glu):
    h = jnp.concatenate([x.reshape(N_LAT, D), ctx.reshape(N_CTX, D)], axis=0).astype(F32)
    cond8 = jnp.concatenate([c, c_ctx[None], jnp.zeros((8 - BATCH - 1, D), c.dtype)], axis=0).astype(F32)
    mods = _adaln(cond8, w_ada.astype(F32), b_ada.astype(F32))
    norm_g = norm_g.astype(F32)
    ffn_w_gu = ffn_w_gu.astype(F32)
    ffn_w_down = ffn_w_down.astype(BF16)

    xn = _prenorm(h, mods[0], norm_g[0], 0, N_ALL)
    for i in range(DEPTH):
        kind, j = i % 3, i // 3
        last = i == DEPTH - 1
        rows = N_LAT if last else N_ALL
        mod, gains = mods[i], norm_g[i]

        act = _ffn_up(xn, ffn_w_gu, i, 0, rows)
        h, xn = _ffn_down_emit(act, ffn_w_down, h, mod, i, 0, 0, gains, mod, 1, rows)

        if kind == 0:
            z = _gmlp(xn, a_w_in[j].astype(BF16), a_v_gain[j].astype(F32)[None],
                      a_w_s[j].astype(BF16), a_b_s[j].astype(F32).T, rows)
            h, xn = _mm_res(z, None, a_w_out[j].astype(BF16), h, mod, gains, rows)
        elif kind == 1:
            head_gains = jnp.stack([b_q_gain[j], b_k_gain[j]]).astype(F32)
            qkv = _qkv(xn, b_w_qkv[j].astype(BF16), head_gains, rows)
            o_lat, o_ctx = _attention(qkv, _attn_bias_table(b_rpb[j].astype(F32)))
            h, xn = _mm_res(o_lat, o_ctx, b_w_out[j].astype(BF16), h, mod, gains, rows)
        else:
            u = _proj(xn, c_w_in[j].astype(BF16), rows)
            y_lat, y_ctx = _s5_scan(u, *_s5_weights(c_a_re[j], c_a_im[j], c_log_dt[j], c_b_re[j], c_b_im[j],
                                                    c_c_re[j], c_c_im[j]))
            h, xn = _glu(u, y_lat, y_ctx, c_d[j].astype(F32)[None], c_w_glu[j].astype(BF16), h, mod, gains, rows)

        act = _ffn_up(xn, ffn_w_gu, i, 1, rows)
        if last:
            h = _ffn_down(act, ffn_w_down, h, mod, i, 1, 2, rows)
        else:
            h, xn = _ffn_down_emit(act, ffn_w_down, h, mod, i, 1, 2, norm_g[i + 1], mods[i + 1], 0, rows)

    return h[:N_LAT].reshape(BATCH, SEQ, D).astype(x.dtype)
```

```python
import functools

import numpy as np
import jax
import jax.numpy as jnp
from jax import lax
from jax.experimental import pallas as pl
from jax.experimental.pallas import tpu as pltpu

F32 = jnp.float32
BF16 = jnp.bfloat16
HIGHEST = lax.Precision.HIGHEST

D = 2048
BATCH = 2
SEQ = 4096
CTX = 256
DEPTH = 4
N_LAT = BATCH * SEQ
N_CTX = BATCH * CTX
N_ALL = N_LAT + N_CTX
N_MOD = 9
D_FF = 5632
RMS_EPS = 1e-6
NEG_INF = -1e30
GRID_W = 64
ROWS = SEQ // GRID_W
CHUNK = 128
A_GROUPS = 16
N_HEADS = 16
HEAD_DIM = 128
WIN_H = 8
WIN_W = 16
C_GROUP = 16
C_GROUPS = D // C_GROUP
C_STATE = 64
LANES = 128
NORM_ROWS = 16
NORM_UNROLL = 4

TM = 512
TM_F = 1024
TF = 512
FF_SPLIT = 2
TM_E = 256
TN = 512
ADA_TN = 2048
S5_Q = 16
S5_GB = LANES // C_GROUP
S5_XW = S5_Q * C_GROUP
VMEM_LIMIT = 56 * 1024 * 1024


def _params(*sem):
    return pltpu.CompilerParams(dimension_semantics=sem, vmem_limit_bytes=VMEM_LIMIT)


def _seg(i, tm=TM):
    return jnp.minimum((i * tm) // SEQ, 2)


def _mod_row(mod_ref, r):
    return mod_ref[0, r:r + 1, :]


def _norm_mod_store(h_ref, xn_ref, m, g_ref, mod_ref, s):
    gs = g_ref[s:s + 1, :] * (1.0 + _mod_row(mod_ref, 3 * s + 1))
    shift = _mod_row(mod_ref, 3 * s)

    def chunk(r, carry):
        rows = pl.ds(pl.multiple_of(r * NORM_ROWS, NORM_ROWS), NORM_ROWS)
        x = h_ref[rows, :]
        ms = jnp.mean(x * x, axis=-1, keepdims=True)
        xn_ref[rows, :] = (x * lax.rsqrt(ms + RMS_EPS) * gs + shift).astype(BF16)
        return carry

    lax.fori_loop(0, m // NORM_ROWS, chunk, 0, unroll=NORM_UNROLL)


def _dot(a, b):
    return jnp.dot(a, b, preferred_element_type=F32)


def _dot_nt(a, b):
    return lax.dot_general(a, b, (((1,), (1,)), ((), ())), preferred_element_type=F32)


def _ada_kernel(c_ref, w_ref, b_ref, o_ref):
    c = c_ref[...]
    a = (c * jax.nn.sigmoid(c)).astype(BF16)
    o_ref[0] = _dot(a, w_ref[0].astype(BF16)) + b_ref[0]


def _adaln(cond8, w_ada, b_ada):
    n = N_MOD * D
    out = pl.pallas_call(
        _ada_kernel,
        grid=(DEPTH, n // ADA_TN),
        in_specs=[
            pl.BlockSpec((8, D), lambda l, j: (0, 0)),
            pl.BlockSpec((1, D, ADA_TN), lambda l, j: (l, 0, j)),
            pl.BlockSpec((1, 1, ADA_TN), lambda l, j: (l, 0, j)),
        ],
        out_specs=pl.BlockSpec((1, 8, ADA_TN), lambda l, j: (l, 0, j)),
        out_shape=jax.ShapeDtypeStruct((DEPTH, 8, n), F32),
        compiler_params=_params("parallel", "parallel"),
        name="adaln",
    )(cond8, w_ada, b_ada.reshape(DEPTH, 1, n))
    return out[:, :3].reshape(DEPTH, 3, N_MOD, D)


def _prenorm_kernel(h_ref, mod_ref, g_ref, o_ref, *, s):
    _norm_mod_store(h_ref, o_ref, TM, g_ref, mod_ref, s)


def _prenorm(h, mod, gains, s, rows):
    return pl.pallas_call(
        functools.partial(_prenorm_kernel, s=s),
        grid=(rows // TM,),
        in_specs=[
            pl.BlockSpec((TM, D), lambda i: (i, 0)),
            pl.BlockSpec((1, N_MOD, D), lambda i: (_seg(i), 0, 0)),
            pl.BlockSpec((3, D), lambda i: (0, 0)),
        ],
        out_specs=pl.BlockSpec((TM, D), lambda i: (i, 0)),
        out_shape=jax.ShapeDtypeStruct((rows, D), BF16),
        compiler_params=_params("parallel"),
        name="ffn_prenorm",
    )(h, mod, gains)


def _ffn_up_kernel(xn_ref, wg_ref, wu_ref, o_ref, *, subtiles):
    wg = wg_ref[0, 0].astype(BF16)
    wu = wu_ref[0, 0].astype(BF16)
    for r0, m in subtiles:
        xn = xn_ref[r0:r0 + m, :]
        g = _dot(xn, wg)
        u = _dot(xn, wu)
        o_ref[r0:r0 + m, :] = ((g * jax.nn.sigmoid(g)) * u).astype(BF16)


def _ffn_up(xn, w_gu, layer, half, rows):
    nk = D_FF // TF
    hr = rows // FF_SPLIT
    subtiles = [(r0, min(TM_F, hr - r0)) for r0 in range(0, hr, TM_F)]
    return pl.pallas_call(
        functools.partial(_ffn_up_kernel, subtiles=subtiles),
        grid=(FF_SPLIT, nk),
        in_specs=[
            pl.BlockSpec((hr, D), lambda r, k: (r, 0), pipeline_mode=pl.Buffered(1)),
            pl.BlockSpec((1, 1, D, TF), lambda r, k: (layer, half, 0, k)),
            pl.BlockSpec((1, 1, D, TF), lambda r, k: (layer, half, 0, nk + k)),
        ],
        out_specs=pl.BlockSpec((hr, TF), lambda r, k: (r, k)),
        out_shape=jax.ShapeDtypeStruct((rows, D_FF), BF16),
        compiler_params=_params("arbitrary", "arbitrary"),
        name="ffn_up",
    )(xn, w_gu, w_gu)


def _ffn_down_kernel(a_ref, wd_ref, h_ref, mod_ref, o_ref, *, s):
    a = a_ref[...]
    gate = 0.5 * _mod_row(mod_ref, 3 * s + 2)
    for j in range(D // TN):
        cs = slice(j * TN, (j + 1) * TN)
        o_ref[:, cs] = h_ref[:, cs] + gate[:, cs] * _dot(a, wd_ref[0, 0, :, cs])


def _ffn_down(act, w_down, h, mod, layer, half, s, rows):
    tile = pl.BlockSpec((TM, D), lambda i: (i, 0))
    return pl.pallas_call(
        functools.partial(_ffn_down_kernel, s=s),
        grid=(rows // TM,),
        in_specs=[
            pl.BlockSpec((TM, D_FF), lambda i: (i, 0)),
            pl.BlockSpec((1, 1, D_FF, D), lambda i: (layer, half, 0, 0), pipeline_mode=pl.Buffered(1)),
            tile,
            pl.BlockSpec((1, N_MOD, D), lambda i: (_seg(i), 0, 0)),
        ],
        out_specs=tile,
        out_shape=jax.ShapeDtypeStruct((rows, D), F32),
        compiler_params=_params("arbitrary"),
        name="ffn_down",
    )(act, w_down, h, mod)


def _emit_tile(y_buf, o_ref, xn_ref, gn_ref, modn_ref, sn, tm):
    gs = gn_ref[sn:sn + 1, :] * (1.0 + _mod_row(modn_ref, 3 * sn + 1))
    shift = _mod_row(modn_ref, 3 * sn)
    for r in range(tm // NORM_ROWS):
        rows = slice(r * NORM_ROWS, (r + 1) * NORM_ROWS)
        y = y_buf[rows, :]
        o_ref[rows, :] = y
        ms = jnp.mean(y * y, axis=-1, keepdims=True)
        xn_ref[rows, :] = (y * lax.rsqrt(ms + RMS_EPS) * gs + shift).astype(BF16)


def _lagged_steps(n, compute, emit, y_a, y_b):
    j = pl.program_id(0)
    last = y_a if (n - 1) % 2 == 0 else y_b

    @pl.when(j == 0)
    def _():
        compute(y_a)

    @pl.when((j > 0) & (j < n) & (j % 2 == 1))
    def _():
        emit(y_a)
        compute(y_b)

    @pl.when((j > 0) & (j < n) & (j % 2 == 0))
    def _():
        emit(y_b)
        compute(y_a)

    @pl.when(j == n)
    def _():
        emit(last)


def _lag_specs(n, tm):
    cur = lambda j: jnp.minimum(j, n - 1)
    prev = lambda j: jnp.maximum(j - 1, 0)
    return cur, prev


def _ffn_down_emit_kernel(a_ref, wd_ref, h_ref, mod_ref, gn_ref, modn_ref, o_ref, xn_ref, y_a, y_b,
                          *, s, sn, n):
    def compute(y_buf):
        a = a_ref[...]
        gate = 0.5 * _mod_row(mod_ref, 3 * s + 2)
        for c in range(D // TN):
            cs = slice(c * TN, (c + 1) * TN)
            y_buf[:, cs] = h_ref[:, cs] + gate[:, cs] * _dot(a, wd_ref[0, 0, :, cs])

    def emit(y_buf):
        _emit_tile(y_buf, o_ref, xn_ref, gn_ref, modn_ref, sn, TM_E)

    _lagged_steps(n, compute, emit, y_a, y_b)


def _ffn_down_emit(act, w_down, h, mod, layer, half, s, gains_n, mod_n, sn, rows):
    n = rows // TM_E
    cur, prev = _lag_specs(n, TM_E)
    return pl.pallas_call(
        functools.partial(_ffn_down_emit_kernel, s=s, sn=sn, n=n),
        grid=(n + 1,),
        in_specs=[
            pl.BlockSpec((TM_E, D_FF), lambda j: (cur(j), 0)),
            pl.BlockSpec((1, 1, D_FF, D), lambda j: (layer, half, 0, 0), pipeline_mode=pl.Buffered(1)),
            pl.BlockSpec((TM_E, D), lambda j: (cur(j), 0)),
            pl.BlockSpec((1, N_MOD, D), lambda j: (_seg(cur(j), TM_E), 0, 0)),
            pl.BlockSpec((3, D), lambda j: (0, 0)),
            pl.BlockSpec((1, N_MOD, D), lambda j: (_seg(prev(j), TM_E), 0, 0)),
        ],
        out_specs=[pl.BlockSpec((TM_E, D), lambda j: (prev(j), 0))] * 2,
        out_shape=[jax.ShapeDtypeStruct((rows, D), F32), jax.ShapeDtypeStruct((rows, D), BF16)],
        scratch_shapes=[pltpu.VMEM((TM_E, D), F32)] * 2,
        compiler_params=_params("arbitrary"),
        name="ffn_down_emit",
    )(act, w_down, h, mod, gains_n, mod_n)


def _mm_res_kernel(*refs, n, n_lat):
    if n_lat is None:
        z_ref, w_ref, h_ref, mod_ref, gn_ref, modn_ref, o_ref, xn_ref, y_a, y_b = refs
    else:
        zl_ref, zc_ref, w_ref, h_ref, mod_ref, gn_ref, modn_ref, o_ref, xn_ref, y_a, y_b = refs

    def compute(y_buf):
        if n_lat is None:
            z = z_ref[...]
        else:
            z = jnp.where(pl.program_id(0) >= n_lat, zc_ref[...], zl_ref[...])
        gate = _mod_row(mod_ref, 5)
        for c in range(D // TN):
            cs = slice(c * TN, (c + 1) * TN)
            y_buf[:, cs] = h_ref[:, cs] + gate[:, cs] * _dot(z, w_ref[:, cs])

    def emit(y_buf):
        _emit_tile(y_buf, o_ref, xn_ref, gn_ref, modn_ref, 2, TM)

    _lagged_steps(n, compute, emit, y_a, y_b)


def _split_specs(n_lat, tm, tile_of=lambda j: j):
    return [pl.BlockSpec((tm, D), lambda j: (jnp.minimum(tile_of(j), n_lat - 1), 0)),
            pl.BlockSpec((tm, D), lambda j: (jnp.maximum(tile_of(j) - n_lat, 0), 0))]


def _mm_res(z, z_ctx, w, h, mod, gains, rows):
    n = rows // TM
    cur, prev = _lag_specs(n, TM)
    n_lat = None if z_ctx is None else N_LAT // TM
    if z_ctx is None:
        z_specs, zs = [pl.BlockSpec((TM, D), lambda j: (cur(j), 0))], (z,)
    else:
        z_specs, zs = _split_specs(n_lat, TM, cur), (z, z_ctx)
    return pl.pallas_call(
        functools.partial(_mm_res_kernel, n=n, n_lat=n_lat),
        grid=(n + 1,),
        in_specs=z_specs + [
            _resident((D, D)),
            pl.BlockSpec((TM, D), lambda j: (cur(j), 0)),
            pl.BlockSpec((1, N_MOD, D), lambda j: (_seg(cur(j), TM), 0, 0)),
            pl.BlockSpec((3, D), lambda j: (0, 0)),
            pl.BlockSpec((1, N_MOD, D), lambda j: (_seg(prev(j), TM), 0, 0)),
        ],
        out_specs=[pl.BlockSpec((TM, D), lambda j: (prev(j), 0))] * 2,
        out_shape=[jax.ShapeDtypeStruct((rows, D), F32), jax.ShapeDtypeStruct((rows, D), BF16)],
        scratch_shapes=[pltpu.VMEM((TM, D), F32)] * 2,
        compiler_params=_params("arbitrary"),
        name="mm_res",
    )(*zs, w, h, mod, gains, mod)


def _gmlp_kernel(xn_ref, win_ref, vg_ref, ws_ref, bst_ref, z_ref, y_ref, *, nj):
    xn = xn_ref[...]
    for j in range(nj):
        y_ref[j] = jax.nn.gelu(_dot(xn, win_ref[:, j * TN:(j + 1) * TN]))

    half = nj // 2
    ssq = jnp.zeros((TM, 1), F32)
    for jj in range(half, nj):
        yv = y_ref[jj]
        ssq = ssq + jnp.sum(yv * yv, axis=-1, keepdims=True)
    inv = lax.rsqrt(ssq / D + RMS_EPS)
    gpb = TN // CHUNK
    for cb in range(half):
        v = (y_ref[half + cb] * inv * vg_ref[:, cb * TN:(cb + 1) * TN]).astype(BF16)
        u = y_ref[cb]
        for gg in range(gpb):
            grp = cb * gpb + gg
            cs = slice(gg * CHUNK, (gg + 1) * CHUNK)
            for c in range(TM // CHUNK):
                rs = slice(c * CHUNK, (c + 1) * CHUNK)
                sg = _dot(ws_ref[grp], v[rs, cs]) + bst_ref[:, grp:grp + 1]
                z_ref[rs, grp * CHUNK:(grp + 1) * CHUNK] = (u[rs, cs] * sg).astype(BF16)


def _resident(shape):
    return pl.BlockSpec(shape, lambda *_: (0,) * len(shape), pipeline_mode=pl.Buffered(1))


def _gmlp(xn, w_in, v_gain, w_s, b_s_t, rows):
    nj = (2 * D) // TN
    return pl.pallas_call(
        functools.partial(_gmlp_kernel, nj=nj),
        grid=(rows // TM,),
        in_specs=[
            pl.BlockSpec((TM, D), lambda i: (i, 0)),
            _resident((D, 2 * D)),
            pl.BlockSpec((1, D), lambda i: (0, 0)),
            _resident((A_GROUPS, CHUNK, CHUNK)),
            pl.BlockSpec((CHUNK, A_GROUPS), lambda i: (0, 0)),
        ],
        out_specs=pl.BlockSpec((TM, D), lambda i: (i, 0)),
        out_shape=jax.ShapeDtypeStruct((rows, D), BF16),
        scratch_shapes=[pltpu.VMEM((nj, TM, TN), F32)],
        compiler_params=_params("arbitrary"),
        name="gmlp",
    )(xn, w_in, v_gain, w_s, b_s_t)


def _qkv_kernel(xn_ref, w_ref, hg_ref, o_ref):
    xn = xn_ref[...]
    for j in range((3 * D) // TN):
        y = _dot(xn, w_ref[:, j * TN:(j + 1) * TN])
        part = j // (D // TN)
        if part == 2:
            o_ref[:, j * TN:(j + 1) * TN] = y.astype(BF16)
            continue
        gain = hg_ref[part:part + 1, :]
        for hh in range(TN // HEAD_DIM):
            yh = y[:, hh * HEAD_DIM:(hh + 1) * HEAD_DIM]
            ms = jnp.mean(yh * yh, axis=-1, keepdims=True)
            c0 = j * TN + hh * HEAD_DIM
            o_ref[:, c0:c0 + HEAD_DIM] = (yh * lax.rsqrt(ms + RMS_EPS) * gain).astype(BF16)


def _qkv(xn, w_qkv, head_gains, rows):
    return pl.pallas_call(
        _qkv_kernel,
        grid=(rows // TM,),
        in_specs=[
            pl.BlockSpec((TM, D), lambda i: (i, 0)),
            _resident((D, 3 * D)),
            pl.BlockSpec((2, HEAD_DIM), lambda i: (0, 0)),
        ],
        out_specs=pl.BlockSpec((TM, 3 * D), lambda i: (i, 0)),
        out_shape=jax.ShapeDtypeStruct((rows, 3 * D), BF16),
        compiler_params=_params("arbitrary"),
        name="qkv",
    )(xn, w_qkv, head_gains)


ATT_QR = 8
ATT_KR = ATT_QR + WIN_H
ATT_QB = ATT_QR * GRID_W
ATT_KB = ATT_KR * GRID_W
N_DR = 2 * WIN_H - 1
ATT_PATTERNS = ((0, 0), (ATT_QR, ATT_QR - WIN_H // 2), (ROWS - ATT_QR, ROWS - ATT_KR))


def _attn_kernel(q_ref, k_ref, v_ref, qc_ref, kc_ref, vc_ref, t_ref, o_ref, oc_ref, bias_ref):
    scale = HEAD_DIM ** -0.5
    left = lax.broadcasted_iota(jnp.int32, (GRID_W, 2 * GRID_W), 1) < GRID_W
    neg = jnp.full((GRID_W, 2 * GRID_W), NEG_INF, F32)

    for p, (r0, kr_base) in enumerate(ATT_PATTERNS):
        for qr in range(ATT_QR):
            r = r0 + qr
            rstart = min(max(r - WIN_H // 2, 0), ROWS - WIN_H)
            for kp in range(ATT_KR // 2):
                halves = []
                for kr in (kr_base + 2 * kp, kr_base + 2 * kp + 1):
                    inside = rstart <= kr < rstart + WIN_H
                    halves.append(t_ref[0, kr - r + WIN_H - 1] if inside else None)
                a, b = halves
                if a is None and b is None:
                    blk = neg
                else:
                    blk = jnp.where(left, neg if a is None else a, neg if b is None else b)
                bias_ref[p, qr * GRID_W:(qr + 1) * GRID_W, kp * 2 * GRID_W:(kp + 1) * 2 * GRID_W] = blk

    kc = kc_ref[...]
    vc = vc_ref[...]

    def block(q0, k0, p):
        q = q_ref[pl.ds(q0, ATT_QB), :]
        k = k_ref[pl.ds(k0, ATT_KB), :]
        v = v_ref[pl.ds(k0, ATT_KB), :]
        s_win = _dot_nt(q, k) * scale + bias_ref[p]
        s_ctx = _dot_nt(q, kc) * scale
        m = jnp.maximum(jnp.max(s_win, axis=-1, keepdims=True), jnp.max(s_ctx, axis=-1, keepdims=True))
        p_win = jnp.exp(s_win - m)
        p_ctx = jnp.exp(s_ctx - m)
        denom = jnp.sum(p_win, axis=-1, keepdims=True) + jnp.sum(p_ctx, axis=-1, keepdims=True)
        o = _dot(p_win.astype(BF16), v) + _dot(p_ctx.astype(BF16), vc)
        o_ref[pl.ds(q0, ATT_QB), :] = (o / denom).astype(BF16)

    block(0, 0, 0)
    for rb in range(1, ROWS // ATT_QR - 1):
        block(rb * ATT_QB, rb * ATT_QB - (WIN_H // 2) * GRID_W, 1)
    block(SEQ - ATT_QB, SEQ - ATT_KB, 2)

    s = _dot_nt(qc_ref[...], kc) * scale
    pc = jnp.exp(s - jnp.max(s, axis=-1, keepdims=True))
    oc = _dot(pc.astype(BF16), vc) / jnp.sum(pc, axis=-1, keepdims=True)
    oc_ref[...] = oc.astype(BF16)


def _attention(qkv, bias_tab):
    lat_blk = (SEQ, HEAD_DIM)
    ctx_blk = (CTX, HEAD_DIM)
    ctx0 = N_LAT // CTX
    return pl.pallas_call(
        _attn_kernel,
        grid=(N_HEADS, BATCH),
        in_specs=[
            pl.BlockSpec(lat_blk, lambda h, b: (b, h)),
            pl.BlockSpec(lat_blk, lambda h, b: (b, N_HEADS + h)),
            pl.BlockSpec(lat_blk, lambda h, b: (b, 2 * N_HEADS + h)),
            pl.BlockSpec(ctx_blk, lambda h, b: (ctx0 + b, h)),
            pl.BlockSpec(ctx_blk, lambda h, b: (ctx0 + b, N_HEADS + h)),
            pl.BlockSpec(ctx_blk, lambda h, b: (ctx0 + b, 2 * N_HEADS + h)),
            pl.BlockSpec((1, N_DR, GRID_W, 2 * GRID_W), lambda h, b: (h, 0, 0, 0)),
        ],
        out_specs=[
            pl.BlockSpec(lat_blk, lambda h, b: (b, h)),
            pl.BlockSpec(ctx_blk, lambda h, b: (b, h)),
        ],
        out_shape=[
            jax.ShapeDtypeStruct((N_LAT, D), BF16),
            jax.ShapeDtypeStruct((N_CTX, D), BF16),
        ],
        scratch_shapes=[pltpu.VMEM((len(ATT_PATTERNS), ATT_QB, ATT_KB), F32)],
        compiler_params=_params("parallel", "parallel"),
        name="nat_attention",
    )(qkv, qkv, qkv, qkv, qkv, qkv, bias_tab)


def _attn_bias_table(rpb):
    qcol = np.arange(GRID_W)[:, None]
    kcol = np.arange(2 * GRID_W)[None, :] % GRID_W
    cstart = np.clip(qcol - WIN_W // 2, 0, GRID_W - WIN_W)
    col_valid = (kcol >= cstart) & (kcol < cstart + WIN_W)
    dc_idx = np.clip(kcol - qcol, 1 - WIN_W, WIN_W - 1) + (WIN_W - 1)
    onehot = (dc_idx[None] == np.arange(2 * WIN_W - 1)[:, None, None]) & col_valid[None]
    onehot = jnp.asarray(onehot.reshape(2 * WIN_W - 1, -1), F32)
    mask = jnp.asarray(np.where(col_valid, 0.0, NEG_INF).reshape(-1), F32)
    t = jnp.dot(rpb.reshape(N_HEADS * N_DR, 2 * WIN_W - 1), onehot, precision=HIGHEST) + mask
    return t.reshape(N_HEADS, N_DR, GRID_W, 2 * GRID_W)


def _proj_kernel(xn_ref, w_ref, o_ref):
    xn = xn_ref[...]
    for j in range(D // TN):
        o_ref[:, j * TN:(j + 1) * TN] = _dot(xn, w_ref[:, j * TN:(j + 1) * TN])


def _proj(xn, w, rows):
    return pl.pallas_call(
        _proj_kernel,
        grid=(rows // TM,),
        in_specs=[
            pl.BlockSpec((TM, D), lambda i: (i, 0)),
            _resident((D, D)),
        ],
        out_specs=pl.BlockSpec((TM, D), lambda i: (i, 0)),
        out_shape=jax.ShapeDtypeStruct((rows, D), F32),
        compiler_params=_params("arbitrary"),
        name="s5_in_proj",
    )(xn, w)


S5_CL = SEQ // S5_Q
S5_CC = CTX // S5_Q
S5_HALF = LANES // C_GROUP


def _s5_kernel(ul_ref, uc_ref, mi_ref, ms_ref, mo_ref, aq_ref, yl_ref, yc_ref,
               xl_scr, xc_scr, s_scr, ssw_scr, hp_scr, yl_acc, yc_acc):
    d = pl.program_id(2)
    ns = 2 * C_STATE

    def granule_transpose(rows):
        n = rows[0].shape[0]
        granule = lax.broadcasted_iota(jnp.int32, (n, LANES), 1) // C_GROUP
        rows = list(rows)
        k = S5_GB // 2
        while k:
            hi = (granule & k) != 0
            for i in range(S5_GB):
                if i & k:
                    continue
                a, b = rows[i], rows[i + k]
                rows[i] = jnp.where(hi, pltpu.roll(b, k * C_GROUP, 1), a)
                rows[i + k] = jnp.where(hi, b, pltpu.roll(a, LANES - k * C_GROUP, 1))
            k //= 2
        return rows

    def to_chunks(u_ref, x_scr, nch):
        for hf in range(S5_Q // S5_HALF):
            by_token = [u_ref[pl.ds(hf * S5_HALF + tl, nch, stride=S5_Q), :] for tl in range(S5_HALF)]
            for gl, x in enumerate(granule_transpose(by_token)):
                x_scr[gl, :, hf * LANES:(hf + 1) * LANES] = x.astype(BF16)

    def from_chunks(y_acc, y_ref, nch):
        for hf in range(S5_Q // S5_HALF):
            by_group = [y_acc[gl, :, hf * LANES:(hf + 1) * LANES] for gl in range(S5_GB)]
            for tl, y in enumerate(granule_transpose(by_group)):
                y_ref[pl.ds(hf * S5_HALF + tl, nch, stride=S5_Q), :] = y

    @pl.when(d == 0)
    def _():
        to_chunks(ul_ref, xl_scr, S5_CL)
        to_chunks(uc_ref, xc_scr, S5_CC)
        yl_acc[...] = jnp.zeros_like(yl_acc)
        yc_acc[...] = jnp.zeros_like(yc_acc)

    lat0 = S5_CC * S5_GB
    for j in range(S5_GB):
        xl, xc = xl_scr[j], xc_scr[j]
        yl_acc[j] += _dot(xl, mi_ref[0, j])
        yc_acc[j] += _dot(xc, mi_ref[0, j])
        sl = _dot(xl, ms_ref[0, j])
        sc = _dot(xc, ms_ref[0, j])
        s_scr[pl.ds(j, S5_CC, stride=S5_GB), :] = sc[:, :ns]
        s_scr[pl.ds(lat0 + j, S5_CL, stride=S5_GB), :] = sl[:, :ns]
        ssw_scr[pl.ds(j, S5_CC, stride=S5_GB), :] = sc[:, ns:]
        ssw_scr[pl.ds(lat0 + j, S5_CL, stride=S5_GB), :] = sl[:, ns:]

    a1 = aq_ref[0, 0]
    a2 = aq_ref[0, 1]
    a3 = aq_ref[0, 2]

    def step(pos, carry):
        hs, hsw = carry
        row = pl.multiple_of(pos * S5_GB, S5_GB)
        hp_scr[pl.ds(row, S5_GB), :] = hs
        s = s_scr[pl.ds(row, S5_GB), :]
        ssw = ssw_scr[pl.ds(row, S5_GB), :]
        return hs * a1 + hsw * a2 + s, hsw * a1 + hs * a3 + ssw

    def ctx_step(i, carry):
        return step(jnp.where(d == 0, i, S5_CC - 1 - i), carry)

    def lat_step(i, carry):
        return step(S5_CC + jnp.where(d == 0, i, S5_CL - 1 - i), carry)

    zero = jnp.zeros((S5_GB, ns), F32)
    carry = lax.fori_loop(0, S5_CC, ctx_step, (zero, zero))
    lax.fori_loop(0, S5_CL, lat_step, carry, unroll=4)

    for j in range(S5_GB):
        mo = mo_ref[0, j]
        hc = hp_scr[pl.ds(j, S5_CC, stride=S5_GB), :]
        hl = hp_scr[pl.ds(lat0 + j, S5_CL, stride=S5_GB), :]
        yc_acc[j] += _dot(hc.astype(BF16), mo)
        yl_acc[j] += _dot(hl.astype(BF16), mo)

    @pl.when(d == 1)
    def _():
        from_chunks(yl_acc, yl_ref, S5_CL)
        from_chunks(yc_acc, yc_ref, S5_CC)


def _s5_scan(u, mi, ms, mo, aq):
    ns = 2 * C_STATE
    nrow = (S5_CC + S5_CL) * S5_GB
    ctx0 = N_LAT // CTX
    return pl.pallas_call(
        _s5_kernel,
        grid=(BATCH, C_GROUPS // S5_GB, 2),
        in_specs=[
            pl.BlockSpec((SEQ, LANES), lambda b, g, d: (b, g)),
            pl.BlockSpec((CTX, LANES), lambda b, g, d: (ctx0 + b, g)),
            pl.BlockSpec((1, S5_GB, S5_XW, S5_XW), lambda b, g, d: (d, g, 0, 0)),
            pl.BlockSpec((1, S5_GB, S5_XW, 2 * ns), lambda b, g, d: (d, g, 0, 0)),
            pl.BlockSpec((1, S5_GB, ns, S5_XW), lambda b, g, d: (d, g, 0, 0)),
            pl.BlockSpec((1, 3, S5_GB, ns), lambda b, g, d: (d, 0, g, 0)),
        ],
        out_specs=[
            pl.BlockSpec((SEQ, LANES), lambda b, g, d: (b, g)),
            pl.BlockSpec((CTX, LANES), lambda b, g, d: (b, g)),
        ],
        out_shape=[
            jax.ShapeDtypeStruct((N_LAT, D), F32),
            jax.ShapeDtypeStruct((N_CTX, D), F32),
        ],
        scratch_shapes=[
            pltpu.VMEM((S5_GB, S5_CL, S5_XW), BF16),
            pltpu.VMEM((S5_GB, S5_CC, S5_XW), BF16),
            pltpu.VMEM((nrow, ns), F32),
            pltpu.VMEM((nrow, ns), F32),
            pltpu.VMEM((nrow, ns), F32),
            pltpu.VMEM((S5_GB, S5_CL, S5_XW), F32),
            pltpu.VMEM((S5_GB, S5_CC, S5_XW), F32),
        ],
        compiler_params=_params("arbitrary", "arbitrary", "arbitrary"),
        name="s5_scan",
    )(u, u, mi, ms, mo, aq)


def _lag_table_kernel(b_ref, c_ref, o_ref):
    for g in range(S5_GB):
        o_ref[0, g] = jnp.dot(b_ref[0, g], c_ref[0, g], preferred_element_type=F32, precision=HIGHEST)


def _lag_table(bt, ck):
    ns = 2 * C_STATE
    return pl.pallas_call(
        _lag_table_kernel,
        grid=(2, C_GROUPS // S5_GB),
        in_specs=[
            pl.BlockSpec((1, S5_GB, C_GROUP, ns), lambda d, g: (d, g, 0, 0)),
            pl.BlockSpec((1, S5_GB, ns, S5_XW), lambda d, g: (d, g, 0, 0)),
        ],
        out_specs=pl.BlockSpec((1, S5_GB, C_GROUP, S5_XW), lambda d, g: (d, g, 0, 0)),
        out_shape=jax.ShapeDtypeStruct((2, C_GROUPS, C_GROUP, S5_XW), F32),
        compiler_params=_params("parallel", "parallel"),
        name="s5_lag_table",
    )(bt, ck)


def _s5_weights(a_re, a_im, log_dt, b_re, b_im, c_re, c_im):
    q = S5_Q
    a_re, a_im, b_re, b_im, c_re, c_im = (v.astype(F32) for v in (a_re, a_im, b_re, b_im, c_re, c_im))
    dt = jnp.exp(log_dt.astype(F32))[..., None]
    zr, zi = a_re * dt, a_im * dt
    ab_r, ab_i = jnp.exp(zr) * jnp.cos(zi), jnp.exp(zr) * jnp.sin(zi)
    den = a_re * a_re + a_im * a_im
    f_r = ((ab_r - 1.0) * a_re + ab_i * a_im) / den
    f_i = (ab_i * a_re - (ab_r - 1.0) * a_im) / den
    bb_r = f_r[..., None] * b_re - f_i[..., None] * b_im
    bb_i = f_r[..., None] * b_im + f_i[..., None] * b_re

    taus = jnp.arange(q + 1, dtype=F32)[:, None]
    mag = jnp.exp(zr[:, :, None] * taus)
    pw_r, pw_i = mag * jnp.cos(zi[:, :, None] * taus), mag * jnp.sin(zi[:, :, None] * taus)

    col = np.arange(S5_XW)
    e_t = jnp.asarray(col[None, :] // C_GROUP == np.arange(q)[:, None], F32)
    e_o = jnp.asarray(col[None, :] % C_GROUP == np.arange(C_GROUP)[:, None], F32)
    xc_r = jnp.einsum("dgop,ox->dgpx", c_re, e_o, precision=HIGHEST)
    xc_i = jnp.einsum("dgop,ox->dgpx", c_im, e_o, precision=HIGHEST)

    def readout(p_r, p_i):
        xp_r = jnp.einsum("dgtp,tx->dgpx", p_r, e_t, precision=HIGHEST)
        xp_i = jnp.einsum("dgtp,tx->dgpx", p_i, e_t, precision=HIGHEST)
        return xc_r * xp_r - xc_i * xp_i, xc_r * xp_i + xc_i * xp_r

    wo_r, wo_i = readout(jnp.stack([pw_r[0, :, 1:q + 1], pw_r[1, :, q:0:-1]]),
                         jnp.stack([pw_i[0, :, 1:q + 1], pw_i[1, :, q:0:-1]]))
    m_out = jnp.concatenate([wo_r, -wo_i], axis=2).astype(BF16)

    ck_r, ck_i = readout(jnp.stack([pw_r[0, :, :q], pw_r[1, :, q - 1::-1]]),
                         jnp.stack([pw_i[0, :, :q], pw_i[1, :, q - 1::-1]]))
    bt_r, bt_i = bb_r.transpose(0, 1, 3, 2), bb_i.transpose(0, 1, 3, 2)
    kmat = _lag_table(jnp.concatenate([bt_r, -bt_i], axis=-1),
                      jnp.concatenate([ck_r, ck_i], axis=2))
    zpad = jnp.zeros_like(kmat[0, :, :, :(q - 1) * C_GROUP])
    kflat = jnp.stack([jnp.concatenate([zpad, kmat[0]], axis=-1), jnp.concatenate([kmat[1], zpad], axis=-1)])
    m_intra = jnp.stack([kflat[..., (q - 1 - s) * C_GROUP:(q - 1 - s) * C_GROUP + S5_XW] for s in range(q)],
                        axis=2).reshape(2, C_GROUPS, S5_XW, S5_XW).astype(BF16)

    ps_r = jnp.stack([pw_r[0, :, q - 1::-1], pw_r[1, :, :q]])
    ps_i = jnp.stack([pw_i[0, :, q - 1::-1], pw_i[1, :, :q]])
    p4_r = jnp.concatenate([ps_r] * 4, axis=-1)[:, :, :, None]
    p4_i = jnp.concatenate([ps_i] * 4, axis=-1)[:, :, :, None]
    b4_a = jnp.concatenate([bt_r, bt_i, bt_i, bt_r], axis=-1)[:, :, None]
    b4_b = jnp.concatenate([-bt_i, bt_r, bt_r, -bt_i], axis=-1)[:, :, None]
    m_state = (p4_r * b4_a + p4_i * b4_b).reshape(2, C_GROUPS, S5_XW, 4 * C_STATE).astype(BF16)

    ar, ai = pw_r[:, :, q], pw_i[:, :, q]
    aq3 = jnp.stack([jnp.concatenate([ar, ar], -1), jnp.concatenate([-ai, ai], -1),
                     jnp.concatenate([ai, -ai], -1)], axis=1)
    return m_intra, m_state, m_out, aq3


def _glu_kernel(u_ref, yl_ref, yc_ref, dsk_ref, w_ref, h_ref, mod_ref, gn_ref, modn_ref, o_ref, xn_ref,
                y_a, y_b, *, n, n_lat):
    def compute(y_buf):
        y = jnp.where(pl.program_id(0) >= n_lat, yc_ref[...], yl_ref[...])
        z = jax.nn.gelu(dsk_ref[...] * u_ref[...] + y).astype(BF16)
        gate = _mod_row(mod_ref, 5)
        for c in range(D // TN):
            cs = slice(c * TN, (c + 1) * TN)
            a = _dot(z, w_ref[:, cs])
            g = _dot(z, w_ref[:, D + c * TN:D + (c + 1) * TN])
            y_buf[:, cs] = h_ref[:, cs] + gate[:, cs] * (a * jax.nn.sigmoid(g))

    def emit(y_buf):
        _emit_tile(y_buf, o_ref, xn_ref, gn_ref, modn_ref, 2, TM_E)

    _lagged_steps(n, compute, emit, y_a, y_b)


def _glu(u, y_lat, y_ctx, d_skip, w_glu, h, mod, gains, rows):
    n = rows // TM_E
    cur, prev = _lag_specs(n, TM_E)
    n_lat = N_LAT // TM_E
    tile = pl.BlockSpec((TM_E, D), lambda j: (cur(j), 0))
    return pl.pallas_call(
        functools.partial(_glu_kernel, n=n, n_lat=n_lat),
        grid=(n + 1,),
        in_specs=[tile] + _split_specs(n_lat, TM_E, cur) + [
            pl.BlockSpec((1, D), lambda j: (0, 0)),
            _resident((D, 2 * D)),
            tile,
            pl.BlockSpec((1, N_MOD, D), lambda j: (_seg(cur(j), TM_E), 0, 0)),
            pl.BlockSpec((3, D), lambda j: (0, 0)),
            pl.BlockSpec((1, N_MOD, D), lambda j: (_seg(prev(j), TM_E), 0, 0)),
        ],
        out_specs=[pl.BlockSpec((TM_E, D), lambda j: (prev(j), 0))] * 2,
        out_shape=[jax.ShapeDtypeStruct((rows, D), F32), jax.ShapeDtypeStruct((rows, D), BF16)],
        scratch_shapes=[pltpu.VMEM((TM_E, D), F32)] * 2,
        compiler_params=_params("arbitrary"),
        name="s5_glu",
    )(u, y_lat, y_ctx, d_skip, w_glu, h, mod, gains, mod)


def kernel(x, c, ctx, c_ctx, w_ada, b_ada, norm_g, ffn_w_gu, ffn_w_down, a_w_in, a_v_gain, a_w_s, a_b_s, a_w_out, b_w_qkv, b_q_gain, b_k_gain, b_rpb, b_w_out, c_w_in, c_a_re, c_a_im, c_log_dt, c_b_re, c_b_im, c_c_re, c_c_im, c_d, c_w_glu):
    h = jnp.concatenate([x.reshape(N_LAT, D), ctx.reshape(N_CTX, D)], axis=0).astype(F32)
    cond8 = jnp.concatenate([c, c_ctx[None], jnp.zeros((8 - BATCH - 1, D), c.dtype)], axis=0).astype(F32)
    mods = _adaln(cond8, w_ada.astype(F32), b_ada.astype(F32))
    norm_g = norm_g.astype(F32)
    ffn_w_gu = ffn_w_gu.astype(F32)
    ffn_w_down = ffn_w_down.astype(BF16)

    xn = _prenorm(h, mods[0], norm_g[0], 0, N_ALL)
    for i in range(DEPTH):
        kind, j = i % 3, i // 3
        last = i == DEPTH - 1
        rows = N_LAT if last else N_ALL
        mod, gains = mods[i], norm_g[i]

        act = _ffn_up(xn, ffn_w_gu, i, 0, rows)
        h, xn = _ffn_down_emit(act, ffn_w_down, h, mod, i, 0, 0, gains, mod, 1, rows)

        if kind == 0:
            z = _gmlp(xn, a_w_in[j].astype(BF16), a_v_gain[j].astype(F32)[None],
                      a_w_s[j].astype(BF16), a_b_s[j].astype(F32).T, rows)
            h, xn = _mm_res(z, None, a_w_out[j].astype(BF16), h, mod, gains, rows)
        elif kind == 1:
            head_gains = jnp.stack([b_q_gain[j], b_k_gain[j]]).astype(F32)
            qkv = _qkv(xn, b_w_qkv[j].astype(BF16), head_gains, rows)
            o_lat, o_ctx = _attention(qkv, _attn_bias_table(b_rpb[j].astype(F32)))
            h, xn = _mm_res(o_lat, o_ctx, b_w_out[j].astype(BF16), h, mod, gains, rows)
        else:
            u = _proj(xn, c_w_in[j].astype(BF16), rows)
            y_lat, y_ctx = _s5_scan(u, *_s5_weights(c_a_re[j], c_a_im[j], c_log_dt[j], c_b_re[j], c_b_im[j],
                                                    c_c_re[j], c_c_im[j]))
            h, xn = _glu(u, y_lat, y_ctx, c_d[j].astype(F32)[None], c_w_glu[j].astype(BF16), h, mod, gains, rows)

        act = _ffn_up(xn, ffn_w_gu, i, 1, rows)
        if last:
            h = _ffn_down(act, ffn_w_down, h, mod, i, 1, 2, rows)
        else:
            h, xn = _ffn_down_emit(act, ffn_w_down, h, mod, i, 1, 2, norm_g[i + 1], mods[i + 1], 0, rows)

    return h[:N_LAT].reshape(BATCH, SEQ, D).astype(x.dtype)
```

```python
import functools

import numpy as np
import jax
import jax.numpy as jnp
from jax import lax
from jax.experimental import pallas as pl
from jax.experimental.pallas import tpu as pltpu

F32 = jnp.float32
BF16 = jnp.bfloat16
HIGHEST = lax.Precision.HIGHEST

D = 2048
BATCH = 2
SEQ = 4096
CTX = 256
DEPTH = 4
N_LAT = BATCH * SEQ
N_CTX = BATCH * CTX
N_ALL = N_LAT + N_CTX
N_MOD = 9
D_FF = 5632
RMS_EPS = 1e-6
NEG_INF = -1e30
GRID_W = 64
ROWS = SEQ // GRID_W
CHUNK = 128
A_GROUPS = 16
N_HEADS = 16
HEAD_DIM = 128
WIN_H = 8
WIN_W = 16
C_GROUP = 16
C_GROUPS = D // C_GROUP
C_STATE = 64
LANES = 128
NORM_ROWS = 16
NORM_UNROLL = 4

TM = 512
TM_F = 1024
TF = 256
FF_SPLIT = 2
TM_E = 256
TN = 512
ADA_TN = 2048
S5_Q = 16
S5_GB = LANES // C_GROUP
S5_XW = S5_Q * C_GROUP
VMEM_LIMIT = 56 * 1024 * 1024


def _params(*sem):
    return pltpu.CompilerParams(dimension_semantics=sem, vmem_limit_bytes=VMEM_LIMIT)


def _seg(i, tm=TM):
    return jnp.minimum((i * tm) // SEQ, 2)


def _mod_row(mod_ref, r):
    return mod_ref[0, r:r + 1, :]


def _norm_mod_store(h_ref, xn_ref, m, g_ref, mod_ref, s):
    gs = g_ref[s:s + 1, :] * (1.0 + _mod_row(mod_ref, 3 * s + 1))
    shift = _mod_row(mod_ref, 3 * s)

    def chunk(r, carry):
        rows = pl.ds(pl.multiple_of(r * NORM_ROWS, NORM_ROWS), NORM_ROWS)
        x = h_ref[rows, :]
        ms = jnp.mean(x * x, axis=-1, keepdims=True)
        xn_ref[rows, :] = (x * lax.rsqrt(ms + RMS_EPS) * gs + shift).astype(BF16)
        return carry

    lax.fori_loop(0, m // NORM_ROWS, chunk, 0, unroll=NORM_UNROLL)


def _dot(a, b):
    return jnp.dot(a, b, preferred_element_type=F32)


def _dot_nt(a, b):
    return lax.dot_general(a, b, (((1,), (1,)), ((), ())), preferred_element_type=F32)


def _ada_kernel(c_ref, w_ref, b_ref, o_ref):
    c = c_ref[...]
    a = (c * jax.nn.sigmoid(c)).astype(BF16)
    o_ref[0] = _dot(a, w_ref[0].astype(BF16)) + b_ref[0]


def _adaln(cond8, w_ada, b_ada):
    n = N_MOD * D
    out = pl.pallas_call(
        _ada_kernel,
        grid=(DEPTH, n // ADA_TN),
        in_specs=[
            pl.BlockSpec((8, D), lambda l, j: (0, 0)),
            pl.BlockSpec((1, D, ADA_TN), lambda l, j: (l, 0, j)),
            pl.BlockSpec((1, 1, ADA_TN), lambda l, j: (l, 0, j)),
        ],
        out_specs=pl.BlockSpec((1, 8, ADA_TN), lambda l, j: (l, 0, j)),
        out_shape=jax.ShapeDtypeStruct((DEPTH, 8, n), F32),
        compiler_params=_params("parallel", "parallel"),
        name="adaln",
    )(cond8, w_ada, b_ada.reshape(DEPTH, 1, n))
    return out[:, :3].reshape(DEPTH, 3, N_MOD, D)


def _prenorm_kernel(hl_ref, hc_ref, mod_ref, g_ref, o_ref, *, s, n_lat):
    i = pl.program_id(0)

    @pl.when(i < n_lat)
    def _():
        _norm_mod_store(hl_ref, o_ref, TM, g_ref, mod_ref, s)

    @pl.when(i >= n_lat)
    def _():
        _norm_mod_store(hc_ref, o_ref, TM, g_ref, mod_ref, s)


def _prenorm(h_lat, h_ctx, mod, gains, s):
    n_lat = N_LAT // TM
    return pl.pallas_call(
        functools.partial(_prenorm_kernel, s=s, n_lat=n_lat),
        grid=(N_ALL // TM,),
        in_specs=_split_specs(n_lat, TM) + [
            pl.BlockSpec((1, N_MOD, D), lambda i: (_seg(i), 0, 0)),
            pl.BlockSpec((3, D), lambda i: (0, 0)),
        ],
        out_specs=pl.BlockSpec((TM, D), lambda i: (i, 0)),
        out_shape=jax.ShapeDtypeStruct((N_ALL, D), BF16),
        compiler_params=_params("parallel"),
        name="ffn_prenorm",
    )(h_lat, h_ctx, mod, gains)


def _ffn_up_kernel(xn_ref, wg_ref, wu_ref, wd_ref, o_ref, wdb_ref, *, subtiles):
    wdb_ref[...] = wd_ref[0, 0].astype(BF16)
    wg = wg_ref[0, 0].astype(BF16)
    wu = wu_ref[0, 0].astype(BF16)
    for r0, m in subtiles:
        xn = xn_ref[r0:r0 + m, :]
        g = _dot(xn, wg)
        u = _dot(xn, wu)
        o_ref[r0:r0 + m, :] = ((g * jax.nn.sigmoid(g)) * u).astype(BF16)


def _ffn_up(xn, w_gu, w_down, layer, half, rows):
    nk = D_FF // TF
    wd_blk = lambda r, k: jnp.where(r == 0, k, nk - 1)
    hr = rows // FF_SPLIT
    subtiles = [(r0, min(TM_F, hr - r0)) for r0 in range(0, hr, TM_F)]
    return pl.pallas_call(
        functools.partial(_ffn_up_kernel, subtiles=subtiles),
        grid=(FF_SPLIT, nk),
        in_specs=[
            pl.BlockSpec((hr, D), lambda r, k: (r, 0), pipeline_mode=pl.Buffered(1)),
            pl.BlockSpec((1, 1, D, TF), lambda r, k: (layer, half, 0, k)),
            pl.BlockSpec((1, 1, D, TF), lambda r, k: (layer, half, 0, nk + k)),
            pl.BlockSpec((1, 1, TF, D), lambda r, k: (layer, half, wd_blk(r, k), 0)),
        ],
        out_specs=[pl.BlockSpec((hr, TF), lambda r, k: (r, k)),
                   pl.BlockSpec((TF, D), lambda r, k: (wd_blk(r, k), 0))],
        out_shape=[jax.ShapeDtypeStruct((rows, D_FF), BF16), jax.ShapeDtypeStruct((D_FF, D), BF16)],
        compiler_params=_params("arbitrary", "arbitrary"),
        name="ffn_up",
    )(xn, w_gu, w_gu, w_down)


def _ffn_down_kernel(a_ref, wd_ref, h_ref, mod_ref, o_ref, *, s):
    a = a_ref[...]
    gate = 0.5 * _mod_row(mod_ref, 3 * s + 2)
    for j in range(D // TN):
        cs = slice(j * TN, (j + 1) * TN)
        o_ref[:, cs] = h_ref[:, cs] + gate[:, cs] * _dot(a, wd_ref[:, cs])


def _ffn_down(act, w_down, h, mod, s, rows):
    tile = pl.BlockSpec((TM, D), lambda i: (i, 0))
    return pl.pallas_call(
        functools.partial(_ffn_down_kernel, s=s),
        grid=(rows // TM,),
        in_specs=[
            pl.BlockSpec((TM, D_FF), lambda i: (i, 0)),
            _resident((D_FF, D)),
            tile,
            pl.BlockSpec((1, N_MOD, D), lambda i: (_seg(i), 0, 0)),
        ],
        out_specs=tile,
        out_shape=jax.ShapeDtypeStruct((rows, D), F32),
        compiler_params=_params("arbitrary"),
        name="ffn_down",
    )(act, w_down, h, mod)


def _emit_tile(y_buf, o_ref, xn_ref, gn_ref, modn_ref, sn, tm):
    gs = gn_ref[sn:sn + 1, :] * (1.0 + _mod_row(modn_ref, 3 * sn + 1))
    shift = _mod_row(modn_ref, 3 * sn)
    for r in range(tm // NORM_ROWS):
        rows = slice(r * NORM_ROWS, (r + 1) * NORM_ROWS)
        y = y_buf[rows, :]
        o_ref[rows, :] = y
        ms = jnp.mean(y * y, axis=-1, keepdims=True)
        xn_ref[rows, :] = (y * lax.rsqrt(ms + RMS_EPS) * gs + shift).astype(BF16)


def _lagged_steps(n, compute, emit, y_a, y_b):
    j = pl.program_id(0)
    last = y_a if (n - 1) % 2 == 0 else y_b

    @pl.when(j == 0)
    def _():
        compute(y_a)

    @pl.when((j > 0) & (j < n) & (j % 2 == 1))
    def _():
        emit(y_a)
        compute(y_b)

    @pl.when((j > 0) & (j < n) & (j % 2 == 0))
    def _():
        emit(y_b)
        compute(y_a)

    @pl.when(j == n)
    def _():
        emit(last)


def _lag_specs(n, tm):
    cur = lambda j: jnp.minimum(j, n - 1)
    prev = lambda j: jnp.maximum(j - 1, 0)
    return cur, prev


def _ffn_down_emit_kernel(*refs, s, sn, n, n_lat):
    if n_lat is None:
        a_ref, wd_ref, h_ref, mod_ref, gn_ref, modn_ref, o_ref, xn_ref, y_a, y_b = refs
    else:
        a_ref, wd_ref, h_ref, hc_ref, mod_ref, gn_ref, modn_ref, o_ref, xn_ref, y_a, y_b = refs

    def compute(y_buf):
        a = a_ref[...]
        gate = 0.5 * _mod_row(mod_ref, 3 * s + 2)
        for c in range(D // TN):
            cs = slice(c * TN, (c + 1) * TN)
            if n_lat is None:
                res = h_ref[:, cs]
            else:
                res = jnp.where(pl.program_id(0) >= n_lat, hc_ref[:, cs], h_ref[:, cs])
            y_buf[:, cs] = res + gate[:, cs] * _dot(a, wd_ref[:, cs])

    def emit(y_buf):
        _emit_tile(y_buf, o_ref, xn_ref, gn_ref, modn_ref, sn, TM_E)

    _lagged_steps(n, compute, emit, y_a, y_b)


def _ffn_down_emit(act, w_down, h, h_ctx, mod, s, gains_n, mod_n, sn, rows):
    n = rows // TM_E
    cur, prev = _lag_specs(n, TM_E)
    n_lat = None if h_ctx is None else N_LAT // TM_E
    if h_ctx is None:
        h_specs, hs = [pl.BlockSpec((TM_E, D), lambda j: (cur(j), 0))], (h,)
    else:
        h_specs, hs = _split_specs(n_lat, TM_E, cur), (h, h_ctx)
    return pl.pallas_call(
        functools.partial(_ffn_down_emit_kernel, s=s, sn=sn, n=n, n_lat=n_lat),
        grid=(n + 1,),
        in_specs=[
            pl.BlockSpec((TM_E, D_FF), lambda j: (cur(j), 0)),
            _resident((D_FF, D)),
        ] + h_specs + [
            pl.BlockSpec((1, N_MOD, D), lambda j: (_seg(cur(j), TM_E), 0, 0)),
            pl.BlockSpec((3, D), lambda j: (0, 0)),
            pl.BlockSpec((1, N_MOD, D), lambda j: (_seg(prev(j), TM_E), 0, 0)),
        ],
        out_specs=[pl.BlockSpec((TM_E, D), lambda j: (prev(j), 0))] * 2,
        out_shape=[jax.ShapeDtypeStruct((rows, D), F32), jax.ShapeDtypeStruct((rows, D), BF16)],
        scratch_shapes=[pltpu.VMEM((TM_E, D), F32)] * 2,
        compiler_params=_params("arbitrary"),
        name="ffn_down_emit",
    )(act, w_down, *hs, mod, gains_n, mod_n)


def _mm_res_kernel(*refs, n, n_lat):
    if n_lat is None:
        z_ref, w_ref, h_ref, mod_ref, gn_ref, modn_ref, o_ref, xn_ref, y_a, y_b = refs
    else:
        zl_ref, zc_ref, w_ref, h_ref, mod_ref, gn_ref, modn_ref, o_ref, xn_ref, y_a, y_b = refs

    def compute(y_buf):
        if n_lat is None:
            z = z_ref[...]
        else:
            z = jnp.where(pl.program_id(0) >= n_lat, zc_ref[...], zl_ref[...])
        gate = _mod_row(mod_ref, 5)
        for c in range(D // TN):
            cs = slice(c * TN, (c + 1) * TN)
            y_buf[:, cs] = h_ref[:, cs] + gate[:, cs] * _dot(z, w_ref[:, cs])

    def emit(y_buf):
        _emit_tile(y_buf, o_ref, xn_ref, gn_ref, modn_ref, 2, TM)

    _lagged_steps(n, compute, emit, y_a, y_b)


def _split_specs(n_lat, tm, tile_of=lambda j: j):
    return [pl.BlockSpec((tm, D), lambda j: (jnp.minimum(tile_of(j), n_lat - 1), 0)),
            pl.BlockSpec((tm, D), lambda j: (jnp.maximum(tile_of(j) - n_lat, 0), 0))]


def _mm_res(z, z_ctx, w, h, mod, gains, rows):
    n = rows // TM
    cur, prev = _lag_specs(n, TM)
    n_lat = None if z_ctx is None else N_LAT // TM
    if z_ctx is None:
        z_specs, zs = [pl.BlockSpec((TM, D), lambda j: (cur(j), 0))], (z,)
    else:
        z_specs, zs = _split_specs(n_lat, TM, cur), (z, z_ctx)
    return pl.pallas_call(
        functools.partial(_mm_res_kernel, n=n, n_lat=n_lat),
        grid=(n + 1,),
        in_specs=z_specs + [
            _resident((D, D)),
            pl.BlockSpec((TM, D), lambda j: (cur(j), 0)),
            pl.BlockSpec((1, N_MOD, D), lambda j: (_seg(cur(j), TM), 0, 0)),
            pl.BlockSpec((3, D), lambda j: (0, 0)),
            pl.BlockSpec((1, N_MOD, D), lambda j: (_seg(prev(j), TM), 0, 0)),
        ],
        out_specs=[pl.BlockSpec((TM, D), lambda j: (prev(j), 0))] * 2,
        out_shape=[jax.ShapeDtypeStruct((rows, D), F32), jax.ShapeDtypeStruct((rows, D), BF16)],
        scratch_shapes=[pltpu.VMEM((TM, D), F32)] * 2,
        compiler_params=_params("arbitrary"),
        name="mm_res",
    )(*zs, w, h, mod, gains, mod)


def _gmlp_kernel(xn_ref, win_ref, vg_ref, ws_ref, bst_ref, z_ref, y_ref, *, nj):
    xn = xn_ref[...]
    for j in range(nj):
        y_ref[j] = jax.nn.gelu(_dot(xn, win_ref[:, j * TN:(j + 1) * TN]))

    half = nj // 2
    ssq = jnp.zeros((TM, 1), F32)
    for jj in range(half, nj):
        yv = y_ref[jj]
        ssq = ssq + jnp.sum(yv * yv, axis=-1, keepdims=True)
    inv = lax.rsqrt(ssq / D + RMS_EPS)
    gpb = TN // CHUNK
    for cb in range(half):
        v = (y_ref[half + cb] * inv * vg_ref[:, cb * TN:(cb + 1) * TN]).astype(BF16)
        u = y_ref[cb]
        for gg in range(gpb):
            grp = cb * gpb + gg
            cs = slice(gg * CHUNK, (gg + 1) * CHUNK)
            for c in range(TM // CHUNK):
                rs = slice(c * CHUNK, (c + 1) * CHUNK)
                sg = _dot(ws_ref[grp], v[rs, cs]) + bst_ref[:, grp:grp + 1]
                z_ref[rs, grp * CHUNK:(grp + 1) * CHUNK] = (u[rs, cs] * sg).astype(BF16)


def _resident(shape):
    return pl.BlockSpec(shape, lambda *_: (0,) * len(shape), pipeline_mode=pl.Buffered(1))


def _gmlp(xn, w_in, v_gain, w_s, b_s_t, rows):
    nj = (2 * D) // TN
    return pl.pallas_call(
        functools.partial(_gmlp_kernel, nj=nj),
        grid=(rows // TM,),
        in_specs=[
            pl.BlockSpec((TM, D), lambda i: (i, 0)),
            _resident((D, 2 * D)),
            pl.BlockSpec((1, D), lambda i: (0, 0)),
            _resident((A_GROUPS, CHUNK, CHUNK)),
            pl.BlockSpec((CHUNK, A_GROUPS), lambda i: (0, 0)),
        ],
        out_specs=pl.BlockSpec((TM, D), lambda i: (i, 0)),
        out_shape=jax.ShapeDtypeStruct((rows, D), BF16),
        scratch_shapes=[pltpu.VMEM((nj, TM, TN), F32)],
        compiler_params=_params("arbitrary"),
        name="gmlp",
    )(xn, w_in, v_gain, w_s, b_s_t)


def _qkv_kernel(xn_ref, w_ref, hg_ref, o_ref):
    xn = xn_ref[...]
    for j in range((3 * D) // TN):
        y = _dot(xn, w_ref[:, j * TN:(j + 1) * TN])
        part = j // (D // TN)
        if part == 2:
            o_ref[:, j * TN:(j + 1) * TN] = y.astype(BF16)
            continue
        gain = hg_ref[part:part + 1, :]
        for hh in range(TN // HEAD_DIM):
            yh = y[:, hh * HEAD_DIM:(hh + 1) * HEAD_DIM]
            ms = jnp.mean(yh * yh, axis=-1, keepdims=True)
            c0 = j * TN + hh * HEAD_DIM
            o_ref[:, c0:c0 + HEAD_DIM] = (yh * lax.rsqrt(ms + RMS_EPS) * gain).astype(BF16)


def _qkv(xn, w_qkv, head_gains, rows):
    return pl.pallas_call(
        _qkv_kernel,
        grid=(rows // TM,),
        in_specs=[
            pl.BlockSpec((TM, D), lambda i: (i, 0)),
            _resident((D, 3 * D)),
            pl.BlockSpec((2, HEAD_DIM), lambda i: (0, 0)),
        ],
        out_specs=pl.BlockSpec((TM, 3 * D), lambda i: (i, 0)),
        out_shape=jax.ShapeDtypeStruct((rows, 3 * D), BF16),
        compiler_params=_params("arbitrary"),
        name="qkv",
    )(xn, w_qkv, head_gains)


ATT_QR = 8
ATT_KR = ATT_QR + WIN_H
ATT_QB = ATT_QR * GRID_W
ATT_KB = ATT_KR * GRID_W
N_DR = 2 * WIN_H - 1
ATT_PATTERNS = ((0, 0), (ATT_QR, ATT_QR - WIN_H // 2), (ROWS - ATT_QR, ROWS - ATT_KR))


def _attn_kernel(q_ref, k_ref, v_ref, qc_ref, kc_ref, vc_ref, t_ref, o_ref, oc_ref, bias_ref):
    scale = HEAD_DIM ** -0.5
    left = lax.broadcasted_iota(jnp.int32, (GRID_W, 2 * GRID_W), 1) < GRID_W
    neg = jnp.full((GRID_W, 2 * GRID_W), NEG_INF, F32)

    for p, (r0, kr_base) in enumerate(ATT_PATTERNS):
        for qr in range(ATT_QR):
            r = r0 + qr
            rstart = min(max(r - WIN_H // 2, 0), ROWS - WIN_H)
            for kp in range(ATT_KR // 2):
                halves = []
                for kr in (kr_base + 2 * kp, kr_base + 2 * kp + 1):
                    inside = rstart <= kr < rstart + WIN_H
                    halves.append(t_ref[0, kr - r + WIN_H - 1] if inside else None)
                a, b = halves
                if a is None and b is None:
                    blk = neg
                else:
                    blk = jnp.where(left, neg if a is None else a, neg if b is None else b)
                bias_ref[p, qr * GRID_W:(qr + 1) * GRID_W, kp * 2 * GRID_W:(kp + 1) * 2 * GRID_W] = blk

    kc = kc_ref[...]
    vc = vc_ref[...]

    def block(q0, k0, p):
        q = q_ref[pl.ds(q0, ATT_QB), :]
        k = k_ref[pl.ds(k0, ATT_KB), :]
        v = v_ref[pl.ds(k0, ATT_KB), :]
        s_win = _dot_nt(q, k) * scale + bias_ref[p]
        s_ctx = _dot_nt(q, kc) * scale
        m = jnp.maximum(jnp.max(s_win, axis=-1, keepdims=True), jnp.max(s_ctx, axis=-1, keepdims=True))
        p_win = jnp.exp(s_win - m)
        p_ctx = jnp.exp(s_ctx - m)
        denom = jnp.sum(p_win, axis=-1, keepdims=True) + jnp.sum(p_ctx, axis=-1, keepdims=True)
        o = _dot(p_win.astype(BF16), v) + _dot(p_ctx.astype(BF16), vc)
        o_ref[pl.ds(q0, ATT_QB), :] = (o / denom).astype(BF16)

    block(0, 0, 0)
    for rb in range(1, ROWS // ATT_QR - 1):
        block(rb * ATT_QB, rb * ATT_QB - (WIN_H // 2) * GRID_W, 1)
    block(SEQ - ATT_QB, SEQ - ATT_KB, 2)

    s = _dot_nt(qc_ref[...], kc) * scale
    pc = jnp.exp(s - jnp.max(s, axis=-1, keepdims=True))
    oc = _dot(pc.astype(BF16), vc) / jnp.sum(pc, axis=-1, keepdims=True)
    oc_ref[...] = oc.astype(BF16)


def _attention(qkv, bias_tab):
    lat_blk = (SEQ, HEAD_DIM)
    ctx_blk = (CTX, HEAD_DIM)
    ctx0 = N_LAT // CTX
    return pl.pallas_call(
        _attn_kernel,
        grid=(N_HEADS, BATCH),
        in_specs=[
            pl.BlockSpec(lat_blk, lambda h, b: (b, h)),
            pl.BlockSpec(lat_blk, lambda h, b: (b, N_HEADS + h)),
            pl.BlockSpec(lat_blk, lambda h, b: (b, 2 * N_HEADS + h)),
            pl.BlockSpec(ctx_blk, lambda h, b: (ctx0 + b, h)),
            pl.BlockSpec(ctx_blk, lambda h, b: (ctx0 + b, N_HEADS + h)),
            pl.BlockSpec(ctx_blk, lambda h, b: (ctx0 + b, 2 * N_HEADS + h)),
            pl.BlockSpec((1, N_DR, GRID_W, 2 * GRID_W), lambda h, b: (h, 0, 0, 0)),
        ],
        out_specs=[
            pl.BlockSpec(lat_blk, lambda h, b: (b, h)),
            pl.BlockSpec(ctx_blk, lambda h, b: (b, h)),
        ],
        out_shape=[
            jax.ShapeDtypeStruct((N_LAT, D), BF16),
            jax.ShapeDtypeStruct((N_CTX, D), BF16),
        ],
        scratch_shapes=[pltpu.VMEM((len(ATT_PATTERNS), ATT_QB, ATT_KB), F32)],
        compiler_params=_params("parallel", "parallel"),
        name="nat_attention",
    )(qkv, qkv, qkv, qkv, qkv, qkv, bias_tab)


def _attn_bias_table(rpb):
    qcol = np.arange(GRID_W)[:, None]
    kcol = np.arange(2 * GRID_W)[None, :] % GRID_W
    cstart = np.clip(qcol - WIN_W // 2, 0, GRID_W - WIN_W)
    col_valid = (kcol >= cstart) & (kcol < cstart + WIN_W)
    dc_idx = np.clip(kcol - qcol, 1 - WIN_W, WIN_W - 1) + (WIN_W - 1)
    onehot = (dc_idx[None] == np.arange(2 * WIN_W - 1)[:, None, None]) & col_valid[None]
    onehot = jnp.asarray(onehot.reshape(2 * WIN_W - 1, -1), F32)
    mask = jnp.asarray(np.where(col_valid, 0.0, NEG_INF).reshape(-1), F32)
    t = jnp.dot(rpb.reshape(N_HEADS * N_DR, 2 * WIN_W - 1), onehot, precision=HIGHEST) + mask
    return t.reshape(N_HEADS, N_DR, GRID_W, 2 * GRID_W)


def _proj_kernel(xn_ref, w_ref, o_ref):
    xn = xn_ref[...]
    for j in range(D // TN):
        o_ref[:, j * TN:(j + 1) * TN] = _dot(xn, w_ref[:, j * TN:(j + 1) * TN])


def _proj(xn, w, rows):
    return pl.pallas_call(
        _proj_kernel,
        grid=(rows // TM,),
        in_specs=[
            pl.BlockSpec((TM, D), lambda i: (i, 0)),
            _resident((D, D)),
        ],
        out_specs=pl.BlockSpec((TM, D), lambda i: (i, 0)),
        out_shape=jax.ShapeDtypeStruct((rows, D), F32),
        compiler_params=_params("arbitrary"),
        name="s5_in_proj",
    )(xn, w)


S5_CL = SEQ // S5_Q
S5_CC = CTX // S5_Q
S5_HALF = LANES // C_GROUP


def _s5_kernel(ul_ref, uc_ref, mi_ref, ms_ref, mo_ref, aq_ref, yl_ref, yc_ref,
               xl_scr, xc_scr, s_scr, ssw_scr, hp_scr, yl_acc, yc_acc):
    d = pl.program_id(2)
    ns = 2 * C_STATE

    def granule_transpose(rows):
        n = rows[0].shape[0]
        granule = lax.broadcasted_iota(jnp.int32, (n, LANES), 1) // C_GROUP
        rows = list(rows)
        k = S5_GB // 2
        while k:
            hi = (granule & k) != 0
            for i in range(S5_GB):
                if i & k:
                    continue
                a, b = rows[i], rows[i + k]
                rows[i] = jnp.where(hi, pltpu.roll(b, k * C_GROUP, 1), a)
                rows[i + k] = jnp.where(hi, b, pltpu.roll(a, LANES - k * C_GROUP, 1))
            k //= 2
        return rows

    def to_chunks(u_ref, x_scr, nch):
        for hf in range(S5_Q // S5_HALF):
            by_token = [u_ref[pl.ds(hf * S5_HALF + tl, nch, stride=S5_Q), :] for tl in range(S5_HALF)]
            for gl, x in enumerate(granule_transpose(by_token)):
                x_scr[gl, :, hf * LANES:(hf + 1) * LANES] = x.astype(BF16)

    def from_chunks(y_acc, y_ref, nch):
        for hf in range(S5_Q // S5_HALF):
            by_group = [y_acc[gl, :, hf * LANES:(hf + 1) * LANES] for gl in range(S5_GB)]
            for tl, y in enumerate(granule_transpose(by_group)):
                y_ref[pl.ds(hf * S5_HALF + tl, nch, stride=S5_Q), :] = y

    @pl.when(d == 0)
    def _():
        to_chunks(ul_ref, xl_scr, S5_CL)
        to_chunks(uc_ref, xc_scr, S5_CC)
        yl_acc[...] = jnp.zeros_like(yl_acc)
        yc_acc[...] = jnp.zeros_like(yc_acc)

    lat0 = S5_CC * S5_GB
    for j in range(S5_GB):
        xl, xc = xl_scr[j], xc_scr[j]
        yl_acc[j] += _dot(xl, mi_ref[0, j])
        yc_acc[j] += _dot(xc, mi_ref[0, j])
        sl = _dot(xl, ms_ref[0, j])
        sc = _dot(xc, ms_ref[0, j])
        s_scr[pl.ds(j, S5_CC, stride=S5_GB), :] = sc[:, :ns]
        s_scr[pl.ds(lat0 + j, S5_CL, stride=S5_GB), :] = sl[:, :ns]
        ssw_scr[pl.ds(j, S5_CC, stride=S5_GB), :] = sc[:, ns:]
        ssw_scr[pl.ds(lat0 + j, S5_CL, stride=S5_GB), :] = sl[:, ns:]

    a1 = aq_ref[0, 0]
    a2 = aq_ref[0, 1]
    a3 = aq_ref[0, 2]

    def step(pos, carry):
        hs, hsw = carry
        row = pl.multiple_of(pos * S5_GB, S5_GB)
        hp_scr[pl.ds(row, S5_GB), :] = hs
        s = s_scr[pl.ds(row, S5_GB), :]
        ssw = ssw_scr[pl.ds(row, S5_GB), :]
        return hs * a1 + hsw * a2 + s, hsw * a1 + hs * a3 + ssw

    def ctx_step(i, carry):
        return step(jnp.where(d == 0, i, S5_CC - 1 - i), carry)

    def lat_step(i, carry):
        return step(S5_CC + jnp.where(d == 0, i, S5_CL - 1 - i), carry)

    zero = jnp.zeros((S5_GB, ns), F32)
    carry = lax.fori_loop(0, S5_CC, ctx_step, (zero, zero))
    lax.fori_loop(0, S5_CL, lat_step, carry, unroll=4)

    for j in range(S5_GB):
        mo = mo_ref[0, j]
        hc = hp_scr[pl.ds(j, S5_CC, stride=S5_GB), :]
        hl = hp_scr[pl.ds(lat0 + j, S5_CL, stride=S5_GB), :]
        yc_acc[j] += _dot(hc.astype(BF16), mo)
        yl_acc[j] += _dot(hl.astype(BF16), mo)

    @pl.when(d == 1)
    def _():
        from_chunks(yl_acc, yl_ref, S5_CL)
        from_chunks(yc_acc, yc_ref, S5_CC)


def _s5_scan(u, mi, ms, mo, aq):
    ns = 2 * C_STATE
    nrow = (S5_CC + S5_CL) * S5_GB
    ctx0 = N_LAT // CTX
    return pl.pallas_call(
        _s5_kernel,
        grid=(BATCH, C_GROUPS // S5_GB, 2),
        in_specs=[
            pl.BlockSpec((SEQ, LANES), lambda b, g, d: (b, g)),
            pl.BlockSpec((CTX, LANES), lambda b, g, d: (ctx0 + b, g)),
            pl.BlockSpec((1, S5_GB, S5_XW, S5_XW), lambda b, g, d: (d, g, 0, 0)),
            pl.BlockSpec((1, S5_GB, S5_XW, 2 * ns), lambda b, g, d: (d, g, 0, 0)),
            pl.BlockSpec((1, S5_GB, ns, S5_XW), lambda b, g, d: (d, g, 0, 0)),
            pl.BlockSpec((1, 3, S5_GB, ns), lambda b, g, d: (d, 0, g, 0)),
        ],
        out_specs=[
            pl.BlockSpec((SEQ, LANES), lambda b, g, d: (b, g)),
            pl.BlockSpec((CTX, LANES), lambda b, g, d: (b, g)),
        ],
        out_shape=[
            jax.ShapeDtypeStruct((N_LAT, D), F32),
            jax.ShapeDtypeStruct((N_CTX, D), F32),
        ],
        scratch_shapes=[
            pltpu.VMEM((S5_GB, S5_CL, S5_XW), BF16),
            pltpu.VMEM((S5_GB, S5_CC, S5_XW), BF16),
            pltpu.VMEM((nrow, ns), F32),
            pltpu.VMEM((nrow, ns), F32),
            pltpu.VMEM((nrow, ns), F32),
            pltpu.VMEM((S5_GB, S5_CL, S5_XW), F32),
            pltpu.VMEM((S5_GB, S5_CC, S5_XW), F32),
        ],
        compiler_params=_params("arbitrary", "arbitrary", "arbitrary"),
        name="s5_scan",
    )(u, u, mi, ms, mo, aq)


def _lag_table_kernel(b_ref, c_ref, o_ref):
    for g in range(S5_GB):
        o_ref[0, g] = jnp.dot(b_ref[0, g], c_ref[0, g], preferred_element_type=F32, precision=HIGHEST)


def _lag_table(bt, ck):
    ns = 2 * C_STATE
    return pl.pallas_call(
        _lag_table_kernel,
        grid=(2, C_GROUPS // S5_GB),
        in_specs=[
            pl.BlockSpec((1, S5_GB, C_GROUP, ns), lambda d, g: (d, g, 0, 0)),
            pl.BlockSpec((1, S5_GB, ns, S5_XW), lambda d, g: (d, g, 0, 0)),
        ],
        out_specs=pl.BlockSpec((1, S5_GB, C_GROUP, S5_XW), lambda d, g: (d, g, 0, 0)),
        out_shape=jax.ShapeDtypeStruct((2, C_GROUPS, C_GROUP, S5_XW), F32),
        compiler_params=_params("parallel", "parallel"),
        name="s5_lag_table",
    )(bt, ck)


def _s5_weights(a_re, a_im, log_dt, b_re, b_im, c_re, c_im):
    q = S5_Q
    a_re, a_im, b_re, b_im, c_re, c_im = (v.astype(F32) for v in (a_re, a_im, b_re, b_im, c_re, c_im))
    dt = jnp.exp(log_dt.astype(F32))[..., None]
    zr, zi = a_re * dt, a_im * dt
    ab_r, ab_i = jnp.exp(zr) * jnp.cos(zi), jnp.exp(zr) * jnp.sin(zi)
    den = a_re * a_re + a_im * a_im
    f_r = ((ab_r - 1.0) * a_re + ab_i * a_im) / den
    f_i = (ab_i * a_re - (ab_r - 1.0) * a_im) / den
    bb_r = f_r[..., None] * b_re - f_i[..., None] * b_im
    bb_i = f_r[..., None] * b_im + f_i[..., None] * b_re

    taus = jnp.arange(q + 1, dtype=F32)[:, None]
    mag = jnp.exp(zr[:, :, None] * taus)
    pw_r, pw_i = mag * jnp.cos(zi[:, :, None] * taus), mag * jnp.sin(zi[:, :, None] * taus)

    col = np.arange(S5_XW)
    e_t = jnp.asarray(col[None, :] // C_GROUP == np.arange(q)[:, None], F32)
    e_o = jnp.asarray(col[None, :] % C_GROUP == np.arange(C_GROUP)[:, None], F32)
    xc_r = jnp.einsum("dgop,ox->dgpx", c_re, e_o, precision=HIGHEST)
    xc_i = jnp.einsum("dgop,ox->dgpx", c_im, e_o, precision=HIGHEST)

    def readout(p_r, p_i):
        xp_r = jnp.einsum("dgtp,tx->dgpx", p_r, e_t, precision=HIGHEST)
        xp_i = jnp.einsum("dgtp,tx->dgpx", p_i, e_t, precision=HIGHEST)
        return xc_r * xp_r - xc_i * xp_i, xc_r * xp_i + xc_i * xp_r

    wo_r, wo_i = readout(jnp.stack([pw_r[0, :, 1:q + 1], pw_r[1, :, q:0:-1]]),
                         jnp.stack([pw_i[0, :, 1:q + 1], pw_i[1, :, q:0:-1]]))
    m_out = jnp.concatenate([wo_r, -wo_i], axis=2).astype(BF16)

    ck_r, ck_i = readout(jnp.stack([pw_r[0, :, :q], pw_r[1, :, q - 1::-1]]),
                         jnp.stack([pw_i[0, :, :q], pw_i[1, :, q - 1::-1]]))
    bt_r, bt_i = bb_r.transpose(0, 1, 3, 2), bb_i.transpose(0, 1, 3, 2)
    kmat = _lag_table(jnp.concatenate([bt_r, -bt_i], axis=-1),
                      jnp.concatenate([ck_r, ck_i], axis=2))
    zpad = jnp.zeros_like(kmat[0, :, :, :(q - 1) * C_GROUP])
    kflat = jnp.stack([jnp.concatenate([zpad, kmat[0]], axis=-1), jnp.concatenate([kmat[1], zpad], axis=-1)])
    m_intra = jnp.stack([kflat[..., (q - 1 - s) * C_GROUP:(q - 1 - s) * C_GROUP + S5_XW] for s in range(q)],
                        axis=2).reshape(2, C_GROUPS, S5_XW, S5_XW).astype(BF16)

    ps_r = jnp.stack([pw_r[0, :, q - 1::-1], pw_r[1, :, :q]])
    ps_i = jnp.stack([pw_i[0, :, q - 1::-1], pw_i[1, :, :q]])
    p4_r = jnp.concatenate([ps_r] * 4, axis=-1)[:, :, :, None]
    p4_i = jnp.concatenate([ps_i] * 4, axis=-1)[:, :, :, None]
    b4_a = jnp.concatenate([bt_r, bt_i, bt_i, bt_r], axis=-1)[:, :, None]
    b4_b = jnp.concatenate([-bt_i, bt_r, bt_r, -bt_i], axis=-1)[:, :, None]
    m_state = (p4_r * b4_a + p4_i * b4_b).reshape(2, C_GROUPS, S5_XW, 4 * C_STATE).astype(BF16)

    ar, ai = pw_r[:, :, q], pw_i[:, :, q]
    aq3 = jnp.stack([jnp.concatenate([ar, ar], -1), jnp.concatenate([-ai, ai], -1),
                     jnp.concatenate([ai, -ai], -1)], axis=1)
    return m_intra, m_state, m_out, aq3


def _glu_kernel(u_ref, yl_ref, yc_ref, dsk_ref, w_ref, h_ref, mod_ref, gn_ref, modn_ref, o_ref, xn_ref,
                y_a, y_b, *, n, n_lat):
    def compute(y_buf):
        y = jnp.where(pl.program_id(0) >= n_lat, yc_ref[...], yl_ref[...])
        z = jax.nn.gelu(dsk_ref[...] * u_ref[...] + y).astype(BF16)
        gate = _mod_row(mod_ref, 5)
        for c in range(D // TN):
            cs = slice(c * TN, (c + 1) * TN)
            a = _dot(z, w_ref[:, cs])
            g = _dot(z, w_ref[:, D + c * TN:D + (c + 1) * TN])
            y_buf[:, cs] = h_ref[:, cs] + gate[:, cs] * (a * jax.nn.sigmoid(g))

    def emit(y_buf):
        _emit_tile(y_buf, o_ref, xn_ref, gn_ref, modn_ref, 2, TM_E)

    _lagged_steps(n, compute, emit, y_a, y_b)


def _glu(u, y_lat, y_ctx, d_skip, w_glu, h, mod, gains, rows):
    n = rows // TM_E
    cur, prev = _lag_specs(n, TM_E)
    n_lat = N_LAT // TM_E
    tile = pl.BlockSpec((TM_E, D), lambda j: (cur(j), 0))
    return pl.pallas_call(
        functools.partial(_glu_kernel, n=n, n_lat=n_lat),
        grid=(n + 1,),
        in_specs=[tile] + _split_specs(n_lat, TM_E, cur) + [
            pl.BlockSpec((1, D), lambda j: (0, 0)),
            _resident((D, 2 * D)),
            tile,
            pl.BlockSpec((1, N_MOD, D), lambda j: (_seg(cur(j), TM_E), 0, 0)),
            pl.BlockSpec((3, D), lambda j: (0, 0)),
            pl.BlockSpec((1, N_MOD, D), lambda j: (_seg(prev(j), TM_E), 0, 0)),
        ],
        out_specs=[pl.BlockSpec((TM_E, D), lambda j: (prev(j), 0))] * 2,
        out_shape=[jax.ShapeDtypeStruct((rows, D), F32), jax.ShapeDtypeStruct((rows, D), BF16)],
        scratch_shapes=[pltpu.VMEM((TM_E, D), F32)] * 2,
        compiler_params=_params("arbitrary"),
        name="s5_glu",
    )(u, y_lat, y_ctx, d_skip, w_glu, h, mod, gains, mod)


def kernel(x, c, ctx, c_ctx, w_ada, b_ada, norm_g, ffn_w_gu, ffn_w_down, a_w_in, a_v_gain, a_w_s, a_b_s, a_w_out, b_w_qkv, b_q_gain, b_k_gain, b_rpb, b_w_out, c_w_in, c_a_re, c_a_im, c_log_dt, c_b_re, c_b_im, c_c_re, c_c_im, c_d, c_w_glu):
    h, h_ctx = x.reshape(N_LAT, D).astype(F32), ctx.reshape(N_CTX, D).astype(F32)
    cond8 = jnp.concatenate([c, c_ctx[None], jnp.zeros((8 - BATCH - 1, D), c.dtype)], axis=0).astype(F32)
    mods = _adaln(cond8, w_ada.astype(F32), b_ada.astype(F32))
    norm_g = norm_g.astype(F32)
    ffn_w_gu = ffn_w_gu.astype(F32)
    ffn_w_down = ffn_w_down.astype(F32)

    xn = _prenorm(h, h_ctx, mods[0], norm_g[0], 0)
    for i in range(DEPTH):
        kind, j = i % 3, i // 3
        last = i == DEPTH - 1
        rows = N_LAT if last else N_ALL
        mod, gains = mods[i], norm_g[i]

        act, w_down = _ffn_up(xn, ffn_w_gu, ffn_w_down, i, 0, rows)
        h, xn = _ffn_down_emit(act, w_down, h, h_ctx if i == 0 else None, mod, 0, gains, mod, 1, rows)

        if kind == 0:
            z = _gmlp(xn, a_w_in[j].astype(BF16), a_v_gain[j].astype(F32)[None],
                      a_w_s[j].astype(BF16), a_b_s[j].astype(F32).T, rows)
            h, xn = _mm_res(z, None, a_w_out[j].astype(BF16), h, mod, gains, rows)
        elif kind == 1:
            head_gains = jnp.stack([b_q_gain[j], b_k_gain[j]]).astype(F32)
            qkv = _qkv(xn, b_w_qkv[j].astype(BF16), head_gains, rows)
            o_lat, o_ctx = _attention(qkv, _attn_bias_table(b_rpb[j].astype(F32)))
            h, xn = _mm_res(o_lat, o_ctx, b_w_out[j].astype(BF16), h, mod, gains, rows)
        else:
            u = _proj(xn, c_w_in[j].astype(BF16), rows)
            y_lat, y_ctx = _s5_scan(u, *_s5_weights(c_a_re[j], c_a_im[j], c_log_dt[j], c_b_re[j], c_b_im[j],
                                                    c_c_re[j], c_c_im[j]))
            h, xn = _glu(u, y_lat, y_ctx, c_d[j].astype(F32)[None], c_w_glu[j].astype(BF16), h, mod, gains, rows)

        act, w_down = _ffn_up(xn, ffn_w_gu, ffn_w_down, i, 1, rows)
        if last:
            h = _ffn_down(act, w_down, h, mod, 2, rows)
        else:
            h, xn = _ffn_down_emit(act, w_down, h, None, mod, 2, norm_g[i + 1], mods[i + 1], 0, rows)

    return h[:N_LAT].reshape(BATCH, SEQ, D).astype(x.dtype)
```

```python
import functools

import numpy as np
import jax
import jax.numpy as jnp
from jax import lax
from jax.experimental import pallas as pl
from jax.experimental.pallas import tpu as pltpu

F32 = jnp.float32
BF16 = jnp.bfloat16
HIGHEST = lax.Precision.HIGHEST

D = 2048
BATCH = 2
SEQ = 4096
CTX = 256
DEPTH = 4
N_LAT = BATCH * SEQ
N_CTX = BATCH * CTX
N_ALL = N_LAT + N_CTX
N_MOD = 9
D_FF = 5632
RMS_EPS = 1e-6
NEG_INF = -1e30
GRID_W = 64
ROWS = SEQ // GRID_W
CHUNK = 128
A_GROUPS = 16
N_HEADS = 16
HEAD_DIM = 128
WIN_H = 8
WIN_W = 16
C_GROUP = 16
C_GROUPS = D // C_GROUP
C_STATE = 64
LANES = 128
NORM_ROWS = 16
NORM_UNROLL = 4

TM = 512
TM_F = 1024
TF = 256
FF_SPLIT = 2
CAST_ROWS = 64
TM_E = 256
TN = 512
ADA_TN = 2048
S5_Q = 16
S5_GB = LANES // C_GROUP
S5_XW = S5_Q * C_GROUP
VMEM_LIMIT = 56 * 1024 * 1024


def _params(*sem):
    return pltpu.CompilerParams(dimension_semantics=sem, vmem_limit_bytes=VMEM_LIMIT)


def _seg(i, tm=TM):
    return jnp.minimum((i * tm) // SEQ, 2)


def _mod_row(mod_ref, r):
    return mod_ref[0, r:r + 1, :]


def _norm_mod_store(h_ref, xn_ref, m, g_ref, mod_ref, s):
    gs = g_ref[s:s + 1, :] * (1.0 + _mod_row(mod_ref, 3 * s + 1))
    shift = _mod_row(mod_ref, 3 * s)

    def chunk(r, carry):
        rows = pl.ds(pl.multiple_of(r * NORM_ROWS, NORM_ROWS), NORM_ROWS)
        x = h_ref[rows, :]
        ms = jnp.mean(x * x, axis=-1, keepdims=True)
        xn_ref[rows, :] = (x * lax.rsqrt(ms + RMS_EPS) * gs + shift).astype(BF16)
        return carry

    lax.fori_loop(0, m // NORM_ROWS, chunk, 0, unroll=NORM_UNROLL)


def _dot(a, b):
    return jnp.dot(a, b, preferred_element_type=F32)


def _dot_nt(a, b):
    return lax.dot_general(a, b, (((1,), (1,)), ((), ())), preferred_element_type=F32)


def _ada_kernel(c_ref, w_ref, b_ref, o_ref):
    c = c_ref[...]
    a = (c * jax.nn.sigmoid(c)).astype(BF16)
    o_ref[0] = _dot(a, w_ref[0].astype(BF16)) + b_ref[0]


def _adaln(cond8, w_ada, b_ada):
    n = N_MOD * D
    out = pl.pallas_call(
        _ada_kernel,
        grid=(DEPTH, n // ADA_TN),
        in_specs=[
            pl.BlockSpec((8, D), lambda l, j: (0, 0)),
            pl.BlockSpec((1, D, ADA_TN), lambda l, j: (l, 0, j)),
            pl.BlockSpec((1, 1, ADA_TN), lambda l, j: (l, 0, j)),
        ],
        out_specs=pl.BlockSpec((1, 8, ADA_TN), lambda l, j: (l, 0, j)),
        out_shape=jax.ShapeDtypeStruct((DEPTH, 8, n), F32),
        compiler_params=_params("parallel", "parallel"),
        name="adaln",
    )(cond8, w_ada, b_ada.reshape(DEPTH, 1, n))
    return out[:, :3].reshape(DEPTH, 3, N_MOD, D)


def _prenorm_kernel(hl_ref, hc_ref, mod_ref, g_ref, o_ref, *, s, n_lat):
    i = pl.program_id(0)

    @pl.when(i < n_lat)
    def _():
        _norm_mod_store(hl_ref, o_ref, TM, g_ref, mod_ref, s)

    @pl.when(i >= n_lat)
    def _():
        _norm_mod_store(hc_ref, o_ref, TM, g_ref, mod_ref, s)


def _prenorm(h_lat, h_ctx, mod, gains, s):
    n_lat = N_LAT // TM
    return pl.pallas_call(
        functools.partial(_prenorm_kernel, s=s, n_lat=n_lat),
        grid=(N_ALL // TM,),
        in_specs=_split_specs(n_lat, TM) + [
            pl.BlockSpec((1, N_MOD, D), lambda i: (_seg(i), 0, 0)),
            pl.BlockSpec((3, D), lambda i: (0, 0)),
        ],
        out_specs=pl.BlockSpec((TM, D), lambda i: (i, 0)),
        out_shape=jax.ShapeDtypeStruct((N_ALL, D), BF16),
        compiler_params=_params("parallel"),
        name="ffn_prenorm",
    )(h_lat, h_ctx, mod, gains)


def _ffn_up_kernel(*refs, subtiles, n_extra):
    xn_ref, wg_ref, wu_ref, wd_ref = refs[:4]
    o_ref, wdb_ref = refs[4 + n_extra:6 + n_extra]
    for src_ref, dst_ref in zip(refs[4:4 + n_extra], refs[6 + n_extra:]):
        dst_ref[...] = src_ref[0].astype(BF16)
    wdb_ref[...] = wd_ref[0, 0].astype(BF16)
    wg = wg_ref[0, 0].astype(BF16)
    wu = wu_ref[0, 0].astype(BF16)
    for r0, m in subtiles:
        xn = xn_ref[r0:r0 + m, :]
        g = _dot(xn, wg)
        u = _dot(xn, wu)
        o_ref[r0:r0 + m, :] = ((g * jax.nn.sigmoid(g)) * u).astype(BF16)


def _ffn_up(xn, w_gu, w_down, layer, half, rows, extras=()):
    nk = D_FF // TF
    wd_blk = lambda r, k: jnp.where(r == 0, k, nk - 1)
    assert D // CAST_ROWS <= FF_SPLIT * nk
    ex_blk = lambda r, k: jnp.minimum(r * nk + k, D // CAST_ROWS - 1)
    hr = rows // FF_SPLIT
    subtiles = [(r0, min(TM_F, hr - r0)) for r0 in range(0, hr, TM_F)]
    ex_in, ex_out, ex_shape = [], [], []
    for w, idx in extras:
        cols = w.shape[-1]
        ex_in.append(pl.BlockSpec((1, CAST_ROWS, cols), lambda r, k, idx=idx: (idx, ex_blk(r, k), 0)))
        ex_out.append(pl.BlockSpec((CAST_ROWS, cols), lambda r, k: (ex_blk(r, k), 0)))
        ex_shape.append(jax.ShapeDtypeStruct((D, cols), BF16))
    return pl.pallas_call(
        functools.partial(_ffn_up_kernel, subtiles=subtiles, n_extra=len(extras)),
        grid=(FF_SPLIT, nk),
        in_specs=[
            pl.BlockSpec((hr, D), lambda r, k: (r, 0), pipeline_mode=pl.Buffered(1)),
            pl.BlockSpec((1, 1, D, TF), lambda r, k: (layer, half, 0, k)),
            pl.BlockSpec((1, 1, D, TF), lambda r, k: (layer, half, 0, nk + k)),
            pl.BlockSpec((1, 1, TF, D), lambda r, k: (layer, half, wd_blk(r, k), 0)),
        ] + ex_in,
        out_specs=[pl.BlockSpec((hr, TF), lambda r, k: (r, k)),
                   pl.BlockSpec((TF, D), lambda r, k: (wd_blk(r, k), 0))] + ex_out,
        out_shape=[jax.ShapeDtypeStruct((rows, D_FF), BF16), jax.ShapeDtypeStruct((D_FF, D), BF16)] + ex_shape,
        compiler_params=_params("arbitrary", "arbitrary"),
        name="ffn_up",
    )(xn, w_gu, w_gu, w_down, *[w for w, _ in extras])


def _ffn_down_kernel(a_ref, wd_ref, h_ref, mod_ref, o_ref, *, s):
    a = a_ref[...]
    gate = 0.5 * _mod_row(mod_ref, 3 * s + 2)
    for j in range(D // TN):
        cs = slice(j * TN, (j + 1) * TN)
        o_ref[:, cs] = h_ref[:, cs] + gate[:, cs] * _dot(a, wd_ref[:, cs])


def _ffn_down(act, w_down, h, mod, s, rows):
    tile = pl.BlockSpec((TM, D), lambda i: (i, 0))
    return pl.pallas_call(
        functools.partial(_ffn_down_kernel, s=s),
        grid=(rows // TM,),
        in_specs=[
            pl.BlockSpec((TM, D_FF), lambda i: (i, 0)),
            _resident((D_FF, D)),
            tile,
            pl.BlockSpec((1, N_MOD, D), lambda i: (_seg(i), 0, 0)),
        ],
        out_specs=tile,
        out_shape=jax.ShapeDtypeStruct((rows, D), F32),
        compiler_params=_params("arbitrary"),
        name="ffn_down",
    )(act, w_down, h, mod)


def _emit_tile(y_buf, o_ref, xn_ref, gn_ref, modn_ref, sn, tm):
    gs = gn_ref[sn:sn + 1, :] * (1.0 + _mod_row(modn_ref, 3 * sn + 1))
    shift = _mod_row(modn_ref, 3 * sn)
    for r in range(tm // NORM_ROWS):
        rows = slice(r * NORM_ROWS, (r + 1) * NORM_ROWS)
        y = y_buf[rows, :]
        o_ref[rows, :] = y
        ms = jnp.mean(y * y, axis=-1, keepdims=True)
        xn_ref[rows, :] = (y * lax.rsqrt(ms + RMS_EPS) * gs + shift).astype(BF16)


def _lagged_steps(n, compute, emit, y_a, y_b):
    j = pl.program_id(0)
    last = y_a if (n - 1) % 2 == 0 else y_b

    @pl.when(j == 0)
    def _():
        compute(y_a)

    @pl.when((j > 0) & (j < n) & (j % 2 == 1))
    def _():
        emit(y_a)
        compute(y_b)

    @pl.when((j > 0) & (j < n) & (j % 2 == 0))
    def _():
        emit(y_b)
        compute(y_a)

    @pl.when(j == n)
    def _():
        emit(last)


def _lag_specs(n, tm):
    cur = lambda j: jnp.minimum(j, n - 1)
    prev = lambda j: jnp.maximum(j - 1, 0)
    return cur, prev


def _ffn_down_emit_kernel(*refs, s, sn, n, n_lat):
    if n_lat is None:
        a_ref, wd_ref, h_ref, mod_ref, gn_ref, modn_ref, o_ref, xn_ref, y_a, y_b = refs
    else:
        a_ref, wd_ref, h_ref, hc_ref, mod_ref, gn_ref, modn_ref, o_ref, xn_ref, y_a, y_b = refs

    def compute(y_buf):
        a = a_ref[...]
        gate = 0.5 * _mod_row(mod_ref, 3 * s + 2)
        for c in range(D // TN):
            cs = slice(c * TN, (c + 1) * TN)
            if n_lat is None:
                res = h_ref[:, cs]
            else:
                res = jnp.where(pl.program_id(0) >= n_lat, hc_ref[:, cs], h_ref[:, cs])
            y_buf[:, cs] = res + gate[:, cs] * _dot(a, wd_ref[:, cs])

    def emit(y_buf):
        _emit_tile(y_buf, o_ref, xn_ref, gn_ref, modn_ref, sn, TM_E)

    _lagged_steps(n, compute, emit, y_a, y_b)


def _ffn_down_emit(act, w_down, h, h_ctx, mod, s, gains_n, mod_n, sn, rows):
    n = rows // TM_E
    cur, prev = _lag_specs(n, TM_E)
    n_lat = None if h_ctx is None else N_LAT // TM_E
    if h_ctx is None:
        h_specs, hs = [pl.BlockSpec((TM_E, D), lambda j: (cur(j), 0))], (h,)
    else:
        h_specs, hs = _split_specs(n_lat, TM_E, cur), (h, h_ctx)
    return pl.pallas_call(
        functools.partial(_ffn_down_emit_kernel, s=s, sn=sn, n=n, n_lat=n_lat),
        grid=(n + 1,),
        in_specs=[
            pl.BlockSpec((TM_E, D_FF), lambda j: (cur(j), 0)),
            _resident((D_FF, D)),
        ] + h_specs + [
            pl.BlockSpec((1, N_MOD, D), lambda j: (_seg(cur(j), TM_E), 0, 0)),
            pl.BlockSpec((3, D), lambda j: (0, 0)),
            pl.BlockSpec((1, N_MOD, D), lambda j: (_seg(prev(j), TM_E), 0, 0)),
        ],
        out_specs=[pl.BlockSpec((TM_E, D), lambda j: (prev(j), 0))] * 2,
        out_shape=[jax.ShapeDtypeStruct((rows, D), F32), jax.ShapeDtypeStruct((rows, D), BF16)],
        scratch_shapes=[pltpu.VMEM((TM_E, D), F32)] * 2,
        compiler_params=_params("arbitrary"),
        name="ffn_down_emit",
    )(act, w_down, *hs, mod, gains_n, mod_n)


def _mm_res_kernel(*refs, n, n_lat):
    if n_lat is None:
        z_ref, w_ref, h_ref, mod_ref, gn_ref, modn_ref, o_ref, xn_ref, y_a, y_b = refs
    else:
        zl_ref, zc_ref, w_ref, h_ref, mod_ref, gn_ref, modn_ref, o_ref, xn_ref, y_a, y_b = refs

    def compute(y_buf):
        if n_lat is None:
            z = z_ref[...]
        else:
            z = jnp.where(pl.program_id(0) >= n_lat, zc_ref[...], zl_ref[...])
        gate = _mod_row(mod_ref, 5)
        for c in range(D // TN):
            cs = slice(c * TN, (c + 1) * TN)
            y_buf[:, cs] = h_ref[:, cs] + gate[:, cs] * _dot(z, w_ref[:, cs])

    def emit(y_buf):
        _emit_tile(y_buf, o_ref, xn_ref, gn_ref, modn_ref, 2, TM)

    _lagged_steps(n, compute, emit, y_a, y_b)


def _split_specs(n_lat, tm, tile_of=lambda j: j):
    return [pl.BlockSpec((tm, D), lambda j: (jnp.minimum(tile_of(j), n_lat - 1), 0)),
            pl.BlockSpec((tm, D), lambda j: (jnp.maximum(tile_of(j) - n_lat, 0), 0))]


def _mm_res(z, z_ctx, w, h, mod, gains, rows):
    n = rows // TM
    cur, prev = _lag_specs(n, TM)
    n_lat = None if z_ctx is None else N_LAT // TM
    if z_ctx is None:
        z_specs, zs = [pl.BlockSpec((TM, D), lambda j: (cur(j), 0))], (z,)
    else:
        z_specs, zs = _split_specs(n_lat, TM, cur), (z, z_ctx)
    return pl.pallas_call(
        functools.partial(_mm_res_kernel, n=n, n_lat=n_lat),
        grid=(n + 1,),
        in_specs=z_specs + [
            _resident((D, D)),
            pl.BlockSpec((TM, D), lambda j: (cur(j), 0)),
            pl.BlockSpec((1, N_MOD, D), lambda j: (_seg(cur(j), TM), 0, 0)),
            pl.BlockSpec((3, D), lambda j: (0, 0)),
            pl.BlockSpec((1, N_MOD, D), lambda j: (_seg(prev(j), TM), 0, 0)),
        ],
        out_specs=[pl.BlockSpec((TM, D), lambda j: (prev(j), 0))] * 2,
        out_shape=[jax.ShapeDtypeStruct((rows, D), F32), jax.ShapeDtypeStruct((rows, D), BF16)],
        scratch_shapes=[pltpu.VMEM((TM, D), F32)] * 2,
        compiler_params=_params("arbitrary"),
        name="mm_res",
    )(*zs, w, h, mod, gains, mod)


def _gmlp_kernel(xn_ref, win_ref, vg_ref, ws_ref, bst_ref, z_ref, y_ref, *, nj):
    xn = xn_ref[...]
    for j in range(nj):
        y_ref[j] = jax.nn.gelu(_dot(xn, win_ref[:, j * TN:(j + 1) * TN]))

    half = nj // 2
    ssq = jnp.zeros((TM, 1), F32)
    for jj in range(half, nj):
        yv = y_ref[jj]
        ssq = ssq + jnp.sum(yv * yv, axis=-1, keepdims=True)
    inv = lax.rsqrt(ssq / D + RMS_EPS)
    gpb = TN // CHUNK
    for cb in range(half):
        v = (y_ref[half + cb] * inv * vg_ref[:, cb * TN:(cb + 1) * TN]).astype(BF16)
        u = y_ref[cb]
        for gg in range(gpb):
            grp = cb * gpb + gg
            cs = slice(gg * CHUNK, (gg + 1) * CHUNK)
            for c in range(TM // CHUNK):
                rs = slice(c * CHUNK, (c + 1) * CHUNK)
                sg = _dot(ws_ref[grp], v[rs, cs]) + bst_ref[:, grp:grp + 1]
                z_ref[rs, grp * CHUNK:(grp + 1) * CHUNK] = (u[rs, cs] * sg).astype(BF16)


def _resident(shape):
    return pl.BlockSpec(shape, lambda *_: (0,) * len(shape), pipeline_mode=pl.Buffered(1))


def _gmlp(xn, w_in, v_gain, w_s, b_s_t, rows):
    nj = (2 * D) // TN
    return pl.pallas_call(
        functools.partial(_gmlp_kernel, nj=nj),
        grid=(rows // TM,),
        in_specs=[
            pl.BlockSpec((TM, D), lambda i: (i, 0)),
            _resident((D, 2 * D)),
            pl.BlockSpec((1, D), lambda i: (0, 0)),
            _resident((A_GROUPS, CHUNK, CHUNK)),
            pl.BlockSpec((CHUNK, A_GROUPS), lambda i: (0, 0)),
        ],
        out_specs=pl.BlockSpec((TM, D), lambda i: (i, 0)),
        out_shape=jax.ShapeDtypeStruct((rows, D), BF16),
        scratch_shapes=[pltpu.VMEM((nj, TM, TN), F32)],
        compiler_params=_params("arbitrary"),
        name="gmlp",
    )(xn, w_in, v_gain, w_s, b_s_t)


def _qkv_kernel(xn_ref, w_ref, hg_ref, o_ref):
    xn = xn_ref[...]
    for j in range((3 * D) // TN):
        y = _dot(xn, w_ref[:, j * TN:(j + 1) * TN])
        part = j // (D // TN)
        if part == 2:
            o_ref[:, j * TN:(j + 1) * TN] = y.astype(BF16)
            continue
        gain = hg_ref[part:part + 1, :]
        for hh in range(TN // HEAD_DIM):
            yh = y[:, hh * HEAD_DIM:(hh + 1) * HEAD_DIM]
            ms = jnp.mean(yh * yh, axis=-1, keepdims=True)
            c0 = j * TN + hh * HEAD_DIM
            o_ref[:, c0:c0 + HEAD_DIM] = (yh * lax.rsqrt(ms + RMS_EPS) * gain).astype(BF16)


def _qkv(xn, w_qkv, head_gains, rows):
    return pl.pallas_call(
        _qkv_kernel,
        grid=(rows // TM,),
        in_specs=[
            pl.BlockSpec((TM, D), lambda i: (i, 0)),
            _resident((D, 3 * D)),
            pl.BlockSpec((2, HEAD_DIM), lambda i: (0, 0)),
        ],
        out_specs=pl.BlockSpec((TM, 3 * D), lambda i: (i, 0)),
        out_shape=jax.ShapeDtypeStruct((rows, 3 * D), BF16),
        compiler_params=_params("arbitrary"),
        name="qkv",
    )(xn, w_qkv, head_gains)


ATT_QR = 8
ATT_KR = ATT_QR + WIN_H
ATT_QB = ATT_QR * GRID_W
ATT_KB = ATT_KR * GRID_W
N_DR = 2 * WIN_H - 1
ATT_PATTERNS = ((0, 0), (ATT_QR, ATT_QR - WIN_H // 2), (ROWS - ATT_QR, ROWS - ATT_KR))


def _attn_kernel(q_ref, k_ref, v_ref, qc_ref, kc_ref, vc_ref, t_ref, o_ref, oc_ref, bias_ref):
    scale = HEAD_DIM ** -0.5
    left = lax.broadcasted_iota(jnp.int32, (GRID_W, 2 * GRID_W), 1) < GRID_W
    neg = jnp.full((GRID_W, 2 * GRID_W), NEG_INF, F32)

    for p, (r0, kr_base) in enumerate(ATT_PATTERNS):
        for qr in range(ATT_QR):
            r = r0 + qr
            rstart = min(max(r - WIN_H // 2, 0), ROWS - WIN_H)
            for kp in range(ATT_KR // 2):
                halves = []
                for kr in (kr_base + 2 * kp, kr_base + 2 * kp + 1):
                    inside = rstart <= kr < rstart + WIN_H
                    halves.append(t_ref[0, kr - r + WIN_H - 1] if inside else None)
                a, b = halves
                if a is None and b is None:
                    blk = neg
                else:
                    blk = jnp.where(left, neg if a is None else a, neg if b is None else b)
                bias_ref[p, qr * GRID_W:(qr + 1) * GRID_W, kp * 2 * GRID_W:(kp + 1) * 2 * GRID_W] = blk

    kc = kc_ref[...]
    vc = vc_ref[...]

    def block(q0, k0, p):
        q = q_ref[pl.ds(q0, ATT_QB), :]
        k = k_ref[pl.ds(k0, ATT_KB), :]
        v = v_ref[pl.ds(k0, ATT_KB), :]
        s_win = _dot_nt(q, k) * scale + bias_ref[p]
        s_ctx = _dot_nt(q, kc) * scale
        m = jnp.maximum(jnp.max(s_win, axis=-1, keepdims=True), jnp.max(s_ctx, axis=-1, keepdims=True))
        p_win = jnp.exp(s_win - m)
        p_ctx = jnp.exp(s_ctx - m)
        denom = jnp.sum(p_win, axis=-1, keepdims=True) + jnp.sum(p_ctx, axis=-1, keepdims=True)
        o = _dot(p_win.astype(BF16), v) + _dot(p_ctx.astype(BF16), vc)
        o_ref[pl.ds(q0, ATT_QB), :] = (o / denom).astype(BF16)

    block(0, 0, 0)
    for rb in range(1, ROWS // ATT_QR - 1):
        block(rb * ATT_QB, rb * ATT_QB - (WIN_H // 2) * GRID_W, 1)
    block(SEQ - ATT_QB, SEQ - ATT_KB, 2)

    s = _dot_nt(qc_ref[...], kc) * scale
    pc = jnp.exp(s - jnp.max(s, axis=-1, keepdims=True))
    oc = _dot(pc.astype(BF16), vc) / jnp.sum(pc, axis=-1, keepdims=True)
    oc_ref[...] = oc.astype(BF16)


def _attention(qkv, bias_tab):
    lat_blk = (SEQ, HEAD_DIM)
    ctx_blk = (CTX, HEAD_DIM)
    ctx0 = N_LAT // CTX
    return pl.pallas_call(
        _attn_kernel,
        grid=(N_HEADS, BATCH),
        in_specs=[
            pl.BlockSpec(lat_blk, lambda h, b: (b, h)),
            pl.BlockSpec(lat_blk, lambda h, b: (b, N_HEADS + h)),
            pl.BlockSpec(lat_blk, lambda h, b: (b, 2 * N_HEADS + h)),
            pl.BlockSpec(ctx_blk, lambda h, b: (ctx0 + b, h)),
            pl.BlockSpec(ctx_blk, lambda h, b: (ctx0 + b, N_HEADS + h)),
            pl.BlockSpec(ctx_blk, lambda h, b: (ctx0 + b, 2 * N_HEADS + h)),
            pl.BlockSpec((1, N_DR, GRID_W, 2 * GRID_W), lambda h, b: (h, 0, 0, 0)),
        ],
        out_specs=[
            pl.BlockSpec(lat_blk, lambda h, b: (b, h)),
            pl.BlockSpec(ctx_blk, lambda h, b: (b, h)),
        ],
        out_shape=[
            jax.ShapeDtypeStruct((N_LAT, D), BF16),
            jax.ShapeDtypeStruct((N_CTX, D), BF16),
        ],
        scratch_shapes=[pltpu.VMEM((len(ATT_PATTERNS), ATT_QB, ATT_KB), F32)],
        compiler_params=_params("parallel", "parallel"),
        name="nat_attention",
    )(qkv, qkv, qkv, qkv, qkv, qkv, bias_tab)


def _attn_bias_table(rpb):
    qcol = np.arange(GRID_W)[:, None]
    kcol = np.arange(2 * GRID_W)[None, :] % GRID_W
    cstart = np.clip(qcol - WIN_W // 2, 0, GRID_W - WIN_W)
    col_valid = (kcol >= cstart) & (kcol < cstart + WIN_W)
    dc_idx = np.clip(kcol - qcol, 1 - WIN_W, WIN_W - 1) + (WIN_W - 1)
    onehot = (dc_idx[None] == np.arange(2 * WIN_W - 1)[:, None, None]) & col_valid[None]
    onehot = jnp.asarray(onehot.reshape(2 * WIN_W - 1, -1), F32)
    mask = jnp.asarray(np.where(col_valid, 0.0, NEG_INF).reshape(-1), F32)
    t = jnp.dot(rpb.reshape(N_HEADS * N_DR, 2 * WIN_W - 1), onehot, precision=HIGHEST) + mask
    return t.reshape(N_HEADS, N_DR, GRID_W, 2 * GRID_W)


def _proj_kernel(xn_ref, w_ref, o_ref):
    xn = xn_ref[...]
    for j in range(D // TN):
        o_ref[:, j * TN:(j + 1) * TN] = _dot(xn, w_ref[:, j * TN:(j + 1) * TN])


def _proj(xn, w, rows):
    return pl.pallas_call(
        _proj_kernel,
        grid=(rows // TM,),
        in_specs=[
            pl.BlockSpec((TM, D), lambda i: (i, 0)),
            _resident((D, D)),
        ],
        out_specs=pl.BlockSpec((TM, D), lambda i: (i, 0)),
        out_shape=jax.ShapeDtypeStruct((rows, D), F32),
        compiler_params=_params("arbitrary"),
        name="s5_in_proj",
    )(xn, w)


S5_CL = SEQ // S5_Q
S5_CC = CTX // S5_Q
S5_HALF = LANES // C_GROUP


def _s5_kernel(ul_ref, uc_ref, mi_ref, ms_ref, mo_ref, aq_ref, yl_ref, yc_ref,
               xl_scr, xc_scr, s_scr, ssw_scr, hp_scr, yl_acc, yc_acc):
    d = pl.program_id(2)
    ns = 2 * C_STATE

    def granule_transpose(rows):
        n = rows[0].shape[0]
        granule = lax.broadcasted_iota(jnp.int32, (n, LANES), 1) // C_GROUP
        rows = list(rows)
        k = S5_GB // 2
        while k:
            hi = (granule & k) != 0
            for i in range(S5_GB):
                if i & k:
                    continue
                a, b = rows[i], rows[i + k]
                rows[i] = jnp.where(hi, pltpu.roll(b, k * C_GROUP, 1), a)
                rows[i + k] = jnp.where(hi, b, pltpu.roll(a, LANES - k * C_GROUP, 1))
            k //= 2
        return rows

    def to_chunks(u_ref, x_scr, nch):
        for hf in range(S5_Q // S5_HALF):
            by_token = [u_ref[pl.ds(hf * S5_HALF + tl, nch, stride=S5_Q), :] for tl in range(S5_HALF)]
            for gl, x in enumerate(granule_transpose(by_token)):
                x_scr[gl, :, hf * LANES:(hf + 1) * LANES] = x.astype(BF16)

    def from_chunks(y_acc, y_ref, nch):
        for hf in range(S5_Q // S5_HALF):
            by_group = [y_acc[gl, :, hf * LANES:(hf + 1) * LANES] for gl in range(S5_GB)]
            for tl, y in enumerate(granule_transpose(by_group)):
                y_ref[pl.ds(hf * S5_HALF + tl, nch, stride=S5_Q), :] = y

    @pl.when(d == 0)
    def _():
        to_chunks(ul_ref, xl_scr, S5_CL)
        to_chunks(uc_ref, xc_scr, S5_CC)
        yl_acc[...] = jnp.zeros_like(yl_acc)
        yc_acc[...] = jnp.zeros_like(yc_acc)

    lat0 = S5_CC * S5_GB
    for j in range(S5_GB):
        xl, xc = xl_scr[j], xc_scr[j]
        yl_acc[j] += _dot(xl, mi_ref[0, j])
        yc_acc[j] += _dot(xc, mi_ref[0, j])
        sl = _dot(xl, ms_ref[0, j])
        sc = _dot(xc, ms_ref[0, j])
        s_scr[pl.ds(j, S5_CC, stride=S5_GB), :] = sc[:, :ns]
        s_scr[pl.ds(lat0 + j, S5_CL, stride=S5_GB), :] = sl[:, :ns]
        ssw_scr[pl.ds(j, S5_CC, stride=S5_GB), :] = sc[:, ns:]
        ssw_scr[pl.ds(lat0 + j, S5_CL, stride=S5_GB), :] = sl[:, ns:]

    a1 = aq_ref[0, 0]
    a2 = aq_ref[0, 1]
    a3 = aq_ref[0, 2]

    def step(pos, carry):
        hs, hsw = carry
        row = pl.multiple_of(pos * S5_GB, S5_GB)
        hp_scr[pl.ds(row, S5_GB), :] = hs
        s = s_scr[pl.ds(row, S5_GB), :]
        ssw = ssw_scr[pl.ds(row, S5_GB), :]
        return hs * a1 + hsw * a2 + s, hsw * a1 + hs * a3 + ssw

    def ctx_step(i, carry):
        return step(jnp.where(d == 0, i, S5_CC - 1 - i), carry)

    def lat_step(i, carry):
        return step(S5_CC + jnp.where(d == 0, i, S5_CL - 1 - i), carry)

    zero = jnp.zeros((S5_GB, ns), F32)
    carry = lax.fori_loop(0, S5_CC, ctx_step, (zero, zero))
    lax.fori_loop(0, S5_CL, lat_step, carry, unroll=4)

    for j in range(S5_GB):
        mo = mo_ref[0, j]
        hc = hp_scr[pl.ds(j, S5_CC, stride=S5_GB), :]
        hl = hp_scr[pl.ds(lat0 + j, S5_CL, stride=S5_GB), :]
        yc_acc[j] += _dot(hc.astype(BF16), mo)
        yl_acc[j] += _dot(hl.astype(BF16), mo)

    @pl.when(d == 1)
    def _():
        from_chunks(yl_acc, yl_ref, S5_CL)
        from_chunks(yc_acc, yc_ref, S5_CC)


def _s5_scan(u, mi, ms, mo, aq):
    ns = 2 * C_STATE
    nrow = (S5_CC + S5_CL) * S5_GB
    ctx0 = N_LAT // CTX
    return pl.pallas_call(
        _s5_kernel,
        grid=(BATCH, C_GROUPS // S5_GB, 2),
        in_specs=[
            pl.BlockSpec((SEQ, LANES), lambda b, g, d: (b, g)),
            pl.BlockSpec((CTX, LANES), lambda b, g, d: (ctx0 + b, g)),
            pl.BlockSpec((1, S5_GB, S5_XW, S5_XW), lambda b, g, d: (d, g, 0, 0)),
            pl.BlockSpec((1, S5_GB, S5_XW, 2 * ns), lambda b, g, d: (d, g, 0, 0)),
            pl.BlockSpec((1, S5_GB, ns, S5_XW), lambda b, g, d: (d, g, 0, 0)),
            pl.BlockSpec((1, 3, S5_GB, ns), lambda b, g, d: (d, 0, g, 0)),
        ],
        out_specs=[
            pl.BlockSpec((SEQ, LANES), lambda b, g, d: (b, g)),
            pl.BlockSpec((CTX, LANES), lambda b, g, d: (b, g)),
        ],
        out_shape=[
            jax.ShapeDtypeStruct((N_LAT, D), F32),
            jax.ShapeDtypeStruct((N_CTX, D), F32),
        ],
        scratch_shapes=[
            pltpu.VMEM((S5_GB, S5_CL, S5_XW), BF16),
            pltpu.VMEM((S5_GB, S5_CC, S5_XW), BF16),
            pltpu.VMEM((nrow, ns), F32),
            pltpu.VMEM((nrow, ns), F32),
            pltpu.VMEM((nrow, ns), F32),
            pltpu.VMEM((S5_GB, S5_CL, S5_XW), F32),
            pltpu.VMEM((S5_GB, S5_CC, S5_XW), F32),
        ],
        compiler_params=_params("arbitrary", "arbitrary", "arbitrary"),
        name="s5_scan",
    )(u, u, mi, ms, mo, aq)


def _lag_table_kernel(b_ref, c_ref, o_ref):
    for g in range(S5_GB):
        o_ref[0, g] = jnp.dot(b_ref[0, g], c_ref[0, g], preferred_element_type=F32, precision=HIGHEST)


def _lag_table(bt, ck):
    ns = 2 * C_STATE
    return pl.pallas_call(
        _lag_table_kernel,
        grid=(2, C_GROUPS // S5_GB),
        in_specs=[
            pl.BlockSpec((1, S5_GB, C_GROUP, ns), lambda d, g: (d, g, 0, 0)),
            pl.BlockSpec((1, S5_GB, ns, S5_XW), lambda d, g: (d, g, 0, 0)),
        ],
        out_specs=pl.BlockSpec((1, S5_GB, C_GROUP, S5_XW), lambda d, g: (d, g, 0, 0)),
        out_shape=jax.ShapeDtypeStruct((2, C_GROUPS, C_GROUP, S5_XW), F32),
        compiler_params=_params("parallel", "parallel"),
        name="s5_lag_table",
    )(bt, ck)


def _s5_weights(a_re, a_im, log_dt, b_re, b_im, c_re, c_im):
    q = S5_Q
    a_re, a_im, b_re, b_im, c_re, c_im = (v.astype(F32) for v in (a_re, a_im, b_re, b_im, c_re, c_im))
    dt = jnp.exp(log_dt.astype(F32))[..., None]
    zr, zi = a_re * dt, a_im * dt
    ab_r, ab_i = jnp.exp(zr) * jnp.cos(zi), jnp.exp(zr) * jnp.sin(zi)
    den = a_re * a_re + a_im * a_im
    f_r = ((ab_r - 1.0) * a_re + ab_i * a_im) / den
    f_i = (ab_i * a_re - (ab_r - 1.0) * a_im) / den
    bb_r = f_r[..., None] * b_re - f_i[..., None] * b_im
    bb_i = f_r[..., None] * b_im + f_i[..., None] * b_re

    taus = jnp.arange(q + 1, dtype=F32)[:, None]
    mag = jnp.exp(zr[:, :, None] * taus)
    pw_r, pw_i = mag * jnp.cos(zi[:, :, None] * taus), mag * jnp.sin(zi[:, :, None] * taus)

    col = np.arange(S5_XW)
    e_t = jnp.asarray(col[None, :] // C_GROUP == np.arange(q)[:, None], F32)
    e_o = jnp.asarray(col[None, :] % C_GROUP == np.arange(C_GROUP)[:, None], F32)
    xc_r = jnp.einsum("dgop,ox->dgpx", c_re, e_o, precision=HIGHEST)
    xc_i = jnp.einsum("dgop,ox->dgpx", c_im, e_o, precision=HIGHEST)

    def readout(p_r, p_i):
        xp_r = jnp.einsum("dgtp,tx->dgpx", p_r, e_t, precision=HIGHEST)
        xp_i = jnp.einsum("dgtp,tx->dgpx", p_i, e_t, precision=HIGHEST)
        return xc_r * xp_r - xc_i * xp_i, xc_r * xp_i + xc_i * xp_r

    wo_r, wo_i = readout(jnp.stack([pw_r[0, :, 1:q + 1], pw_r[1, :, q:0:-1]]),
                         jnp.stack([pw_i[0, :, 1:q + 1], pw_i[1, :, q:0:-1]]))
    m_out = jnp.concatenate([wo_r, -wo_i], axis=2).astype(BF16)

    ck_r, ck_i = readout(jnp.stack([pw_r[0, :, :q], pw_r[1, :, q - 1::-1]]),
                         jnp.stack([pw_i[0, :, :q], pw_i[1, :, q - 1::-1]]))
    bt_r, bt_i = bb_r.transpose(0, 1, 3, 2), bb_i.transpose(0, 1, 3, 2)
    kmat = _lag_table(jnp.concatenate([bt_r, -bt_i], axis=-1),
                      jnp.concatenate([ck_r, ck_i], axis=2))
    zpad = jnp.zeros_like(kmat[0, :, :, :(q - 1) * C_GROUP])
    kflat = jnp.stack([jnp.concatenate([zpad, kmat[0]], axis=-1), jnp.concatenate([kmat[1], zpad], axis=-1)])
    m_intra = jnp.stack([kflat[..., (q - 1 - s) * C_GROUP:(q - 1 - s) * C_GROUP + S5_XW] for s in range(q)],
                        axis=2).reshape(2, C_GROUPS, S5_XW, S5_XW).astype(BF16)

    ps_r = jnp.stack([pw_r[0, :, q - 1::-1], pw_r[1, :, :q]])
    ps_i = jnp.stack([pw_i[0, :, q - 1::-1], pw_i[1, :, :q]])
    p4_r = jnp.concatenate([ps_r] * 4, axis=-1)[:, :, :, None]
    p4_i = jnp.concatenate([ps_i] * 4, axis=-1)[:, :, :, None]
    b4_a = jnp.concatenate([bt_r, bt_i, bt_i, bt_r], axis=-1)[:, :, None]
    b4_b = jnp.concatenate([-bt_i, bt_r, bt_r, -bt_i], axis=-1)[:, :, None]
    m_state = (p4_r * b4_a + p4_i * b4_b).reshape(2, C_GROUPS, S5_XW, 4 * C_STATE).astype(BF16)

    ar, ai = pw_r[:, :, q], pw_i[:, :, q]
    aq3 = jnp.stack([jnp.concatenate([ar, ar], -1), jnp.concatenate([-ai, ai], -1),
                     jnp.concatenate([ai, -ai], -1)], axis=1)
    return m_intra, m_state, m_out, aq3


def _glu_kernel(u_ref, yl_ref, yc_ref, dsk_ref, w_ref, h_ref, mod_ref, gn_ref, modn_ref, o_ref, xn_ref,
                y_a, y_b, *, n, n_lat):
    def compute(y_buf):
        y = jnp.where(pl.program_id(0) >= n_lat, yc_ref[...], yl_ref[...])
        z = jax.nn.gelu(dsk_ref[...] * u_ref[...] + y).astype(BF16)
        gate = _mod_row(mod_ref, 5)
        for c in range(D // TN):
            cs = slice(c * TN, (c + 1) * TN)
            a = _dot(z, w_ref[:, cs])
            g = _dot(z, w_ref[:, D + c * TN:D + (c + 1) * TN])
            y_buf[:, cs] = h_ref[:, cs] + gate[:, cs] * (a * jax.nn.sigmoid(g))

    def emit(y_buf):
        _emit_tile(y_buf, o_ref, xn_ref, gn_ref, modn_ref, 2, TM_E)

    _lagged_steps(n, compute, emit, y_a, y_b)


def _glu(u, y_lat, y_ctx, d_skip, w_glu, h, mod, gains, rows):
    n = rows // TM_E
    cur, prev = _lag_specs(n, TM_E)
    n_lat = N_LAT // TM_E
    tile = pl.BlockSpec((TM_E, D), lambda j: (cur(j), 0))
    return pl.pallas_call(
        functools.partial(_glu_kernel, n=n, n_lat=n_lat),
        grid=(n + 1,),
        in_specs=[tile] + _split_specs(n_lat, TM_E, cur) + [
            pl.BlockSpec((1, D), lambda j: (0, 0)),
            _resident((D, 2 * D)),
            tile,
            pl.BlockSpec((1, N_MOD, D), lambda j: (_seg(cur(j), TM_E), 0, 0)),
            pl.BlockSpec((3, D), lambda j: (0, 0)),
            pl.BlockSpec((1, N_MOD, D), lambda j: (_seg(prev(j), TM_E), 0, 0)),
        ],
        out_specs=[pl.BlockSpec((TM_E, D), lambda j: (prev(j), 0))] * 2,
        out_shape=[jax.ShapeDtypeStruct((rows, D), F32), jax.ShapeDtypeStruct((rows, D), BF16)],
        scratch_shapes=[pltpu.VMEM((TM_E, D), F32)] * 2,
        compiler_params=_params("arbitrary"),
        name="s5_glu",
    )(u, y_lat, y_ctx, d_skip, w_glu, h, mod, gains, mod)


def kernel(x, c, ctx, c_ctx, w_ada, b_ada, norm_g, ffn_w_gu, ffn_w_down, a_w_in, a_v_gain, a_w_s, a_b_s, a_w_out, b_w_qkv, b_q_gain, b_k_gain, b_rpb, b_w_out, c_w_in, c_a_re, c_a_im, c_log_dt, c_b_re, c_b_im, c_c_re, c_c_im, c_d, c_w_glu):
    h, h_ctx = x.reshape(N_LAT, D).astype(F32), ctx.reshape(N_CTX, D).astype(F32)
    cond8 = jnp.concatenate([c, c_ctx[None], jnp.zeros((8 - BATCH - 1, D), c.dtype)], axis=0).astype(F32)
    mods = _adaln(cond8, w_ada.astype(F32), b_ada.astype(F32))
    norm_g = norm_g.astype(F32)
    ffn_w_gu = ffn_w_gu.astype(F32)
    ffn_w_down = ffn_w_down.astype(F32)

    xn = _prenorm(h, h_ctx, mods[0], norm_g[0], 0)
    for i in range(DEPTH):
        kind, j = i % 3, i // 3
        last = i == DEPTH - 1
        rows = N_LAT if last else N_ALL
        mod, gains = mods[i], norm_g[i]

        w_in, w_out = ((a_w_in, a_w_out), (b_w_qkv, b_w_out), (c_w_in, c_w_glu))[kind]
        act, w_down, w_in, w_out = _ffn_up(xn, ffn_w_gu, ffn_w_down, i, 0, rows,
                                           extras=((w_in.astype(F32), j), (w_out.astype(F32), j)))
        h, xn = _ffn_down_emit(act, w_down, h, h_ctx if i == 0 else None, mod, 0, gains, mod, 1, rows)

        if kind == 0:
            z = _gmlp(xn, w_in, a_v_gain[j].astype(F32)[None], a_w_s[j].astype(BF16), a_b_s[j].astype(F32).T, rows)
            h, xn = _mm_res(z, None, w_out, h, mod, gains, rows)
        elif kind == 1:
            head_gains = jnp.stack([b_q_gain[j], b_k_gain[j]]).astype(F32)
            qkv = _qkv(xn, w_in, head_gains, rows)
            o_lat, o_ctx = _attention(qkv, _attn_bias_table(b_rpb[j].astype(F32)))
            h, xn = _mm_res(o_lat, o_ctx, w_out, h, mod, gains, rows)
        else:
            u = _proj(xn, w_in, rows)
            y_lat, y_ctx = _s5_scan(u, *_s5_weights(c_a_re[j], c_a_im[j], c_log_dt[j], c_b_re[j], c_b_im[j],
                                                    c_c_re[j], c_c_im[j]))
            h, xn = _glu(u, y_lat, y_ctx, c_d[j].astype(F32)[None], w_out, h, mod, gains, rows)

        act, w_down = _ffn_up(xn, ffn_w_gu, ffn_w_down, i, 1, rows)[:2]
        if last:
            h = _ffn_down(act, w_down, h, mod, 2, rows)
        else:
            h, xn = _ffn_down_emit(act, w_down, h, None, mod, 2, norm_g[i + 1], mods[i + 1], 0, rows)

    return h[:N_LAT].reshape(BATCH, SEQ, D).astype(x.dtype)
```

```python
import functools

import numpy as np
import jax
import jax.numpy as jnp
from jax import lax
from jax.experimental import pallas as pl
from jax.experimental.pallas import tpu as pltpu

F32 = jnp.float32
BF16 = jnp.bfloat16
HIGHEST = lax.Precision.HIGHEST

D = 2048
BATCH = 2
SEQ = 4096
CTX = 256
DEPTH = 4
N_LAT = BATCH * SEQ
N_CTX = BATCH * CTX
N_ALL = N_LAT + N_CTX
N_MOD = 9
D_FF = 5632
RMS_EPS = 1e-6
NEG_INF = -1e30
GRID_W = 64
ROWS = SEQ // GRID_W
CHUNK = 128
A_GROUPS = 16
N_HEADS = 16
HEAD_DIM = 128
WIN_H = 8
WIN_W = 16
C_GROUP = 16
C_GROUPS = D // C_GROUP
C_STATE = 64
LANES = 128
NORM_ROWS = 16
NORM_UNROLL = 4

TM = 512
TM_F = 1024
TF = 256
FF_SPLIT = 2
CAST_ROWS = 64
TM_E = 256
TN = 512
ADA_TN = 2048
S5_Q = 16
S5_GB = LANES // C_GROUP
S5_XW = S5_Q * C_GROUP
VMEM_LIMIT = 56 * 1024 * 1024


def _params(*sem):
    return pltpu.CompilerParams(dimension_semantics=sem, vmem_limit_bytes=VMEM_LIMIT)


def _seg(i, tm=TM):
    return jnp.minimum((i * tm) // SEQ, 2)


def _mod_row(mod_ref, r):
    return mod_ref[0, r:r + 1, :]


def _norm_mod_store(h_ref, xn_ref, m, g_ref, mod_ref, s):
    gs = g_ref[s:s + 1, :] * (1.0 + _mod_row(mod_ref, 3 * s + 1))
    shift = _mod_row(mod_ref, 3 * s)

    def chunk(r, carry):
        rows = pl.ds(pl.multiple_of(r * NORM_ROWS, NORM_ROWS), NORM_ROWS)
        x = h_ref[rows, :]
        ms = jnp.mean(x * x, axis=-1, keepdims=True)
        xn_ref[rows, :] = (x * lax.rsqrt(ms + RMS_EPS) * gs + shift).astype(BF16)
        return carry

    lax.fori_loop(0, m // NORM_ROWS, chunk, 0, unroll=NORM_UNROLL)


def _dot(a, b):
    return jnp.dot(a, b, preferred_element_type=F32)


def _dot_nt(a, b):
    return lax.dot_general(a, b, (((1,), (1,)), ((), ())), preferred_element_type=F32)


def _ada_kernel(c_ref, wa_ref, wb_ref, b_ref, o_ref):
    c = c_ref[...]
    a = (c * jax.nn.sigmoid(c)).astype(BF16)
    half = ADA_TN // 2
    o_ref[0, :, :half] = _dot(a, wa_ref[0].astype(BF16)) + b_ref[0, :, :half]
    o_ref[0, :, half:] = _dot(a, wb_ref[0].astype(BF16)) + b_ref[0, :, half:]


def _adaln(cond8, w_ada, b_ada):
    n = N_MOD * D
    half = ADA_TN // 2
    out = pl.pallas_call(
        _ada_kernel,
        grid=(DEPTH, n // ADA_TN),
        in_specs=[
            pl.BlockSpec((8, D), lambda l, j: (0, 0)),
            pl.BlockSpec((1, D, half), lambda l, j: (l, 0, 2 * j)),
            pl.BlockSpec((1, D, half), lambda l, j: (l, 0, 2 * j + 1)),
            pl.BlockSpec((1, 1, ADA_TN), lambda l, j: (l, 0, j)),
        ],
        out_specs=pl.BlockSpec((1, 8, ADA_TN), lambda l, j: (l, 0, j)),
        out_shape=jax.ShapeDtypeStruct((DEPTH, 8, n), F32),
        compiler_params=_params("parallel", "parallel"),
        name="adaln",
    )(cond8, w_ada, w_ada, b_ada.reshape(DEPTH, 1, n))
    return out[:, :3].reshape(DEPTH, 3, N_MOD, D)


def _prenorm_kernel(hl_ref, hc_ref, mod_ref, g_ref, o_ref, *, s, n_lat):
    i = pl.program_id(0)

    @pl.when(i < n_lat)
    def _():
        _norm_mod_store(hl_ref, o_ref, TM, g_ref, mod_ref, s)

    @pl.when(i >= n_lat)
    def _():
        _norm_mod_store(hc_ref, o_ref, TM, g_ref, mod_ref, s)


def _prenorm(h_lat, h_ctx, mod, gains, s):
    n_lat = N_LAT // TM
    return pl.pallas_call(
        functools.partial(_prenorm_kernel, s=s, n_lat=n_lat),
        grid=(N_ALL // TM,),
        in_specs=_split_specs(n_lat, TM) + [
            pl.BlockSpec((1, N_MOD, D), lambda i: (_seg(i), 0, 0)),
            pl.BlockSpec((3, D), lambda i: (0, 0)),
        ],
        out_specs=pl.BlockSpec((TM, D), lambda i: (i, 0)),
        out_shape=jax.ShapeDtypeStruct((N_ALL, D), BF16),
        compiler_params=_params("parallel"),
        name="ffn_prenorm",
    )(h_lat, h_ctx, mod, gains)


def _ffn_up_kernel(*refs, subtiles, n_extra):
    xn_ref, wg_ref, wu_ref, wd_ref = refs[:4]
    o_ref, wdb_ref = refs[4 + n_extra:6 + n_extra]
    for src_ref, dst_ref in zip(refs[4:4 + n_extra], refs[6 + n_extra:]):
        dst_ref[...] = src_ref[0].astype(BF16)
    wdb_ref[...] = wd_ref[0, 0].astype(BF16)
    wg = wg_ref[0, 0].astype(BF16)
    wu = wu_ref[0, 0].astype(BF16)
    for r0, m in subtiles:
        xn = xn_ref[r0:r0 + m, :]
        g = _dot(xn, wg)
        u = _dot(xn, wu)
        o_ref[r0:r0 + m, :] = ((g * jax.nn.sigmoid(g)) * u).astype(BF16)


def _ffn_up(xn, w_gu, w_down, layer, half, rows, extras=()):
    nk = D_FF // TF
    wd_blk = lambda r, k: jnp.where(r == 0, k, nk - 1)
    assert D // CAST_ROWS <= FF_SPLIT * nk
    ex_blk = lambda r, k: jnp.minimum(r * nk + k, D // CAST_ROWS - 1)
    hr = rows // FF_SPLIT
    subtiles = [(r0, min(TM_F, hr - r0)) for r0 in range(0, hr, TM_F)]
    ex_in, ex_out, ex_shape = [], [], []
    for w, idx in extras:
        cols = w.shape[-1]
        ex_in.append(pl.BlockSpec((1, CAST_ROWS, cols), lambda r, k, idx=idx: (idx, ex_blk(r, k), 0)))
        ex_out.append(pl.BlockSpec((CAST_ROWS, cols), lambda r, k: (ex_blk(r, k), 0)))
        ex_shape.append(jax.ShapeDtypeStruct((D, cols), BF16))
    return pl.pallas_call(
        functools.partial(_ffn_up_kernel, subtiles=subtiles, n_extra=len(extras)),
        grid=(FF_SPLIT, nk),
        in_specs=[
            pl.BlockSpec((hr, D), lambda r, k: (r, 0), pipeline_mode=pl.Buffered(1)),
            pl.BlockSpec((1, 1, D, TF), lambda r, k: (layer, half, 0, k)),
            pl.BlockSpec((1, 1, D, TF), lambda r, k: (layer, half, 0, nk + k)),
            pl.BlockSpec((1, 1, TF, D), lambda r, k: (layer, half, wd_blk(r, k), 0)),
        ] + ex_in,
        out_specs=[pl.BlockSpec((hr, TF), lambda r, k: (r, k)),
                   pl.BlockSpec((TF, D), lambda r, k: (wd_blk(r, k), 0))] + ex_out,
        out_shape=[jax.ShapeDtypeStruct((rows, D_FF), BF16), jax.ShapeDtypeStruct((D_FF, D), BF16)] + ex_shape,
        compiler_params=_params("arbitrary", "arbitrary"),
        name="ffn_up",
    )(xn, w_gu, w_gu, w_down, *[w for w, _ in extras])


def _ffn_down_kernel(a_ref, wd_ref, h_ref, mod_ref, o_ref, *, s):
    a = a_ref[...]
    gate = 0.5 * _mod_row(mod_ref, 3 * s + 2)
    for j in range(D // TN):
        cs = slice(j * TN, (j + 1) * TN)
        o_ref[:, cs] = h_ref[:, cs] + gate[:, cs] * _dot(a, wd_ref[:, cs])


def _ffn_down(act, w_down, h, mod, s, rows):
    tile = pl.BlockSpec((TM, D), lambda i: (i, 0))
    return pl.pallas_call(
        functools.partial(_ffn_down_kernel, s=s),
        grid=(rows // TM,),
        in_specs=[
            pl.BlockSpec((TM, D_FF), lambda i: (i, 0)),
            _resident((D_FF, D)),
            tile,
            pl.BlockSpec((1, N_MOD, D), lambda i: (_seg(i), 0, 0)),
        ],
        out_specs=tile,
        out_shape=jax.ShapeDtypeStruct((rows, D), F32),
        compiler_params=_params("arbitrary"),
        name="ffn_down",
    )(act, w_down, h, mod)


def _emit_tile(y_buf, o_ref, xn_ref, gn_ref, modn_ref, sn, tm):
    gs = gn_ref[sn:sn + 1, :] * (1.0 + _mod_row(modn_ref, 3 * sn + 1))
    shift = _mod_row(modn_ref, 3 * sn)
    for r in range(tm // NORM_ROWS):
        rows = slice(r * NORM_ROWS, (r + 1) * NORM_ROWS)
        y = y_buf[rows, :]
        o_ref[rows, :] = y
        ms = jnp.mean(y * y, axis=-1, keepdims=True)
        xn_ref[rows, :] = (y * lax.rsqrt(ms + RMS_EPS) * gs + shift).astype(BF16)


def _lagged_steps(n, compute, emit, y_a, y_b):
    j = pl.program_id(0)
    last = y_a if (n - 1) % 2 == 0 else y_b

    @pl.when(j == 0)
    def _():
        compute(y_a)

    @pl.when((j > 0) & (j < n) & (j % 2 == 1))
    def _():
        emit(y_a)
        compute(y_b)

    @pl.when((j > 0) & (j < n) & (j % 2 == 0))
    def _():
        emit(y_b)
        compute(y_a)

    @pl.when(j == n)
    def _():
        emit(last)


def _lag_specs(n, tm):
    cur = lambda j: jnp.minimum(j, n - 1)
    prev = lambda j: jnp.maximum(j - 1, 0)
    return cur, prev


def _ffn_down_emit_kernel(*refs, s, sn, n, n_lat):
    if n_lat is None:
        a_ref, wd_ref, h_ref, mod_ref, gn_ref, modn_ref, o_ref, xn_ref, y_a, y_b = refs
    else:
        a_ref, wd_ref, h_ref, hc_ref, mod_ref, gn_ref, modn_ref, o_ref, xn_ref, y_a, y_b = refs

    def compute(y_buf):
        a = a_ref[...]
        gate = 0.5 * _mod_row(mod_ref, 3 * s + 2)
        for c in range(D // TN):
            cs = slice(c * TN, (c + 1) * TN)
            if n_lat is None:
                res = h_ref[:, cs]
            else:
                res = jnp.where(pl.program_id(0) >= n_lat, hc_ref[:, cs], h_ref[:, cs])
            y_buf[:, cs] = res + gate[:, cs] * _dot(a, wd_ref[:, cs])

    def emit(y_buf):
        _emit_tile(y_buf, o_ref, xn_ref, gn_ref, modn_ref, sn, TM_E)

    _lagged_steps(n, compute, emit, y_a, y_b)


def _ffn_down_emit(act, w_down, h, h_ctx, mod, s, gains_n, mod_n, sn, rows):
    n = rows // TM_E
    cur, prev = _lag_specs(n, TM_E)
    n_lat = None if h_ctx is None else N_LAT // TM_E
    if h_ctx is None:
        h_specs, hs = [pl.BlockSpec((TM_E, D), lambda j: (cur(j), 0))], (h,)
    else:
        h_specs, hs = _split_specs(n_lat, TM_E, cur), (h, h_ctx)
    return pl.pallas_call(
        functools.partial(_ffn_down_emit_kernel, s=s, sn=sn, n=n, n_lat=n_lat),
        grid=(n + 1,),
        in_specs=[
            pl.BlockSpec((TM_E, D_FF), lambda j: (cur(j), 0)),
            _resident((D_FF, D)),
        ] + h_specs + [
            pl.BlockSpec((1, N_MOD, D), lambda j: (_seg(cur(j), TM_E), 0, 0)),
            pl.BlockSpec((3, D), lambda j: (0, 0)),
            pl.BlockSpec((1, N_MOD, D), lambda j: (_seg(prev(j), TM_E), 0, 0)),
        ],
        out_specs=[pl.BlockSpec((TM_E, D), lambda j: (prev(j), 0))] * 2,
        out_shape=[jax.ShapeDtypeStruct((rows, D), F32), jax.ShapeDtypeStruct((rows, D), BF16)],
        scratch_shapes=[pltpu.VMEM((TM_E, D), F32)] * 2,
        compiler_params=_params("arbitrary"),
        name="ffn_down_emit",
    )(act, w_down, *hs, mod, gains_n, mod_n)


def _mm_res_kernel(*refs, n, n_lat):
    if n_lat is None:
        z_ref, w_ref, h_ref, mod_ref, gn_ref, modn_ref, o_ref, xn_ref, y_a, y_b = refs
    else:
        zl_ref, zc_ref, w_ref, h_ref, mod_ref, gn_ref, modn_ref, o_ref, xn_ref, y_a, y_b = refs

    def compute(y_buf):
        if n_lat is None:
            z = z_ref[...]
        else:
            z = jnp.where(pl.program_id(0) >= n_lat, zc_ref[...], zl_ref[...])
        gate = _mod_row(mod_ref, 5)
        for c in range(D // TN):
            cs = slice(c * TN, (c + 1) * TN)
            y_buf[:, cs] = h_ref[:, cs] + gate[:, cs] * _dot(z, w_ref[:, cs])

    def emit(y_buf):
        _emit_tile(y_buf, o_ref, xn_ref, gn_ref, modn_ref, 2, TM)

    _lagged_steps(n, compute, emit, y_a, y_b)


def _split_specs(n_lat, tm, tile_of=lambda j: j):
    return [pl.BlockSpec((tm, D), lambda j: (jnp.minimum(tile_of(j), n_lat - 1), 0)),
            pl.BlockSpec((tm, D), lambda j: (jnp.maximum(tile_of(j) - n_lat, 0), 0))]


def _mm_res(z, z_ctx, w, h, mod, gains, rows):
    n = rows // TM
    cur, prev = _lag_specs(n, TM)
    n_lat = None if z_ctx is None else N_LAT // TM
    if z_ctx is None:
        z_specs, zs = [pl.BlockSpec((TM, D), lambda j: (cur(j), 0))], (z,)
    else:
        z_specs, zs = _split_specs(n_lat, TM, cur), (z, z_ctx)
    return pl.pallas_call(
        functools.partial(_mm_res_kernel, n=n, n_lat=n_lat),
        grid=(n + 1,),
        in_specs=z_specs + [
            _resident((D, D)),
            pl.BlockSpec((TM, D), lambda j: (cur(j), 0)),
            pl.BlockSpec((1, N_MOD, D), lambda j: (_seg(cur(j), TM), 0, 0)),
            pl.BlockSpec((3, D), lambda j: (0, 0)),
            pl.BlockSpec((1, N_MOD, D), lambda j: (_seg(prev(j), TM), 0, 0)),
        ],
        out_specs=[pl.BlockSpec((TM, D), lambda j: (prev(j), 0))] * 2,
        out_shape=[jax.ShapeDtypeStruct((rows, D), F32), jax.ShapeDtypeStruct((rows, D), BF16)],
        scratch_shapes=[pltpu.VMEM((TM, D), F32)] * 2,
        compiler_params=_params("arbitrary"),
        name="mm_res",
    )(*zs, w, h, mod, gains, mod)


def _gmlp_kernel(xn_ref, win_ref, vg_ref, ws_ref, bst_ref, z_ref, y_ref, *, nj):
    xn = xn_ref[...]
    for j in range(nj):
        y_ref[j] = jax.nn.gelu(_dot(xn, win_ref[:, j * TN:(j + 1) * TN]))

    half = nj // 2
    ssq = jnp.zeros((TM, 1), F32)
    for jj in range(half, nj):
        yv = y_ref[jj]
        ssq = ssq + jnp.sum(yv * yv, axis=-1, keepdims=True)
    inv = lax.rsqrt(ssq / D + RMS_EPS)
    gpb = TN // CHUNK
    for cb in range(half):
        v = (y_ref[half + cb] * inv * vg_ref[:, cb * TN:(cb + 1) * TN]).astype(BF16)
        u = y_ref[cb]
        for gg in range(gpb):
            grp = cb * gpb + gg
            cs = slice(gg * CHUNK, (gg + 1) * CHUNK)
            for c in range(TM // CHUNK):
                rs = slice(c * CHUNK, (c + 1) * CHUNK)
                sg = _dot(ws_ref[grp], v[rs, cs]) + bst_ref[:, grp:grp + 1]
                z_ref[rs, grp * CHUNK:(grp + 1) * CHUNK] = (u[rs, cs] * sg).astype(BF16)


def _resident(shape):
    return pl.BlockSpec(shape, lambda *_: (0,) * len(shape), pipeline_mode=pl.Buffered(1))


def _gmlp(xn, w_in, v_gain, w_s, b_s_t, rows):
    nj = (2 * D) // TN
    return pl.pallas_call(
        functools.partial(_gmlp_kernel, nj=nj),
        grid=(rows // TM,),
        in_specs=[
            pl.BlockSpec((TM, D), lambda i: (i, 0)),
            _resident((D, 2 * D)),
            pl.BlockSpec((1, D), lambda i: (0, 0)),
            _resident((A_GROUPS, CHUNK, CHUNK)),
            pl.BlockSpec((CHUNK, A_GROUPS), lambda i: (0, 0)),
        ],
        out_specs=pl.BlockSpec((TM, D), lambda i: (i, 0)),
        out_shape=jax.ShapeDtypeStruct((rows, D), BF16),
        scratch_shapes=[pltpu.VMEM((nj, TM, TN), F32)],
        compiler_params=_params("arbitrary"),
        name="gmlp",
    )(xn, w_in, v_gain, w_s, b_s_t)


def _qkv_kernel(xn_ref, w_ref, hg_ref, o_ref):
    xn = xn_ref[...]
    for j in range((3 * D) // TN):
        y = _dot(xn, w_ref[:, j * TN:(j + 1) * TN])
        part = j // (D // TN)
        if part == 2:
            o_ref[:, j * TN:(j + 1) * TN] = y.astype(BF16)
            continue
        gain = hg_ref[part:part + 1, :]
        for hh in range(TN // HEAD_DIM):
            yh = y[:, hh * HEAD_DIM:(hh + 1) * HEAD_DIM]
            ms = jnp.mean(yh * yh, axis=-1, keepdims=True)
            c0 = j * TN + hh * HEAD_DIM
            o_ref[:, c0:c0 + HEAD_DIM] = (yh * lax.rsqrt(ms + RMS_EPS) * gain).astype(BF16)


def _qkv(xn, w_qkv, head_gains, rows):
    return pl.pallas_call(
        _qkv_kernel,
        grid=(rows // TM,),
        in_specs=[
            pl.BlockSpec((TM, D), lambda i: (i, 0)),
            _resident((D, 3 * D)),
            pl.BlockSpec((2, HEAD_DIM), lambda i: (0, 0)),
        ],
        out_specs=pl.BlockSpec((TM, 3 * D), lambda i: (i, 0)),
        out_shape=jax.ShapeDtypeStruct((rows, 3 * D), BF16),
        compiler_params=_params("arbitrary"),
        name="qkv",
    )(xn, w_qkv, head_gains)


ATT_QR = 8
ATT_KR = ATT_QR + WIN_H
ATT_QB = ATT_QR * GRID_W
ATT_KB = ATT_KR * GRID_W
N_DR = 2 * WIN_H - 1
ATT_PATTERNS = ((0, 0), (ATT_QR, ATT_QR - WIN_H // 2), (ROWS - ATT_QR, ROWS - ATT_KR))


def _attn_kernel(q_ref, k_ref, v_ref, qc_ref, kc_ref, vc_ref, t_ref, o_ref, oc_ref, bias_ref):
    scale = HEAD_DIM ** -0.5
    left = lax.broadcasted_iota(jnp.int32, (GRID_W, 2 * GRID_W), 1) < GRID_W
    neg = jnp.full((GRID_W, 2 * GRID_W), NEG_INF, F32)

    for p, (r0, kr_base) in enumerate(ATT_PATTERNS):
        for qr in range(ATT_QR):
            r = r0 + qr
            rstart = min(max(r - WIN_H // 2, 0), ROWS - WIN_H)
            for kp in range(ATT_KR // 2):
                halves = []
                for kr in (kr_base + 2 * kp, kr_base + 2 * kp + 1):
                    inside = rstart <= kr < rstart + WIN_H
                    halves.append(t_ref[0, kr - r + WIN_H - 1] if inside else None)
                a, b = halves
                if a is None and b is None:
                    blk = neg
                else:
                    blk = jnp.where(left, neg if a is None else a, neg if b is None else b)
                bias_ref[p, qr * GRID_W:(qr + 1) * GRID_W, kp * 2 * GRID_W:(kp + 1) * 2 * GRID_W] = blk

    kc = kc_ref[...]
    vc = vc_ref[...]

    def block(q0, k0, p):
        q = q_ref[pl.ds(q0, ATT_QB), :]
        k = k_ref[pl.ds(k0, ATT_KB), :]
        v = v_ref[pl.ds(k0, ATT_KB), :]
        s_win = _dot_nt(q, k) * scale + bias_ref[p]
        s_ctx = _dot_nt(q, kc) * scale
        m = jnp.maximum(jnp.max(s_win, axis=-1, keepdims=True), jnp.max(s_ctx, axis=-1, keepdims=True))
        p_win = jnp.exp(s_win - m)
        p_ctx = jnp.exp(s_ctx - m)
        denom = jnp.sum(p_win, axis=-1, keepdims=True) + jnp.sum(p_ctx, axis=-1, keepdims=True)
        o = _dot(p_win.astype(BF16), v) + _dot(p_ctx.astype(BF16), vc)
        o_ref[pl.ds(q0, ATT_QB), :] = (o / denom).astype(BF16)

    block(0, 0, 0)
    for rb in range(1, ROWS // ATT_QR - 1):
        block(rb * ATT_QB, rb * ATT_QB - (WIN_H // 2) * GRID_W, 1)
    block(SEQ - ATT_QB, SEQ - ATT_KB, 2)

    s = _dot_nt(qc_ref[...], kc) * scale
    pc = jnp.exp(s - jnp.max(s, axis=-1, keepdims=True))
    oc = _dot(pc.astype(BF16), vc) / jnp.sum(pc, axis=-1, keepdims=True)
    oc_ref[...] = oc.astype(BF16)


def _attention(qkv, bias_tab):
    lat_blk = (SEQ, HEAD_DIM)
    ctx_blk = (CTX, HEAD_DIM)
    ctx0 = N_LAT // CTX
    return pl.pallas_call(
        _attn_kernel,
        grid=(N_HEADS, BATCH),
        in_specs=[
            pl.BlockSpec(lat_blk, lambda h, b: (b, h)),
            pl.BlockSpec(lat_blk, lambda h, b: (b, N_HEADS + h)),
            pl.BlockSpec(lat_blk, lambda h, b: (b, 2 * N_HEADS + h)),
            pl.BlockSpec(ctx_blk, lambda h, b: (ctx0 + b, h)),
            pl.BlockSpec(ctx_blk, lambda h, b: (ctx0 + b, N_HEADS + h)),
            pl.BlockSpec(ctx_blk, lambda h, b: (ctx0 + b, 2 * N_HEADS + h)),
            pl.BlockSpec((1, N_DR, GRID_W, 2 * GRID_W), lambda h, b: (h, 0, 0, 0)),
        ],
        out_specs=[
            pl.BlockSpec(lat_blk, lambda h, b: (b, h)),
            pl.BlockSpec(ctx_blk, lambda h, b: (b, h)),
        ],
        out_shape=[
            jax.ShapeDtypeStruct((N_LAT, D), BF16),
            jax.ShapeDtypeStruct((N_CTX, D), BF16),
        ],
        scratch_shapes=[pltpu.VMEM((len(ATT_PATTERNS), ATT_QB, ATT_KB), F32)],
        compiler_params=_params("parallel", "parallel"),
        name="nat_attention",
    )(qkv, qkv, qkv, qkv, qkv, qkv, bias_tab)


def _attn_bias_table(rpb):
    qcol = np.arange(GRID_W)[:, None]
    kcol = np.arange(2 * GRID_W)[None, :] % GRID_W
    cstart = np.clip(qcol - WIN_W // 2, 0, GRID_W - WIN_W)
    col_valid = (kcol >= cstart) & (kcol < cstart + WIN_W)
    dc_idx = np.clip(kcol - qcol, 1 - WIN_W, WIN_W - 1) + (WIN_W - 1)
    onehot = (dc_idx[None] == np.arange(2 * WIN_W - 1)[:, None, None]) & col_valid[None]
    onehot = jnp.asarray(onehot.reshape(2 * WIN_W - 1, -1), F32)
    mask = jnp.asarray(np.where(col_valid, 0.0, NEG_INF).reshape(-1), F32)
    t = jnp.dot(rpb.reshape(N_HEADS * N_DR, 2 * WIN_W - 1), onehot, precision=HIGHEST) + mask
    return t.reshape(N_HEADS, N_DR, GRID_W, 2 * GRID_W)


def _proj_kernel(xn_ref, w_ref, o_ref):
    xn = xn_ref[...]
    for j in range(D // TN):
        o_ref[:, j * TN:(j + 1) * TN] = _dot(xn, w_ref[:, j * TN:(j + 1) * TN])


def _proj(xn, w, rows):
    return pl.pallas_call(
        _proj_kernel,
        grid=(rows // TM,),
        in_specs=[
            pl.BlockSpec((TM, D), lambda i: (i, 0)),
            _resident((D, D)),
        ],
        out_specs=pl.BlockSpec((TM, D), lambda i: (i, 0)),
        out_shape=jax.ShapeDtypeStruct((rows, D), F32),
        compiler_params=_params("arbitrary"),
        name="s5_in_proj",
    )(xn, w)


S5_CL = SEQ // S5_Q
S5_CC = CTX // S5_Q
S5_HALF = LANES // C_GROUP


def _s5_kernel(ul_ref, uc_ref, mi_ref, ms_ref, mo_ref, aq_ref, yl_ref, yc_ref,
               xl_scr, xc_scr, s_scr, ssw_scr, hp_scr, yl_acc, yc_acc):
    d = pl.program_id(2)
    ns = 2 * C_STATE

    def granule_transpose(rows):
        n = rows[0].shape[0]
        granule = lax.broadcasted_iota(jnp.int32, (n, LANES), 1) // C_GROUP
        rows = list(rows)
        k = S5_GB // 2
        while k:
            hi = (granule & k) != 0
            for i in range(S5_GB):
                if i & k:
                    continue
                a, b = rows[i], rows[i + k]
                rows[i] = jnp.where(hi, pltpu.roll(b, k * C_GROUP, 1), a)
                rows[i + k] = jnp.where(hi, b, pltpu.roll(a, LANES - k * C_GROUP, 1))
            k //= 2
        return rows

    def to_chunks(u_ref, x_scr, nch):
        for hf in range(S5_Q // S5_HALF):
            by_token = [u_ref[pl.ds(hf * S5_HALF + tl, nch, stride=S5_Q), :] for tl in range(S5_HALF)]
            for gl, x in enumerate(granule_transpose(by_token)):
                x_scr[gl, :, hf * LANES:(hf + 1) * LANES] = x.astype(BF16)

    def from_chunks(y_acc, y_ref, nch):
        for hf in range(S5_Q // S5_HALF):
            by_group = [y_acc[gl, :, hf * LANES:(hf + 1) * LANES] for gl in range(S5_GB)]
            for tl, y in enumerate(granule_transpose(by_group)):
                y_ref[pl.ds(hf * S5_HALF + tl, nch, stride=S5_Q), :] = y

    @pl.when(d == 0)
    def _():
        to_chunks(ul_ref, xl_scr, S5_CL)
        to_chunks(uc_ref, xc_scr, S5_CC)
        yl_acc[...] = jnp.zeros_like(yl_acc)
        yc_acc[...] = jnp.zeros_like(yc_acc)

    lat0 = S5_CC * S5_GB
    for j in range(S5_GB):
        xl, xc = xl_scr[j], xc_scr[j]
        yl_acc[j] += _dot(xl, mi_ref[0, j])
        yc_acc[j] += _dot(xc, mi_ref[0, j])
        sl = _dot(xl, ms_ref[0, j])
        sc = _dot(xc, ms_ref[0, j])
        s_scr[pl.ds(j, S5_CC, stride=S5_GB), :] = sc[:, :ns]
        s_scr[pl.ds(lat0 + j, S5_CL, stride=S5_GB), :] = sl[:, :ns]
        ssw_scr[pl.ds(j, S5_CC, stride=S5_GB), :] = sc[:, ns:]
        ssw_scr[pl.ds(lat0 + j, S5_CL, stride=S5_GB), :] = sl[:, ns:]

    a1 = aq_ref[0, 0]
    a2 = aq_ref[0, 1]
    a3 = aq_ref[0, 2]

    def step(pos, carry):
        hs, hsw = carry
        row = pl.multiple_of(pos * S5_GB, S5_GB)
        hp_scr[pl.ds(row, S5_GB), :] = hs
        s = s_scr[pl.ds(row, S5_GB), :]
        ssw = ssw_scr[pl.ds(row, S5_GB), :]
        return hs * a1 + hsw * a2 + s, hsw * a1 + hs * a3 + ssw

    def ctx_step(i, carry):
        return step(jnp.where(d == 0, i, S5_CC - 1 - i), carry)

    def lat_step(i, carry):
        return step(S5_CC + jnp.where(d == 0, i, S5_CL - 1 - i), carry)

    zero = jnp.zeros((S5_GB, ns), F32)
    carry = lax.fori_loop(0, S5_CC, ctx_step, (zero, zero))
    lax.fori_loop(0, S5_CL, lat_step, carry, unroll=4)

    for j in range(S5_GB):
        mo = mo_ref[0, j]
        hc = hp_scr[pl.ds(j, S5_CC, stride=S5_GB), :]
        hl = hp_scr[pl.ds(lat0 + j, S5_CL, stride=S5_GB), :]
        yc_acc[j] += _dot(hc.astype(BF16), mo)
        yl_acc[j] += _dot(hl.astype(BF16), mo)

    @pl.when(d == 1)
    def _():
        from_chunks(yl_acc, yl_ref, S5_CL)
        from_chunks(yc_acc, yc_ref, S5_CC)


def _s5_scan(u, mi, ms, mo, aq):
    ns = 2 * C_STATE
    nrow = (S5_CC + S5_CL) * S5_GB
    ctx0 = N_LAT // CTX
    return pl.pallas_call(
        _s5_kernel,
        grid=(BATCH, C_GROUPS // S5_GB, 2),
        in_specs=[
            pl.BlockSpec((SEQ, LANES), lambda b, g, d: (b, g)),
            pl.BlockSpec((CTX, LANES), lambda b, g, d: (ctx0 + b, g)),
            pl.BlockSpec((1, S5_GB, S5_XW, S5_XW), lambda b, g, d: (d, g, 0, 0)),
            pl.BlockSpec((1, S5_GB, S5_XW, 2 * ns), lambda b, g, d: (d, g, 0, 0)),
            pl.BlockSpec((1, S5_GB, ns, S5_XW), lambda b, g, d: (d, g, 0, 0)),
            pl.BlockSpec((1, 3, S5_GB, ns), lambda b, g, d: (d, 0, g, 0)),
        ],
        out_specs=[
            pl.BlockSpec((SEQ, LANES), lambda b, g, d: (b, g)),
            pl.BlockSpec((CTX, LANES), lambda b, g, d: (b, g)),
        ],
        out_shape=[
            jax.ShapeDtypeStruct((N_LAT, D), F32),
            jax.ShapeDtypeStruct((N_CTX, D), F32),
        ],
        scratch_shapes=[
            pltpu.VMEM((S5_GB, S5_CL, S5_XW), BF16),
            pltpu.VMEM((S5_GB, S5_CC, S5_XW), BF16),
            pltpu.VMEM((nrow, ns), F32),
            pltpu.VMEM((nrow, ns), F32),
            pltpu.VMEM((nrow, ns), F32),
            pltpu.VMEM((S5_GB, S5_CL, S5_XW), F32),
            pltpu.VMEM((S5_GB, S5_CC, S5_XW), F32),
        ],
        compiler_params=_params("arbitrary", "arbitrary", "arbitrary"),
        name="s5_scan",
    )(u, u, mi, ms, mo, aq)


def _lag_table_kernel(b_ref, c_ref, o_ref):
    for g in range(S5_GB):
        o_ref[0, g] = jnp.dot(b_ref[0, g], c_ref[0, g], preferred_element_type=F32, precision=HIGHEST)


def _lag_table(bt, ck):
    ns = 2 * C_STATE
    return pl.pallas_call(
        _lag_table_kernel,
        grid=(2, C_GROUPS // S5_GB),
        in_specs=[
            pl.BlockSpec((1, S5_GB, C_GROUP, ns), lambda d, g: (d, g, 0, 0)),
            pl.BlockSpec((1, S5_GB, ns, S5_XW), lambda d, g: (d, g, 0, 0)),
        ],
        out_specs=pl.BlockSpec((1, S5_GB, C_GROUP, S5_XW), lambda d, g: (d, g, 0, 0)),
        out_shape=jax.ShapeDtypeStruct((2, C_GROUPS, C_GROUP, S5_XW), F32),
        compiler_params=_params("parallel", "parallel"),
        name="s5_lag_table",
    )(bt, ck)


def _s5_weights(a_re, a_im, log_dt, b_re, b_im, c_re, c_im):
    q = S5_Q
    a_re, a_im, b_re, b_im, c_re, c_im = (v.astype(F32) for v in (a_re, a_im, b_re, b_im, c_re, c_im))
    dt = jnp.exp(log_dt.astype(F32))[..., None]
    zr, zi = a_re * dt, a_im * dt
    ab_r, ab_i = jnp.exp(zr) * jnp.cos(zi), jnp.exp(zr) * jnp.sin(zi)
    den = a_re * a_re + a_im * a_im
    f_r = ((ab_r - 1.0) * a_re + ab_i * a_im) / den
    f_i = (ab_i * a_re - (ab_r - 1.0) * a_im) / den
    bb_r = f_r[..., None] * b_re - f_i[..., None] * b_im
    bb_i = f_r[..., None] * b_im + f_i[..., None] * b_re

    taus = jnp.arange(q + 1, dtype=F32)[:, None]
    mag = jnp.exp(zr[:, :, None] * taus)
    pw_r, pw_i = mag * jnp.cos(zi[:, :, None] * taus), mag * jnp.sin(zi[:, :, None] * taus)

    col = np.arange(S5_XW)
    e_t = jnp.asarray(col[None, :] // C_GROUP == np.arange(q)[:, None], F32)
    e_o = jnp.asarray(col[None, :] % C_GROUP == np.arange(C_GROUP)[:, None], F32)
    xc_r = jnp.einsum("dgop,ox->dgpx", c_re, e_o, precision=HIGHEST)
    xc_i = jnp.einsum("dgop,ox->dgpx", c_im, e_o, precision=HIGHEST)

    def readout(p_r, p_i):
        xp_r = jnp.einsum("dgtp,tx->dgpx", p_r, e_t, precision=HIGHEST)
        xp_i = jnp.einsum("dgtp,tx->dgpx", p_i, e_t, precision=HIGHEST)
        return xc_r * xp_r - xc_i * xp_i, xc_r * xp_i + xc_i * xp_r

    wo_r, wo_i = readout(jnp.stack([pw_r[0, :, 1:q + 1], pw_r[1, :, q:0:-1]]),
                         jnp.stack([pw_i[0, :, 1:q + 1], pw_i[1, :, q:0:-1]]))
    m_out = jnp.concatenate([wo_r, -wo_i], axis=2).astype(BF16)

    ck_r, ck_i = readout(jnp.stack([pw_r[0, :, :q], pw_r[1, :, q - 1::-1]]),
                         jnp.stack([pw_i[0, :, :q], pw_i[1, :, q - 1::-1]]))
    bt_r, bt_i = bb_r.transpose(0, 1, 3, 2), bb_i.transpose(0, 1, 3, 2)
    kmat = _lag_table(jnp.concatenate([bt_r, -bt_i], axis=-1),
                      jnp.concatenate([ck_r, ck_i], axis=2))
    zpad = jnp.zeros_like(kmat[0, :, :, :(q - 1) * C_GROUP])
    kflat = jnp.stack([jnp.concatenate([zpad, kmat[0]], axis=-1), jnp.concatenate([kmat[1], zpad], axis=-1)])
    m_intra = jnp.stack([kflat[..., (q - 1 - s) * C_GROUP:(q - 1 - s) * C_GROUP + S5_XW] for s in range(q)],
                        axis=2).reshape(2, C_GROUPS, S5_XW, S5_XW).astype(BF16)

    ps_r = jnp.stack([pw_r[0, :, q - 1::-1], pw_r[1, :, :q]])
    ps_i = jnp.stack([pw_i[0, :, q - 1::-1], pw_i[1, :, :q]])
    p4_r = jnp.concatenate([ps_r] * 4, axis=-1)[:, :, :, None]
    p4_i = jnp.concatenate([ps_i] * 4, axis=-1)[:, :, :, None]
    b4_a = jnp.concatenate([bt_r, bt_i, bt_i, bt_r], axis=-1)[:, :, None]
    b4_b = jnp.concatenate([-bt_i, bt_r, bt_r, -bt_i], axis=-1)[:, :, None]
    m_state = (p4_r * b4_a + p4_i * b4_b).reshape(2, C_GROUPS, S5_XW, 4 * C_STATE).astype(BF16)

    ar, ai = pw_r[:, :, q], pw_i[:, :, q]
    aq3 = jnp.stack([jnp.concatenate([ar, ar], -1), jnp.concatenate([-ai, ai], -1),
                     jnp.concatenate([ai, -ai], -1)], axis=1)
    return m_intra, m_state, m_out, aq3


def _glu_kernel(u_ref, yl_ref, yc_ref, dsk_ref, w_ref, h_ref, mod_ref, gn_ref, modn_ref, o_ref, xn_ref,
                y_a, y_b, *, n, n_lat):
    def compute(y_buf):
        y = jnp.where(pl.program_id(0) >= n_lat, yc_ref[...], yl_ref[...])
        z = jax.nn.gelu(dsk_ref[...] * u_ref[...] + y).astype(BF16)
        gate = _mod_row(mod_ref, 5)
        for c in range(D // TN):
            cs = slice(c * TN, (c + 1) * TN)
            a = _dot(z, w_ref[:, cs])
            g = _dot(z, w_ref[:, D + c * TN:D + (c + 1) * TN])
            y_buf[:, cs] = h_ref[:, cs] + gate[:, cs] * (a * jax.nn.sigmoid(g))

    def emit(y_buf):
        _emit_tile(y_buf, o_ref, xn_ref, gn_ref, modn_ref, 2, TM_E)

    _lagged_steps(n, compute, emit, y_a, y_b)


def _glu(u, y_lat, y_ctx, d_skip, w_glu, h, mod, gains, rows):
    n = rows // TM_E
    cur, prev = _lag_specs(n, TM_E)
    n_lat = N_LAT // TM_E
    tile = pl.BlockSpec((TM_E, D), lambda j: (cur(j), 0))
    return pl.pallas_call(
        functools.partial(_glu_kernel, n=n, n_lat=n_lat),
        grid=(n + 1,),
        in_specs=[tile] + _split_specs(n_lat, TM_E, cur) + [
            pl.BlockSpec((1, D), lambda j: (0, 0)),
            _resident((D, 2 * D)),
            tile,
            pl.BlockSpec((1, N_MOD, D), lambda j: (_seg(cur(j), TM_E), 0, 0)),
            pl.BlockSpec((3, D), lambda j: (0, 0)),
            pl.BlockSpec((1, N_MOD, D), lambda j: (_seg(prev(j), TM_E), 0, 0)),
        ],
        out_specs=[pl.BlockSpec((TM_E, D), lambda j: (prev(j), 0))] * 2,
        out_shape=[jax.ShapeDtypeStruct((rows, D), F32), jax.ShapeDtypeStruct((rows, D), BF16)],
        scratch_shapes=[pltpu.VMEM((TM_E, D), F32)] * 2,
        compiler_params=_params("arbitrary"),
        name="s5_glu",
    )(u, y_lat, y_ctx, d_skip, w_glu, h, mod, gains, mod)


def kernel(x, c, ctx, c_ctx, w_ada, b_ada, norm_g, ffn_w_gu, ffn_w_down, a_w_in, a_v_gain, a_w_s, a_b_s, a_w_out, b_w_qkv, b_q_gain, b_k_gain, b_rpb, b_w_out, c_w_in, c_a_re, c_a_im, c_log_dt, c_b_re, c_b_im, c_c_re, c_c_im, c_d, c_w_glu):
    h, h_ctx = x.reshape(N_LAT, D).astype(F32), ctx.reshape(N_CTX, D).astype(F32)
    cond8 = jnp.concatenate([c, c_ctx[None], jnp.zeros((8 - BATCH - 1, D), c.dtype)], axis=0).astype(F32)
    mods = _adaln(cond8, w_ada.astype(F32), b_ada.astype(F32))
    norm_g = norm_g.astype(F32)
    ffn_w_gu = ffn_w_gu.astype(F32)
    ffn_w_down = ffn_w_down.astype(F32)

    xn = _prenorm(h, h_ctx, mods[0], norm_g[0], 0)
    for i in range(DEPTH):
        kind, j = i % 3, i // 3
        last = i == DEPTH - 1
        rows = N_LAT if last else N_ALL
        mod, gains = mods[i], norm_g[i]

        w_in, w_out = ((a_w_in, a_w_out), (b_w_qkv, b_w_out), (c_w_in, c_w_glu))[kind]
        act, w_down, w_in, w_out = _ffn_up(xn, ffn_w_gu, ffn_w_down, i, 0, rows,
                                           extras=((w_in.astype(F32), j), (w_out.astype(F32), j)))
        h, xn = _ffn_down_emit(act, w_down, h, h_ctx if i == 0 else None, mod, 0, gains, mod, 1, rows)

        if kind == 0:
            z = _gmlp(xn, w_in, a_v_gain[j].astype(F32)[None], a_w_s[j].astype(BF16), a_b_s[j].astype(F32).T, rows)
            h, xn = _mm_res(z, None, w_out, h, mod, gains, rows)
        elif kind == 1:
            head_gains = jnp.stack([b_q_gain[j], b_k_gain[j]]).astype(F32)
            qkv = _qkv(xn, w_in, head_gains, rows)
            o_lat, o_ctx = _attention(qkv, _attn_bias_table(b_rpb[j].astype(F32)))
            h, xn = _mm_res(o_lat, o_ctx, w_out, h, mod, gains, rows)
        else:
            u = _proj(xn, w_in, rows)
            y_lat, y_ctx = _s5_scan(u, *_s5_weights(c_a_re[j], c_a_im[j], c_log_dt[j], c_b_re[j], c_b_im[j],
                                                    c_c_re[j], c_c_im[j]))
            h, xn = _glu(u, y_lat, y_ctx, c_d[j].astype(F32)[None], w_out, h, mod, gains, rows)

        act, w_down = _ffn_up(xn, ffn_w_gu, ffn_w_down, i, 1, rows)[:2]
        if last:
            h = _ffn_down(act, w_down, h, mod, 2, rows)
        else:
            h, xn = _ffn_down_emit(act, w_down, h, None, mod, 2, norm_g[i + 1], mods[i + 1], 0, rows)

    return h[:N_LAT].reshape(BATCH, SEQ, D).astype(x.dtype)
```
